```python
import jax, jax.numpy as jnp
from jax import lax
import numpy as np

D_MODEL = 1024
BATCH = 8
SEQ = 4096
DEPTH = 4

N_MIXERS = 2
N_NSA_LAYERS = (DEPTH + 1) // 2
N_FOX_LAYERS = DEPTH // 2

HEAD_DIM = 64
N_HEADS = D_MODEL // HEAD_DIM
HQ = N_HEADS * HEAD_DIM

NSA_GROUPS = 4
NSA_REP = N_HEADS // NSA_GROUPS
NSA_KV = NSA_GROUPS * HEAD_DIM
CMP_LEN = 32
CMP_STRIDE = 16
SEL_LEN = 64
N_SEL = 16
WINDOW = 512
PHI_HIDDEN = 256
Q_CHUNK = 64
N_BRANCH = 3
NSA_PROJ = HQ + 6 * NSA_KV + N_BRANCH * N_HEADS
FORCE_SCORE = 1e4

FOX_BLOCK = 128
FOX_PROJ = 3 * HQ + N_HEADS

D_FF = 2816
CONV_WIDTH = 3

ROPE_THETA = 10000.0
ALPHA = (2 * DEPTH) ** 0.25
BETA = (8 * DEPTH) ** -0.25
LN_EPS = 1e-5

kernel_name = "nsa_fox_interleaved_deepnorm_convffn"


def layer_norm(x, g, b):
    xf = x.astype(jnp.float32)
    mu = jnp.mean(xf, axis=-1, keepdims=True)
    var = jnp.mean(jnp.square(xf - mu), axis=-1, keepdims=True)
    y = (xf - mu) * lax.rsqrt(var + LN_EPS)
    return (y * g.astype(jnp.float32) + b.astype(jnp.float32)).astype(x.dtype)


def rope_tables(seq, dtype):
    inv = ROPE_THETA ** (-jnp.arange(0, HEAD_DIM, 2, dtype=jnp.float32) / HEAD_DIM)
    ang = jnp.arange(seq, dtype=jnp.float32)[:, None] * inv[None, :]
    ang = jnp.concatenate([ang, ang], axis=-1)
    return jnp.cos(ang).astype(dtype), jnp.sin(ang).astype(dtype)


def apply_rope(x, cos, sin):
    x1, x2 = jnp.split(x, 2, axis=-1)
    return x * cos + jnp.concatenate([-x2, x1], axis=-1) * sin


def masked_softmax(s, mask):
    s = jnp.where(mask, s, -jnp.inf)
    m = jnp.max(s, axis=-1, keepdims=True)
    m = jnp.where(jnp.isfinite(m), m, 0.0)
    e = jnp.exp(s - m)
    d = jnp.sum(e, axis=-1, keepdims=True)
    return e / jnp.where(d > 0, d, 1.0)


def compress(kv, pe, w1, b1, w2):
    S = kv.shape[2]
    n_cmp = (S - CMP_LEN) // CMP_STRIDE + 1
    idx = jnp.arange(n_cmp)[:, None] * CMP_STRIDE + jnp.arange(CMP_LEN)[None, :]
    blocks = kv[:, :, idx] + pe
    flat = blocks.reshape(blocks.shape[0], blocks.shape[1], n_cmp, CMP_LEN * HEAD_DIM)
    return jax.nn.gelu(flat @ w1 + b1) @ w2


def nsa_mixer(x, w_in, b_gate, pe, phik_w1, phik_b1, phik_w2, phiv_w1, phiv_b1, phiv_w2, w_o):
    B, S, _ = x.shape
    G, R, hd = NSA_GROUPS, NSA_REP, HEAD_DIM
    proj = x @ w_in
    cuts = [HQ + i * NSA_KV for i in range(7)]
    q, kc, vc, ks, vs, kw, vw, gl = jnp.split(proj, cuts, axis=-1)

    def kv_heads(t):
        return t.reshape(B, S, G, hd).transpose(0, 2, 1, 3)

    q = q.reshape(B, S, G, R, hd).transpose(0, 2, 3, 1, 4)
    gates = jax.nn.sigmoid((gl + b_gate).astype(jnp.float32))
    gates = gates.reshape(B, S, G, R, N_BRANCH).transpose(0, 2, 3, 1, 4).astype(x.dtype)

    cos, sin = rope_tables(S, x.dtype)
    q_rot = apply_rope(q, cos, sin)
    k_slc = apply_rope(kv_heads(ks), cos, sin)
    v_slc = kv_heads(vs)
    k_win = apply_rope(kv_heads(kw), cos, sin)
    v_win = kv_heads(vw)

    k_cmp = compress(kv_heads(kc), pe, phik_w1, phik_b1, phik_w2)
    v_cmp = compress(kv_heads(vc), pe, phiv_w1, phiv_b1, phiv_w2)
    n_cmp = k_cmp.shape[2]
    n_sb = S // SEL_LEN
    n_sel = min(N_SEL, n_sb)

    cmp_start = jnp.arange(n_cmp) * CMP_STRIDE
    sel_start = jnp.arange(n_sb) * SEL_LEN
    overlap = jnp.clip(jnp.minimum(cmp_start[:, None] + CMP_LEN, sel_start[None, :] + SEL_LEN)
                       - jnp.maximum(cmp_start[:, None], sel_start[None, :]), 0, None
                       ).astype(jnp.float32) / CMP_LEN
    cmp_end = cmp_start + CMP_LEN - 1

    ks_blocks = k_slc.reshape(B, G, n_sb, SEL_LEN, hd)
    vs_blocks = v_slc.reshape(B, G, n_sb, SEL_LEN, hd)
    pad = ((0, 0), (0, 0), (WINDOW, 0), (0, 0))
    kw_pad = jnp.pad(k_win, pad)
    vw_pad = jnp.pad(v_win, pad)
    bi = jnp.arange(B)[:, None, None, None]
    gi = jnp.arange(G)[None, :, None, None]
    blk = jnp.arange(n_sb)
    scale = HEAD_DIM ** -0.5

    def chunk(c):
        start = c * Q_CHUNK
        t = start + jnp.arange(Q_CHUNK)
        qn = lax.dynamic_slice_in_dim(q, start, Q_CHUNK, axis=3)
        qr = lax.dynamic_slice_in_dim(q_rot, start, Q_CHUNK, axis=3)
        gc = lax.dynamic_slice_in_dim(gates, start, Q_CHUNK, axis=3)

        s_c = jnp.einsum('bgrqd,bgkd->bgrqk', qn, k_cmp).astype(jnp.float32) * scale
        p_c = masked_softmax(s_c, cmp_end[None, :] <= t[:, None])
        o_c = jnp.einsum('bgrqk,bgkd->bgrqd', p_c.astype(v_cmp.dtype), v_cmp)

        imp = jnp.einsum('bgrqk,kj->bgqj', p_c, overlap)
        cur = t // SEL_LEN
        forced = (blk[None, :] == 0) | (blk[None, :] == cur[:, None]) | (blk[None, :] == cur[:, None] - 1)
        valid = blk[None, :] * SEL_LEN <= t[:, None]
        score = jnp.where(forced, FORCE_SCORE, jnp.where(valid, imp, -1.0))
        _, sel = lax.top_k(score, n_sel)
        k_sel = ks_blocks[bi, gi, sel].reshape(B, G, Q_CHUNK, n_sel * SEL_LEN, hd)
        v_sel = vs_blocks[bi, gi, sel].reshape(B, G, Q_CHUNK, n_sel * SEL_LEN, hd)
        pos = (sel[..., None] * SEL_LEN + jnp.arange(SEL_LEN)).reshape(B, G, Q_CHUNK, n_sel * SEL_LEN)
        s_s = jnp.einsum('bgrqd,bgqkd->bgrqk', qr, k_sel).astype(jnp.float32) * scale
        p_s = masked_softmax(s_s, (pos <= t[:, None])[:, :, None])
        o_s = jnp.einsum('bgrqk,bgqkd->bgrqd', p_s.astype(v_sel.dtype), v_sel)

        k_band = lax.dynamic_slice_in_dim(kw_pad, start, WINDOW + Q_CHUNK, axis=2)
        v_band = lax.dynamic_slice_in_dim(vw_pad, start, WINDOW + Q_CHUNK, axis=2)
        spos = start - WINDOW + jnp.arange(WINDOW + Q_CHUNK)
        diff = t[:, None] - spos[None, :]
        win_mask = (spos[None, :] >= 0) & (diff >= 0) & (diff < WINDOW)
        s_w = jnp.einsum('bgrqd,bgkd->bgrqk', qr, k_band).astype(jnp.float32) * scale
        p_w = masked_softmax(s_w, win_mask)
        o_w = jnp.einsum('bgrqk,bgkd->bgrqd', p_w.astype(v_band.dtype), v_band)

        return gc[..., 0:1] * o_c + gc[..., 1:2] * o_s + gc[..., 2:3] * o_w

    out = lax.map(chunk, jnp.arange(S // Q_CHUNK))
    out = out.transpose(1, 0, 4, 2, 3, 5).reshape(B, S, HQ)
    return out @ w_o


def fox_mixer(x, w_in, b_f, w_o):
    B, S, _ = x.shape
    proj = x @ w_in
    q, k, v, fl = jnp.split(proj, [HQ, 2 * HQ, 3 * HQ], axis=-1)

    def heads(t):
        return t.reshape(B, S, N_HEADS, HEAD_DIM).transpose(0, 2, 1, 3)

    q, k, v = heads(q), heads(k), heads(v)
    log_f = jax.nn.log_sigmoid((fl + b_f).astype(jnp.float32)).transpose(0, 2, 1)
    cum = jnp.cumsum(log_f, axis=-1)
    scale = HEAD_DIM ** -0.5
    outs = []
    for blk in range(S // FOX_BLOCK):
        s0, s1 = blk * FOX_BLOCK, (blk + 1) * FOX_BLOCK
        logits = jnp.einsum('bhqd,bhkd->bhqk', q[:, :, s0:s1], k[:, :, :s1]).astype(jnp.float32) * scale
        logits = logits + cum[:, :, s0:s1, None] - cum[:, :, None, :s1]
        causal = jnp.arange(s0, s1)[:, None] >= jnp.arange(s1)[None, :]
        p = masked_softmax(logits, causal)
        outs.append(jnp.einsum('bhqk,bhkd->bhqd', p.astype(v.dtype), v[:, :, :s1]))
    o = jnp.concatenate(outs, axis=2)
    return o.transpose(0, 2, 1, 3).reshape(B, S, HQ) @ w_o


def conv_ffn(x, w_up, conv_w, conv_b, w_down):
    h = x @ w_up
    C = h.shape[-1]
    h = lax.conv_general_dilated(h, conv_w[:, None, :], window_strides=(1,),
                                 padding=[(CONV_WIDTH - 1, 0)],
                                 dimension_numbers=('NWC', 'WIO', 'NWC'),
                                 feature_group_count=C) + conv_b
    a, b = jnp.split(h, 2, axis=-1)
    return (jax.nn.gelu(a) * b) @ w_down


def setup_inputs(seed: int = 0) -> dict:
    key = jax.random.key(seed)
    ks = jax.random.split(key, 24)
    nrm = jax.random.normal
    f32 = jnp.float32
    NA, NF = N_NSA_LAYERS, N_FOX_LAYERS
    return {
        "x": nrm(ks[0], (BATCH, SEQ, D_MODEL), f32),
        "nsa_w_in": nrm(ks[1], (NA, D_MODEL, NSA_PROJ), f32) * D_MODEL ** -0.5,
        "nsa_b_gate": nrm(ks[2], (NA, N_BRANCH * N_HEADS), f32) * 0.1,
        "nsa_pe": nrm(ks[3], (NA, CMP_LEN, HEAD_DIM), f32) * 0.1,
        "nsa_phik_w1": nrm(ks[4], (NA, CMP_LEN * HEAD_DIM, PHI_HIDDEN), f32) * (CMP_LEN * HEAD_DIM) ** -0.5,
        "nsa_phik_b1": nrm(ks[5], (NA, PHI_HIDDEN), f32) * 0.02,
        "nsa_phik_w2": nrm(ks[6], (NA, PHI_HIDDEN, HEAD_DIM), f32) * PHI_HIDDEN ** -0.5,
        "nsa_phiv_w1": nrm(ks[7], (NA, CMP_LEN * HEAD_DIM, PHI_HIDDEN), f32) * (CMP_LEN * HEAD_DIM) ** -0.5,
        "nsa_phiv_b1": nrm(ks[8], (NA, PHI_HIDDEN), f32) * 0.02,
        "nsa_phiv_w2": nrm(ks[9], (NA, PHI_HIDDEN, HEAD_DIM), f32) * PHI_HIDDEN ** -0.5,
        "nsa_w_o": nrm(ks[10], (NA, HQ, D_MODEL), f32) * HQ ** -0.5 * BETA,
        "fox_w_in": nrm(ks[11], (NF, D_MODEL, FOX_PROJ), f32) * D_MODEL ** -0.5,
        "fox_b_f": jax.random.uniform(ks[12], (NF, N_HEADS), f32, 1.0, 5.0),
        "fox_w_o": nrm(ks[13], (NF, HQ, D_MODEL), f32) * HQ ** -0.5 * BETA,
        "ffn_w_up": nrm(ks[14], (DEPTH, D_MODEL, 2 * D_FF), f32) * D_MODEL ** -0.5,
        "ffn_conv_w": nrm(ks[15], (DEPTH, CONV_WIDTH, 2 * D_FF), f32) * CONV_WIDTH ** -0.5,
        "ffn_conv_b": nrm(ks[16], (DEPTH, 2 * D_FF), f32) * 0.02,
        "ffn_w_down": nrm(ks[17], (DEPTH, D_FF, D_MODEL), f32) * D_FF ** -0.5 * BETA,
        "ln1_g": 1.0 + nrm(ks[18], (DEPTH, D_MODEL), f32) * 0.02,
        "ln1_b": nrm(ks[19], (DEPTH, D_MODEL), f32) * 0.02,
        "ln2_g": 1.0 + nrm(ks[20], (DEPTH, D_MODEL), f32) * 0.02,
        "ln2_b": nrm(ks[21], (DEPTH, D_MODEL), f32) * 0.02,
    }


def reference(x, nsa_w_in, nsa_b_gate, nsa_pe, nsa_phik_w1, nsa_phik_b1, nsa_phik_w2,
              nsa_phiv_w1, nsa_phiv_b1, nsa_phiv_w2, nsa_w_o, fox_w_in, fox_b_f, fox_w_o,
              ffn_w_up, ffn_conv_w, ffn_conv_b, ffn_w_down, ln1_g, ln1_b, ln2_g, ln2_b):
    for i in range(DEPTH):
        j = i // N_MIXERS
        if i % N_MIXERS == 0:
            y = nsa_mixer(x, nsa_w_in[j], nsa_b_gate[j], nsa_pe[j], nsa_phik_w1[j], nsa_phik_b1[j],
                          nsa_phik_w2[j], nsa_phiv_w1[j], nsa_phiv_b1[j], nsa_phiv_w2[j], nsa_w_o[j])
        else:
            y = fox_mixer(x, fox_w_in[j], fox_b_f[j], fox_w_o[j])
        x = layer_norm(ALPHA * x + y, ln1_g[i], ln1_b[i])
        f = conv_ffn(x, ffn_w_up[i], ffn_conv_w[i], ffn_conv_b[i], ffn_w_down[i])
        x = layer_norm(ALPHA * x + f, ln2_g[i], ln2_b[i])
    return x
```

```python
import functools
import math

import jax
import jax.numpy as jnp
from jax import lax
from jax.experimental import pallas as pl
from jax.experimental.pallas import tpu as pltpu

F32 = jnp.float32
BF16 = jnp.bfloat16

D_MODEL = 1024
DEPTH = 4
HEAD_DIM = 64
N_HEADS = 16
HQ = N_HEADS * HEAD_DIM
NSA_GROUPS = 4
NSA_REP = 4
NSA_KV = NSA_GROUPS * HEAD_DIM
CMP_LEN = 32
CMP_STRIDE = 16
SEL_LEN = 64
N_SEL = 16
WINDOW = 512
PHI_HIDDEN = 256
N_BRANCH = 3
FORCE_SCORE = 1e4
D_FF = 2816
ROPE_THETA = 10000.0
ALPHA = (2 * DEPTH) ** 0.25
LN_EPS = 1e-5
SCALE = HEAD_DIM ** -0.5

TT = 256
AUG = 128
MASK_BIAS = -1e30
FF_CHUNK = 256
FFN_TM = 512
HALO = 16
VMEM_LIMIT = 56 * 1024 * 1024


def _cparams(sem):
    return pltpu.CompilerParams(dimension_semantics=sem, vmem_limit_bytes=VMEM_LIMIT)


def _layer_norm(z, g, b):
    mu = jnp.mean(z, axis=-1, keepdims=True)
    zc = z - mu
    var = jnp.mean(zc * zc, axis=-1, keepdims=True)
    return zc * lax.rsqrt(var + LN_EPS) * g + b


def _gelu_tanh(x):
    c = math.sqrt(2.0 / math.pi)
    return x * (0.5 * (1.0 + jnp.tanh(c * (x + 0.044715 * (x * x * x)))))


def _log_sigmoid(z):
    return -(jnp.maximum(-z, 0.0) + jnp.log1p(jnp.exp(-jnp.abs(z))))


def _dot(a, b):
    return jnp.dot(a, b, preferred_element_type=F32)


def _dot_nt(a, b):
    return lax.dot_general(a, b, (((1,), (1,)), ((), ())), preferred_element_type=F32)


def _split3(x):
    p1 = x.astype(BF16)
    r1 = x - p1.astype(F32)
    p2 = r1.astype(BF16)
    p3 = (r1 - p2.astype(F32)).astype(BF16)
    return p1, p2, p3


def _proj_ln_body(a_ref, w_ref, x_ref, g_ref, b_ref, o_ref):
    y = _dot(a_ref[...], w_ref[...])
    o_ref[...] = _layer_norm(ALPHA * x_ref[...] + y, g_ref[...], b_ref[...])


def _proj_ln(a, w, x, g, b):
    n, k = a.shape
    d = w.shape[1]
    tm = FFN_TM
    return pl.pallas_call(
        _proj_ln_body,
        grid=(n // tm,),
        in_specs=[
            pl.BlockSpec((tm, k), lambda i: (i, 0)),
            pl.BlockSpec((k, d), lambda i: (0, 0)),
            pl.BlockSpec((tm, d), lambda i: (i, 0)),
            pl.BlockSpec((1, d), lambda i: (0, 0)),
            pl.BlockSpec((1, d), lambda i: (0, 0)),
        ],
        out_specs=pl.BlockSpec((tm, d), lambda i: (i, 0)),
        out_shape=jax.ShapeDtypeStruct((n, d), F32),
        compiler_params=_cparams(("parallel",)),
        name="proj_ln",
    )(a, w, x, g, b)


def _ffn_body(seq_tiles, x_ref, xp_ref, wa_ref, wb_ref, cwa_ref, cwb_ref, wd_ref, g_ref, b_ref,
              o_ref, acc_ref):
    i = pl.program_id(0)
    tm = x_ref.shape[0]
    x = x_ref[...]
    starts_seq = (i % seq_tiles) == 0
    halo = jnp.where(starts_seq, 0.0, xp_ref[...]).astype(BF16)
    xcat = jnp.concatenate([halo, x.astype(BF16)], axis=0)
    acc_ref[...] = jnp.zeros_like(acc_ref)
    n_chunks = wa_ref.shape[0]

    def conv(h, cw):
        return (cw[0:1] * h[HALO - 2:HALO - 2 + tm] + cw[1:2] * h[HALO - 1:HALO - 1 + tm]
                + cw[2:3] * h[HALO:HALO + tm] + cw[3:4])

    def chunk(c, carry):
        ha = conv(_dot(xcat, wa_ref[c]), cwa_ref[c])
        hb = conv(_dot(xcat, wb_ref[c]), cwb_ref[c])
        gated = (_gelu_tanh(ha) * hb).astype(BF16)
        acc_ref[...] += _dot(gated, wd_ref[c])
        return carry

    lax.fori_loop(0, n_chunks, chunk, 0)
    o_ref[...] = _layer_norm(ALPHA * x + acc_ref[...], g_ref[...], b_ref[...])


def _ffn(x, seq_len, wa, wb, cwa, cwb, wd, g, b):
    n, d = x.shape
    tm = FFN_TM
    nc = wa.shape[0]
    cf = wa.shape[2]
    hblk = tm // HALO
    return pl.pallas_call(
        functools.partial(_ffn_body, seq_len // tm),
        grid=(n // tm,),
        in_specs=[
            pl.BlockSpec((tm, d), lambda i: (i, 0)),
            pl.BlockSpec((HALO, d), lambda i: (jnp.maximum(i * hblk - 1, 0), 0)),
            pl.BlockSpec((nc, d, cf), lambda i: (0, 0, 0)),
            pl.BlockSpec((nc, d, cf), lambda i: (0, 0, 0)),
            pl.BlockSpec((nc, 4, cf), lambda i: (0, 0, 0)),
            pl.BlockSpec((nc, 4, cf), lambda i: (0, 0, 0)),
            pl.BlockSpec((nc, cf, d), lambda i: (0, 0, 0)),
            pl.BlockSpec((1, d), lambda i: (0, 0)),
            pl.BlockSpec((1, d), lambda i: (0, 0)),
        ],
        out_specs=pl.BlockSpec((tm, d), lambda i: (i, 0)),
        out_shape=jax.ShapeDtypeStruct((n, d), F32),
        scratch_shapes=[pltpu.VMEM((tm, d), F32)],
        compiler_params=_cparams(("parallel",)),
        name="conv_ffn",
    )(x, x, wa, wb, cwa, cwb, wd, g, b)


def _ffn_weights(w_up, conv_w, conv_b, w_down):
    nc = D_FF // FF_CHUNK

    def up(w):
        return w.reshape(D_MODEL, nc, FF_CHUNK).transpose(1, 0, 2).astype(BF16)

    def taps(cw, cb):
        t = jnp.concatenate([cw, cb[None]], axis=0)
        return t.reshape(4, nc, FF_CHUNK).transpose(1, 0, 2)

    wa, wb = up(w_up[:, :D_FF]), up(w_up[:, D_FF:])
    cwa, cwb = taps(conv_w[:, :D_FF], conv_b[:D_FF]), taps(conv_w[:, D_FF:], conv_b[D_FF:])
    wd = w_down.reshape(nc, FF_CHUNK, D_MODEL).astype(BF16)
    return wa, wb, cwa, cwb, wd


_NSA_T_ROWS = HQ + 2 * NSA_KV + N_BRANCH * N_HEADS
_NSA_N_COLS = 2 * NSA_KV + 2 * NSA_GROUPS * AUG


def _nsa_inproj_body(seq_tiles, x_ref, wt_ref, wn_ref, bg_ref, cost_ref, sint_ref, cosn_ref, sinn_ref,
                     qn_ref, qr_ref, vs_ref, vw_ref, gt_ref, ksa_ref, kwa_ref, kc_ref, vc_ref):
    i = pl.program_id(0)
    xb = x_ref[...].astype(BF16)
    t = _dot_nt(wt_ref[...], xb)
    q = t[0:HQ] * SCALE
    qn_ref[0] = q.astype(BF16)
    q3 = q.reshape(N_HEADS, HEAD_DIM, TT)
    half = HEAD_DIM // 2
    rot = jnp.concatenate([-q3[:, half:], q3[:, :half]], axis=1)
    qr = q3 * cost_ref[0][None] + rot * sint_ref[0][None]
    qr_ref[0] = qr.reshape(HQ, TT).astype(BF16)
    vs_ref[0] = t[HQ:HQ + NSA_KV].astype(BF16)
    vw_ref[0] = t[HQ + NSA_KV:HQ + 2 * NSA_KV].astype(BF16)
    gt_ref[0] = jax.nn.sigmoid(t[HQ + 2 * NSA_KV:] + bg_ref[...])

    n = _dot(xb, wn_ref[...])
    kc_ref[...] = n[:, 0:NSA_KV]
    vc_ref[...] = n[:, NSA_KV:2 * NSA_KV]
    width = NSA_GROUPS * AUG
    lane = lax.broadcasted_iota(jnp.int32, (TT, width), 1) % AUG
    cosn, sinn = cosn_ref[...], sinn_ref[...]

    def rope_nat(k):
        rh = jnp.where(lane < half, -pltpu.roll(k, width - half, axis=1), pltpu.roll(k, half, axis=1))
        return k * cosn + rh * sinn

    ks = rope_nat(n[:, 2 * NSA_KV:2 * NSA_KV + width])
    kw = rope_nat(n[:, 2 * NSA_KV + width:])
    row = lax.broadcasted_iota(jnp.int32, (TT, width), 0)
    blk = ((i % seq_tiles) * TT + row) // SEL_LEN
    onehot = (lane - HEAD_DIM == blk).astype(F32)
    ksa_ref[...] = (ks + onehot).astype(BF16)
    kwa_ref[...] = kw.astype(BF16)


def _nsa_inproj(x, seq_len, wt, wn, bg, cost, sint, cosn, sinn):
    n = x.shape[0]
    nt = n // TT
    seq_tiles = seq_len // TT
    width = NSA_GROUPS * AUG
    full = lambda r, c: pl.BlockSpec((r, c), lambda i: (0, 0))
    tile3 = lambda r: pl.BlockSpec((1, r, TT), lambda i: (i, 0, 0))
    nat = lambda c: pl.BlockSpec((TT, c), lambda i: (i, 0))
    return pl.pallas_call(
        functools.partial(_nsa_inproj_body, seq_tiles),
        grid=(nt,),
        in_specs=[
            nat(D_MODEL),
            full(_NSA_T_ROWS, D_MODEL),
            full(D_MODEL, _NSA_N_COLS),
            full(N_BRANCH * N_HEADS, 1),
            pl.BlockSpec((1, HEAD_DIM, TT), lambda i: (i % seq_tiles, 0, 0)),
            pl.BlockSpec((1, HEAD_DIM, TT), lambda i: (i % seq_tiles, 0, 0)),
            pl.BlockSpec((TT, width), lambda i: (i % seq_tiles, 0)),
            pl.BlockSpec((TT, width), lambda i: (i % seq_tiles, 0)),
        ],
        out_specs=[tile3(HQ), tile3(HQ), tile3(NSA_KV), tile3(NSA_KV), tile3(N_BRANCH * N_HEADS),
                   nat(width), nat(width), nat(NSA_KV), nat(NSA_KV)],
        out_shape=[
            jax.ShapeDtypeStruct((nt, HQ, TT), BF16),
            jax.ShapeDtypeStruct((nt, HQ, TT), BF16),
            jax.ShapeDtypeStruct((nt, NSA_KV, TT), BF16),
            jax.ShapeDtypeStruct((nt, NSA_KV, TT), BF16),
            jax.ShapeDtypeStruct((nt, N_BRANCH * N_HEADS, TT), F32),
            jax.ShapeDtypeStruct((n, width), BF16),
            jax.ShapeDtypeStruct((n, width), BF16),
            jax.ShapeDtypeStruct((n, NSA_KV), F32),
            jax.ShapeDtypeStruct((n, NSA_KV), F32),
        ],
        compiler_params=_cparams(("parallel",)),
        name="nsa_inproj",
    )(x, wt, wn, bg, cost, sint, cosn, sinn)


def _pad_heads(w, n_heads):
    w3 = w.reshape(w.shape[0], n_heads, HEAD_DIM)
    return jnp.concatenate([w3, jnp.zeros_like(w3)], axis=-1).reshape(w.shape[0], n_heads * AUG)


def _nsa_inproj_weights(w_in, b_gate):
    cuts = [HQ + i * NSA_KV for i in range(7)]
    wq, wkc, wvc, wks, wvs, wkw, wvw, wg = jnp.split(w_in, cuts, axis=1)
    wt = jnp.concatenate([wq, wvs, wvw, wg], axis=1).T.astype(BF16)
    wn = jnp.concatenate([wkc, wvc, _pad_heads(wks, NSA_GROUPS), _pad_heads(wkw, NSA_GROUPS)],
                         axis=1).astype(BF16)
    return wt, wn, b_gate.reshape(-1, 1)


def _rope_tables(seq_len):
    inv = ROPE_THETA ** (-jnp.arange(0, HEAD_DIM, 2, dtype=F32) / HEAD_DIM)
    ang = jnp.arange(seq_len, dtype=F32)[:, None] * inv[None, :]
    ang = jnp.concatenate([ang, ang], axis=-1)
    cos, sin = jnp.cos(ang), jnp.sin(ang)
    seq_tiles = seq_len // TT

    def transposed(t):
        return t.reshape(seq_tiles, TT, HEAD_DIM).transpose(0, 2, 1)

    def natural(t):
        return jnp.tile(jnp.concatenate([t, jnp.zeros_like(t)], axis=1), (1, NSA_GROUPS))

    return transposed(cos), transposed(sin), natural(cos), natural(sin)


def _compress_body(hk_ref, hv_ref, pe_ref, w1k_ref, b1k_ref, w2k_ref, w1vt_ref, b1v_ref, w2vt_ref,
                   kc_ref, vct_ref):
    nc = hk_ref.shape[1]
    pe_top, pe_bot = pe_ref[0:1], pe_ref[1:2]
    hk = hk_ref[0]
    top = _dot((hk + pe_top).astype(BF16), w1k_ref[0])
    bot = _dot((hk + pe_bot).astype(BF16), w1k_ref[1])
    hid = _gelu_tanh(top + pltpu.roll(bot, nc - 1, axis=0) + b1k_ref[...])
    kc_ref[0] = _dot(hid.astype(BF16), w2k_ref[...]).astype(BF16)

    hv = hv_ref[0]
    top_t = _dot_nt(w1vt_ref[0], (hv + pe_top).astype(BF16))
    bot_t = _dot_nt(w1vt_ref[1], (hv + pe_bot).astype(BF16))
    hid_t = _gelu_tanh(top_t + pltpu.roll(bot_t, nc - 1, axis=1) + b1v_ref[...])
    vct_ref[0] = _dot(w2vt_ref[...], hid_t.astype(BF16)).astype(BF16)


def _compress(hk, hv, pe2, w1k, b1k, w2k, w1vt, b1v, w2vt):
    bg, nc, hw = hk.shape
    cst = lambda shape: pl.BlockSpec(shape, lambda i: (0,) * len(shape))
    return pl.pallas_call(
        _compress_body,
        grid=(bg,),
        in_specs=[
            pl.BlockSpec((1, nc, hw), lambda i: (i, 0, 0)),
            pl.BlockSpec((1, nc, hw), lambda i: (i, 0, 0)),
            cst((2, hw)),
            cst((2, hw, PHI_HIDDEN)), cst((1, PHI_HIDDEN)), cst((PHI_HIDDEN, HEAD_DIM)),
            cst((2, PHI_HIDDEN, hw)), cst((PHI_HIDDEN, 1)), cst((HEAD_DIM, PHI_HIDDEN)),
        ],
        out_specs=[pl.BlockSpec((1, nc, HEAD_DIM), lambda i: (i, 0, 0)),
                   pl.BlockSpec((1, HEAD_DIM, nc), lambda i: (i, 0, 0))],
        out_shape=[jax.ShapeDtypeStruct((bg, nc, HEAD_DIM), BF16),
                   jax.ShapeDtypeStruct((bg, HEAD_DIM, nc), BF16)],
        compiler_params=_cparams(("parallel",)),
        name="nsa_compress",
    )(hk, hv, pe2, w1k, b1k, w2k, w1vt, b1v, w2vt)


def _online_update(state, s, v_t):
    m, l, acc = state
    m_new = jnp.maximum(m, jnp.max(s, axis=0, keepdims=True))
    alpha = jnp.exp(m - m_new)
    p = jnp.exp(s - m_new)
    l = alpha * l + jnp.sum(p, axis=0, keepdims=True)
    acc = alpha * acc + _dot(v_t, p.astype(BF16))
    return m_new, l, acc


def _init_state():
    return (jnp.full((1, TT), MASK_BIAS, F32), jnp.zeros((1, TT), F32), jnp.zeros((HEAD_DIM, TT), F32))


def _nsa_attn_body(n_sb, qn_ref, qr_ref, kc_ref, vct_ref, ovt_ref, ks_ref, vs_ref, kw_ref, vw_ref,
                   g_ref, o_ref, qa_ref):
    qi = pl.program_id(2)
    t0 = qi * TT
    tpos = t0 + lax.broadcasted_iota(jnp.int32, (1, TT), 1)
    nc = kc_ref.shape[1]

    kc = kc_ref[0]
    cmp_end = lax.broadcasted_iota(jnp.int32, (nc, 1), 0) * CMP_STRIDE + (CMP_LEN - 1)
    cmask = cmp_end <= tpos
    psum = jnp.zeros((nc, TT), F32)
    o_cmp = []
    for r in range(NSA_REP):
        s = _dot(kc, qn_ref[0, r * HEAD_DIM:(r + 1) * HEAD_DIM, :])
        s = jnp.where(cmask, s, -jnp.inf)
        m = jnp.max(s, axis=0, keepdims=True)
        m = jnp.where(m == -jnp.inf, 0.0, m)
        e = jnp.exp(s - m)
        d = jnp.sum(e, axis=0, keepdims=True)
        p = e * (1.0 / jnp.where(d > 0, d, 1.0))
        psum = psum + p
        o_cmp.append(_dot(vct_ref[0], p.astype(BF16)))
    p_hi = psum.astype(BF16)
    p_lo = (psum - p_hi.astype(F32)).astype(BF16)
    imp = _dot(ovt_ref[...], p_hi) + _dot(ovt_ref[...], p_lo)

    nrow = ovt_ref.shape[0]
    blk = lax.broadcasted_iota(jnp.int32, (nrow, 1), 0)
    cur = tpos // SEL_LEN
    forced = (blk == 0) | (blk == cur) | (blk == cur - 1)
    valid = blk * SEL_LEN <= tpos
    score = jnp.where(forced, FORCE_SCORE, jnp.where(valid, imp, -1.0))
    score = jnp.where(blk < n_sb, score, -2.0)
    rank = jnp.zeros((nrow, TT), F32)
    for k in range(n_sb):
        sk = score[k:k + 1, :]
        beats = (sk > score) | ((sk == score) & (blk > k))
        rank = rank + jnp.where(beats, 1.0, 0.0)
    sel_bias = jnp.where(rank < min(N_SEL, n_sb), 0.0, MASK_BIAS).astype(BF16)
    for r in range(NSA_REP):
        qa_ref[r, 0:HEAD_DIM, :] = qr_ref[0, r * HEAD_DIM:(r + 1) * HEAD_DIM, :]
        qa_ref[r, HEAD_DIM:AUG, :] = sel_bias

    kpos0 = lax.broadcasted_iota(jnp.int32, (TT, 1), 0)

    def sel_tile(j, states, causal):
        k = ks_ref[0, pl.ds(pl.multiple_of(j * TT, TT), TT), :]
        v_t = vs_ref[0, j]
        out = []
        for r in range(NSA_REP):
            s = _dot(k, qa_ref[r])
            if causal:
                s = jnp.where(j * TT + kpos0 <= tpos, s, MASK_BIAS)
            out.append(_online_update(states[r], s, v_t))
        return tuple(out)

    st = tuple(_init_state() for _ in range(NSA_REP))
    st = lax.fori_loop(0, qi, lambda j, c: sel_tile(j, c, False), st)
    st = sel_tile(qi, st, True)
    o_sel = [acc * (1.0 / l) for (_, l, acc) in st]

    def win_tile(j, states, mask):
        k = kw_ref[0, pl.ds(pl.multiple_of(j * TT, TT), TT), :]
        v_t = vw_ref[0, j]
        out = []
        for r in range(NSA_REP):
            s = _dot(k[:, 0:HEAD_DIM], qr_ref[0, r * HEAD_DIM:(r + 1) * HEAD_DIM, :])
            if mask is not None:
                s = jnp.where(mask, s, MASK_BIAS)
            out.append(_online_update(states[r], s, v_t))
        return tuple(out)

    sw = tuple(_init_state() for _ in range(NSA_REP))
    sw = win_tile(qi, sw, t0 + kpos0 <= tpos)
    j1 = jnp.maximum(qi - 1, 0)
    sw = win_tile(j1, sw, jnp.broadcast_to(qi >= 1, (TT, TT)))
    j2 = jnp.maximum(qi - 2, 0)
    sw = win_tile(j2, sw, (qi >= 2) & (tpos - (j2 * TT + kpos0) < WINDOW))
    o_win = [acc * (1.0 / l) for (_, l, acc) in sw]

    rows = []
    for r in range(NSA_REP):
        g = g_ref[0, 0]
        gc = g[r * N_BRANCH + 0:r * N_BRANCH + 1]
        gs = g[r * N_BRANCH + 1:r * N_BRANCH + 2]
        gw = g[r * N_BRANCH + 2:r * N_BRANCH + 3]
        rows.append(gc * o_cmp[r] + gs * o_sel[r] + gw * o_win[r])
    o_ref[0] = jnp.concatenate(rows, axis=0).T.astype(BF16)


def _nsa_attn(batch, seq_len, qn, qr, kc, vct, ovt, ksa, vs, kwa, vw, gt):
    nq = seq_len // TT
    nc = kc.shape[1]
    n_sb = seq_len // SEL_LEN
    g_rows = NSA_REP * N_BRANCH
    ksa3 = ksa.reshape(batch, seq_len, NSA_GROUPS * AUG)
    kwa3 = kwa.reshape(batch, seq_len, NSA_GROUPS * AUG)
    vs4 = vs.reshape(batch, nq, NSA_KV, TT)
    vw4 = vw.reshape(batch, nq, NSA_KV, TT)
    gt4 = gt.reshape(batch * nq, NSA_GROUPS, g_rows, TT)
    qspec = pl.BlockSpec((1, NSA_REP * HEAD_DIM, TT), lambda b, g, q: (b * nq + q, g, 0))
    kspec = pl.BlockSpec((1, seq_len, AUG), lambda b, g, q: (b, 0, g))
    vspec = pl.BlockSpec((1, nq, HEAD_DIM, TT), lambda b, g, q: (b, 0, g, 0))
    return pl.pallas_call(
        functools.partial(_nsa_attn_body, n_sb),
        grid=(batch, NSA_GROUPS, nq),
        in_specs=[
            qspec, qspec,
            pl.BlockSpec((1, nc, HEAD_DIM), lambda b, g, q: (b * NSA_GROUPS + g, 0, 0)),
            pl.BlockSpec((1, HEAD_DIM, nc), lambda b, g, q: (b * NSA_GROUPS + g, 0, 0)),
            pl.BlockSpec(ovt.shape, lambda b, g, q: (0, 0)),
            kspec, vspec, kspec, vspec,
            pl.BlockSpec((1, 1, g_rows, TT), lambda b, g, q: (b * nq + q, g, 0, 0)),
        ],
        out_specs=pl.BlockSpec((1, TT, NSA_REP * HEAD_DIM), lambda b, g, q: (b, q, g)),
        out_shape=jax.ShapeDtypeStruct((batch, seq_len, HQ), BF16),
        scratch_shapes=[pltpu.VMEM((NSA_REP, AUG, TT), BF16)],
        compiler_params=_cparams(("parallel", "parallel", "arbitrary")),
        name="nsa_attn",
    )(qn, qr, kc, vct, ovt, ksa3, vs4, kwa3, vw4, gt4)


def _overlap_t(seq_len, nc):
    n_cmp = (seq_len - CMP_LEN) // CMP_STRIDE + 1
    n_sb = seq_len // SEL_LEN
    cmp_start = jnp.arange(n_cmp) * CMP_STRIDE
    sel_start = jnp.arange(n_sb) * SEL_LEN
    ov = jnp.clip(jnp.minimum(cmp_start[:, None] + CMP_LEN, sel_start[None, :] + SEL_LEN)
                  - jnp.maximum(cmp_start[:, None], sel_start[None, :]), 0, None).astype(F32) / CMP_LEN
    out = jnp.zeros((SEL_LEN, nc), F32).at[:n_sb, :n_cmp].set(ov.T)
    return out.astype(BF16)


def _nsa_attention(x, batch, seq_len, w_in, b_gate, pe, k_w1, k_b1, k_w2, v_w1, v_b1, v_w2, tables):
    wt, wn, bg = _nsa_inproj_weights(w_in, b_gate)
    qn, qr, vs, vw, gt, ksa, kwa, kc, vc = _nsa_inproj(x, seq_len, wt, wn, bg, *tables)

    nc = seq_len // CMP_STRIDE
    hw = CMP_STRIDE * HEAD_DIM

    def half_blocks(t):
        t = t.reshape(batch, seq_len, NSA_GROUPS, HEAD_DIM).transpose(0, 2, 1, 3)
        return t.reshape(batch * NSA_GROUPS, nc, hw)

    kcmp, vcmp_t = _compress(
        half_blocks(kc), half_blocks(vc), pe.reshape(2, hw),
        k_w1.reshape(2, hw, PHI_HIDDEN).astype(BF16), k_b1.reshape(1, -1), k_w2.astype(BF16),
        v_w1.reshape(2, hw, PHI_HIDDEN).transpose(0, 2, 1).astype(BF16), v_b1.reshape(-1, 1),
        v_w2.T.astype(BF16))
    attn = _nsa_attn(batch, seq_len, qn, qr, kcmp, vcmp_t, _overlap_t(seq_len, nc), ksa, vs, kwa, vw, gt)
    return attn.reshape(batch * seq_len, HQ)


def _nsa_layer(x, batch, seq_len, w_in, b_gate, pe, k_w1, k_b1, k_w2, v_w1, v_b1, v_w2, w_o,
               ln_g, ln_b, tables):
    attn = _nsa_attention(x, batch, seq_len, w_in, b_gate, pe, k_w1, k_b1, k_w2, v_w1, v_b1, v_w2, tables)
    return _proj_ln(attn, w_o.astype(BF16), x, ln_g.reshape(1, -1), ln_b.reshape(1, -1))


_FOX_T_ROWS = 2 * HQ + N_HEADS
_FOX_N_COLS = N_HEADS * AUG + 128
_N_PIECE = 3


def _fox_inproj_body(seq_tiles, x_ref, wt_ref, wn_ref, bft_ref, bfn_ref, place_ref, ones_ref, route_ref,
                     qa_ref, ka_ref, vt_ref, off_ref, run_ref):
    i = pl.program_id(0)
    xb = x_ref[...].astype(BF16)
    t = _dot_nt(wt_ref[...], xb)
    vt_ref[0] = t[HQ:2 * HQ].astype(BF16)
    lf_t = _log_sigmoid(t[2 * HQ:] + bft_ref[...])
    n = _dot(xb, wn_ref[...])
    kw = N_HEADS * AUG
    lf_n = _log_sigmoid(n[:, kw:kw + _N_PIECE * N_HEADS] + bfn_ref[...])

    r_i = lax.broadcasted_iota(jnp.int32, (TT, TT), 0)
    c_i = lax.broadcasted_iota(jnp.int32, (TT, TT), 1)
    upper = ((r_i > 0) & (r_i <= c_i)).astype(BF16)
    lower = ((c_i > 0) & (c_i <= r_i)).astype(BF16)
    a_t = sum(_dot(p, upper) for p in _split3(lf_t))
    a_n = sum(_dot(lower, p) for p in _split3(lf_n))

    q3 = (t[0:HQ] * SCALE).reshape(N_HEADS, HEAD_DIM, TT)
    qa_ref[0, :, 0:HEAD_DIM, :] = q3.astype(BF16)
    stacked = jnp.concatenate(list(_split3(a_t)) + [jnp.ones((N_HEADS, TT), BF16)], axis=0)
    for h in range(N_HEADS):
        qa_ref[0, h, HEAD_DIM:AUG, :] = _dot(route_ref[h], stacked).astype(BF16)

    b1, b2, b3 = _split3(-a_n)
    grp = lax.broadcasted_iota(jnp.int32, (TT, _N_PIECE * N_HEADS), 1) // N_HEADS
    bsel = jnp.where(grp == 0, b1, jnp.where(grp == 1, b2, b3))
    ka_ref[...] = (n[:, 0:kw] + _dot(bsel, place_ref[...]) + ones_ref[...]).astype(BF16)

    @pl.when(i % seq_tiles == 0)
    def _():
        run_ref[...] = jnp.zeros_like(run_ref)

    first = lf_t[:, 0:1]
    off_ref[0] = jnp.broadcast_to(run_ref[:, 0:1] + first, (N_HEADS, TT))
    run_ref[...] = run_ref[...] + (a_t[:, TT - 1:TT] + first)


def _fox_inproj(x, seq_len, wt, wn, bft, bfn, place, ones, route):
    n = x.shape[0]
    nt = n // TT
    kw = N_HEADS * AUG
    full = lambda r, c: pl.BlockSpec((r, c), lambda i: (0, 0))
    return pl.pallas_call(
        functools.partial(_fox_inproj_body, seq_len // TT),
        grid=(nt,),
        in_specs=[
            pl.BlockSpec((TT, D_MODEL), lambda i: (i, 0)),
            full(_FOX_T_ROWS, D_MODEL), full(D_MODEL, _FOX_N_COLS),
            full(N_HEADS, 1), full(1, _N_PIECE * N_HEADS),
            full(_N_PIECE * N_HEADS, kw), full(1, kw),
            pl.BlockSpec((N_HEADS, AUG - HEAD_DIM, AUG - HEAD_DIM), lambda i: (0, 0, 0)),
        ],
        out_specs=[
            pl.BlockSpec((1, N_HEADS, AUG, TT), lambda i: (i, 0, 0, 0)),
            pl.BlockSpec((TT, kw), lambda i: (i, 0)),
            pl.BlockSpec((1, HQ, TT), lambda i: (i, 0, 0)),
            pl.BlockSpec((1, N_HEADS, TT), lambda i: (i, 0, 0)),
        ],
        out_shape=[
            jax.ShapeDtypeStruct((nt, N_HEADS, AUG, TT), BF16),
            jax.ShapeDtypeStruct((n, kw), BF16),
            jax.ShapeDtypeStruct((nt, HQ, TT), BF16),
            jax.ShapeDtypeStruct((nt, N_HEADS, TT), F32),
        ],
        scratch_shapes=[pltpu.VMEM((N_HEADS, 128), F32)],
        compiler_params=_cparams(("arbitrary",)),
        name="fox_inproj",
    )(x, wt, wn, bft, bfn, place, ones, route)


def _fox_inproj_weights(w_in, b_f):
    wq, wk, wv, wf = jnp.split(w_in, [HQ, 2 * HQ, 3 * HQ], axis=1)
    wt = jnp.concatenate([wq, wv, wf], axis=1).T.astype(BF16)
    pad = jnp.zeros((D_MODEL, _FOX_N_COLS - N_HEADS * AUG - _N_PIECE * N_HEADS), F32)
    wn = jnp.concatenate([_pad_heads(wk, N_HEADS)] + [wf] * _N_PIECE + [pad], axis=1).astype(BF16)
    bft = b_f.reshape(-1, 1)
    bfn = jnp.tile(b_f, _N_PIECE).reshape(1, -1)
    rows = jnp.arange(_N_PIECE * N_HEADS)
    cols = (rows % N_HEADS) * AUG + HEAD_DIM + _N_PIECE + rows // N_HEADS
    place = jnp.zeros((_N_PIECE * N_HEADS, N_HEADS * AUG), F32).at[rows, cols].set(1.0).astype(BF16)
    lane = jnp.arange(N_HEADS * AUG) % AUG
    ones = ((lane >= HEAD_DIM) & (lane < HEAD_DIM + _N_PIECE)).astype(F32).reshape(1, -1)
    hh = jnp.arange(N_HEADS)
    route = jnp.zeros((N_HEADS, AUG - HEAD_DIM, AUG - HEAD_DIM), F32)
    for k in range(_N_PIECE):
        route = route.at[hh, k, k * N_HEADS + hh].set(1.0)
        route = route.at[hh, _N_PIECE + k, _N_PIECE * N_HEADS].set(1.0)
    return wt, wn, bft, bfn, place, ones, route.astype(BF16)


def _fox_attn_body(qa_ref, ka_ref, vt_ref, off_ref, o_ref):
    h = pl.program_id(1)
    qi = pl.program_id(2)
    qa = qa_ref[0, 0]
    off_q = off_ref[0, pl.ds(qi, 1), pl.ds(h, 1), :].reshape(1, TT)
    tpos = lax.broadcasted_iota(jnp.int32, (1, TT), 1)
    kpos = lax.broadcasted_iota(jnp.int32, (TT, 1), 0)

    def tile(j, state, diagonal):
        k = ka_ref[0, pl.ds(pl.multiple_of(j * TT, TT), TT), :]
        s = _dot(k, qa)
        if diagonal:
            s = jnp.where(kpos <= tpos, s, MASK_BIAS)
        else:
            s = s + (off_q - off_ref[0, pl.ds(j, 1), pl.ds(h, 1), :].reshape(1, TT))
        return _online_update(state, s, vt_ref[0, j])

    st = lax.fori_loop(0, qi, lambda j, c: tile(j, c, False), _init_state())
    _, l, acc = tile(qi, st, True)
    o_ref[0] = (acc * (1.0 / l)).astype(BF16)


def _fox_attn(batch, seq_len, qa, ka, vt, off):
    nq = seq_len // TT
    ka3 = ka.reshape(batch, seq_len, N_HEADS * AUG)
    vt4 = vt.reshape(batch, nq, HQ, TT)
    off4 = off.reshape(batch, nq, N_HEADS, TT)
    return pl.pallas_call(
        _fox_attn_body,
        grid=(batch, N_HEADS, nq),
        in_specs=[
            pl.BlockSpec((1, 1, AUG, TT), lambda b, h, q: (b * nq + q, h, 0, 0)),
            pl.BlockSpec((1, seq_len, AUG), lambda b, h, q: (b, 0, h)),
            pl.BlockSpec((1, nq, HEAD_DIM, TT), lambda b, h, q: (b, 0, h, 0)),
            pl.BlockSpec((1, nq, N_HEADS, TT), lambda b, h, q: (b, 0, 0, 0)),
        ],
        out_specs=pl.BlockSpec((1, HEAD_DIM, TT), lambda b, h, q: (b * nq + q, h, 0)),
        out_shape=jax.ShapeDtypeStruct((batch * nq, HQ, TT), BF16),
        compiler_params=_cparams(("parallel", "parallel", "arbitrary")),
        name="fox_attn",
    )(qa, ka3, vt4, off4)


def _proj_ln_t_body(at_ref, w_ref, x_ref, g_ref, b_ref, o_ref):
    y = _dot(at_ref[0].T, w_ref[...])
    o_ref[...] = _layer_norm(ALPHA * x_ref[...] + y, g_ref[...], b_ref[...])


def _proj_ln_t(at, w, x, g, b):
    nt, k, _ = at.shape
    n, d = x.shape
    return pl.pallas_call(
        _proj_ln_t_body,
        grid=(nt,),
        in_specs=[
            pl.BlockSpec((1, k, TT), lambda i: (i, 0, 0)),
            pl.BlockSpec((k, d), lambda i: (0, 0)),
            pl.BlockSpec((TT, d), lambda i: (i, 0)),
            pl.BlockSpec((1, d), lambda i: (0, 0)),
            pl.BlockSpec((1, d), lambda i: (0, 0)),
        ],
        out_specs=pl.BlockSpec((TT, d), lambda i: (i, 0)),
        out_shape=jax.ShapeDtypeStruct((n, d), F32),
        compiler_params=_cparams(("parallel",)),
        name="proj_ln_t",
    )(at, w, x, g, b)


def _fox_layer(x, batch, seq_len, w_in, b_f, w_o, ln_g, ln_b):
    qa, ka, vt, off = _fox_inproj(x, seq_len, *_fox_inproj_weights(w_in, b_f))
    attn_t = _fox_attn(batch, seq_len, qa, ka, vt, off)
    return _proj_ln_t(attn_t, w_o.astype(BF16), x, ln_g.reshape(1, -1), ln_b.reshape(1, -1))


def kernel(x, nsa_w_in, nsa_b_gate, nsa_pe, nsa_phik_w1, nsa_phik_b1, nsa_phik_w2, nsa_phiv_w1,
           nsa_phiv_b1, nsa_phiv_w2, nsa_w_o, fox_w_in, fox_b_f, fox_w_o, ffn_w_up, ffn_conv_w,
           ffn_conv_b, ffn_w_down, ln1_g, ln1_b, ln2_g, ln2_b):
    batch, seq_len, d = x.shape
    assert d == D_MODEL and seq_len % FFN_TM == 0 and seq_len % TT == 0
    assert seq_len // SEL_LEN <= AUG - HEAD_DIM
    tables = _rope_tables(seq_len)
    h = x.reshape(batch * seq_len, d)
    for i in range(DEPTH):
        j = i // 2
        if i % 2 == 0:
            h = _nsa_layer(h, batch, seq_len, nsa_w_in[j], nsa_b_gate[j], nsa_pe[j], nsa_phik_w1[j],
                           nsa_phik_b1[j], nsa_phik_w2[j], nsa_phiv_w1[j], nsa_phiv_b1[j],
                           nsa_phiv_w2[j], nsa_w_o[j], ln1_g[i], ln1_b[i], tables)
        else:
            h = _fox_layer(h, batch, seq_len, fox_w_in[j], fox_b_f[j], fox_w_o[j], ln1_g[i], ln1_b[i])
        h = _ffn(h, seq_len, *_ffn_weights(ffn_w_up[i], ffn_conv_w[i], ffn_conv_b[i], ffn_w_down[i]),
                 ln2_g[i].reshape(1, -1), ln2_b[i].reshape(1, -1))
    return h.reshape(batch, seq_len, d)
```

```python
import functools
import math

import jax
import jax.numpy as jnp
from jax import lax
from jax.experimental import pallas as pl
from jax.experimental.pallas import tpu as pltpu

F32 = jnp.float32
BF16 = jnp.bfloat16

D_MODEL = 1024
DEPTH = 4
HEAD_DIM = 64
N_HEADS = 16
HQ = N_HEADS * HEAD_DIM
NSA_GROUPS = 4
NSA_REP = 4
NSA_KV = NSA_GROUPS * HEAD_DIM
CMP_LEN = 32
CMP_STRIDE = 16
SEL_LEN = 64
N_SEL = 16
WINDOW = 512
PHI_HIDDEN = 256
N_BRANCH = 3
FORCE_SCORE = 1e4
D_FF = 2816
ROPE_THETA = 10000.0
ALPHA = (2 * DEPTH) ** 0.25
LN_EPS = 1e-5
SCALE = HEAD_DIM ** -0.5
LOG2E = math.log2(math.e)
QSCALE = SCALE * LOG2E
VROWS = 80
OROWS = 72

TT = 256
AUG = 128
MASK_BIAS = -1e30
FF_CHUNK = 256
FFN_TM = 512
HEAD_BLOCK = 4
HALO = 16
VMEM_LIMIT = 56 * 1024 * 1024


def _cparams(sem):
    return pltpu.CompilerParams(dimension_semantics=sem, vmem_limit_bytes=VMEM_LIMIT)


def _layer_norm(z, g, b):
    mu = jnp.mean(z, axis=-1, keepdims=True)
    zc = z - mu
    var = jnp.mean(zc * zc, axis=-1, keepdims=True)
    return zc * lax.rsqrt(var + LN_EPS) * g + b


def _gelu_tanh(x):
    c = math.sqrt(2.0 / math.pi)
    return x * (0.5 * (1.0 + jnp.tanh(c * (x + 0.044715 * (x * x * x)))))


def _log_sigmoid(z):
    return -(jnp.maximum(-z, 0.0) + jnp.log1p(jnp.exp(-jnp.abs(z))))


def _dot(a, b):
    return jnp.dot(a, b, preferred_element_type=F32)


def _dot_nt(a, b):
    return lax.dot_general(a, b, (((1,), (1,)), ((), ())), preferred_element_type=F32)


def _split3(x):
    p1 = x.astype(BF16)
    r1 = x - p1.astype(F32)
    p2 = r1.astype(BF16)
    p3 = (r1 - p2.astype(F32)).astype(BF16)
    return p1, p2, p3


def _proj_ln_body(half, lo_ref, hi_ref, w_ref, x_ref, g_ref, b_ref, o_ref):
    in_lo = (pl.program_id(0) % (2 * half)) < half
    y = _dot(jnp.where(in_lo, lo_ref[0], hi_ref[0]), w_ref[...])
    o_ref[...] = _layer_norm(ALPHA * x_ref[...] + y, g_ref[...], b_ref[...])


def _proj_ln(a_lo, a_hi, w, x, g, b):
    _, s_half, k = a_lo.shape
    n, d = x.shape
    tm = FFN_TM
    half = s_half // tm
    return pl.pallas_call(
        functools.partial(_proj_ln_body, half),
        grid=(n // tm,),
        in_specs=[
            pl.BlockSpec((1, tm, k), lambda i: (i // (2 * half), jnp.minimum(i % (2 * half), half - 1), 0)),
            pl.BlockSpec((1, tm, k), lambda i: (i // (2 * half), jnp.maximum(i % (2 * half) - half, 0), 0)),
            pl.BlockSpec((k, d), lambda i: (0, 0)),
            pl.BlockSpec((tm, d), lambda i: (i, 0)),
            pl.BlockSpec((1, d), lambda i: (0, 0)),
            pl.BlockSpec((1, d), lambda i: (0, 0)),
        ],
        out_specs=pl.BlockSpec((tm, d), lambda i: (i, 0)),
        out_shape=jax.ShapeDtypeStruct((n, d), F32),
        compiler_params=_cparams(("parallel",)),
        name="proj_ln",
    )(a_lo, a_hi, w, x, g, b)


def _ffn_body(seq_tiles, x_ref, xp_ref, wa_ref, wb_ref, cwa_ref, cwb_ref, wd_ref, g_ref, b_ref,
              o_ref, acc_ref):
    i = pl.program_id(0)
    tm = x_ref.shape[0]
    x = x_ref[...]
    starts_seq = (i % seq_tiles) == 0
    halo = jnp.where(starts_seq, 0.0, xp_ref[...]).astype(BF16)
    xcat = jnp.concatenate([halo, x.astype(BF16)], axis=0)
    acc_ref[...] = jnp.zeros_like(acc_ref)
    n_chunks = wa_ref.shape[0]

    def conv(h, cw):
        return (cw[0:1] * h[HALO - 2:HALO - 2 + tm] + cw[1:2] * h[HALO - 1:HALO - 1 + tm]
                + cw[2:3] * h[HALO:HALO + tm] + cw[3:4])

    def chunk(c, carry):
        ha = conv(_dot(xcat, wa_ref[c]), cwa_ref[c])
        hb = conv(_dot(xcat, wb_ref[c]), cwb_ref[c])
        gated = (_gelu_tanh(ha) * hb).astype(BF16)
        acc_ref[...] += _dot(gated, wd_ref[c])
        return carry

    lax.fori_loop(0, n_chunks, chunk, 0)
    o_ref[...] = _layer_norm(ALPHA * x + acc_ref[...], g_ref[...], b_ref[...])


def _ffn(x, seq_len, wa, wb, cwa, cwb, wd, g, b):
    n, d = x.shape
    tm = FFN_TM
    nc = wa.shape[0]
    cf = wa.shape[2]
    hblk = tm // HALO
    return pl.pallas_call(
        functools.partial(_ffn_body, seq_len // tm),
        grid=(n // tm,),
        in_specs=[
            pl.BlockSpec((tm, d), lambda i: (i, 0)),
            pl.BlockSpec((HALO, d), lambda i: (jnp.maximum(i * hblk - 1, 0), 0)),
            pl.BlockSpec((nc, d, cf), lambda i: (0, 0, 0)),
            pl.BlockSpec((nc, d, cf), lambda i: (0, 0, 0)),
            pl.BlockSpec((nc, 4, cf), lambda i: (0, 0, 0)),
            pl.BlockSpec((nc, 4, cf), lambda i: (0, 0, 0)),
            pl.BlockSpec((nc, cf, d), lambda i: (0, 0, 0)),
            pl.BlockSpec((1, d), lambda i: (0, 0)),
            pl.BlockSpec((1, d), lambda i: (0, 0)),
        ],
        out_specs=pl.BlockSpec((tm, d), lambda i: (i, 0)),
        out_shape=jax.ShapeDtypeStruct((n, d), F32),
        scratch_shapes=[pltpu.VMEM((tm, d), F32)],
        compiler_params=_cparams(("parallel",)),
        name="conv_ffn",
    )(x, x, wa, wb, cwa, cwb, wd, g, b)


def _ffn_weights(w_up, conv_w, conv_b, w_down):
    nc = D_FF // FF_CHUNK

    def up(w):
        return w.reshape(D_MODEL, nc, FF_CHUNK).transpose(1, 0, 2).astype(BF16)

    def taps(cw, cb):
        t = jnp.concatenate([cw, cb[None]], axis=0)
        return t.reshape(4, nc, FF_CHUNK).transpose(1, 0, 2)

    wa, wb = up(w_up[:, :D_FF]), up(w_up[:, D_FF:])
    cwa, cwb = taps(conv_w[:, :D_FF], conv_b[:D_FF]), taps(conv_w[:, D_FF:], conv_b[D_FF:])
    wd = w_down.reshape(nc, FF_CHUNK, D_MODEL).astype(BF16)
    return wa, wb, cwa, cwb, wd


_NSA_T_ROWS = HQ + 2 * NSA_KV + N_BRANCH * N_HEADS
_NSA_N_COLS = 2 * NSA_KV + 2 * NSA_GROUPS * AUG


def _nsa_inproj_body(seq_tiles, x_ref, wt_ref, wn_ref, bg_ref, cost_ref, sint_ref, cosn_ref, sinn_ref,
                     qn_ref, qr_ref, vs_ref, vw_ref, gt_ref, ksa_ref, kwa_ref, kc_ref, vc_ref):
    i = pl.program_id(0)
    xb = x_ref[...].astype(BF16)
    t = _dot_nt(wt_ref[...], xb)
    q = t[0:HQ] * QSCALE
    qn_ref[0] = q.astype(BF16)
    q3 = q.reshape(N_HEADS, HEAD_DIM, TT)
    half = HEAD_DIM // 2
    rot = jnp.concatenate([-q3[:, half:], q3[:, :half]], axis=1)
    qr = q3 * cost_ref[0][None] + rot * sint_ref[0][None]
    qr_ref[0] = qr.reshape(HQ, TT).astype(BF16)
    ones = jnp.ones((NSA_GROUPS, VROWS - HEAD_DIM, TT), BF16)
    for v_ref, lo in ((vs_ref, HQ), (vw_ref, HQ + NSA_KV)):
        v_ref[0, :, 0:HEAD_DIM, :] = t[lo:lo + NSA_KV].reshape(NSA_GROUPS, HEAD_DIM, TT).astype(BF16)
        v_ref[0, :, HEAD_DIM:VROWS, :] = ones
    gt_ref[0] = jax.nn.sigmoid(t[HQ + 2 * NSA_KV:] + bg_ref[...])

    n = _dot(xb, wn_ref[...])
    kc_ref[...] = n[:, 0:NSA_KV]
    vc_ref[...] = n[:, NSA_KV:2 * NSA_KV]
    width = NSA_GROUPS * AUG
    lane = lax.broadcasted_iota(jnp.int32, (TT, width), 1) % AUG
    cosn, sinn = cosn_ref[...], sinn_ref[...]

    def rope_nat(k):
        rh = jnp.where(lane < half, -pltpu.roll(k, width - half, axis=1), pltpu.roll(k, half, axis=1))
        return k * cosn + rh * sinn

    ks = rope_nat(n[:, 2 * NSA_KV:2 * NSA_KV + width])
    kw = rope_nat(n[:, 2 * NSA_KV + width:])
    row = lax.broadcasted_iota(jnp.int32, (TT, width), 0)
    blk = ((i % seq_tiles) * TT + row) // SEL_LEN
    onehot = (lane - HEAD_DIM == blk).astype(F32)
    ksa_ref[...] = (ks + onehot).astype(BF16)
    kwa_ref[...] = kw.astype(BF16)


def _nsa_inproj(x, seq_len, wt, wn, bg, cost, sint, cosn, sinn):
    n = x.shape[0]
    nt = n // TT
    seq_tiles = seq_len // TT
    width = NSA_GROUPS * AUG
    full = lambda r, c: pl.BlockSpec((r, c), lambda i: (0, 0))
    tile3 = lambda r: pl.BlockSpec((1, r, TT), lambda i: (i, 0, 0))
    nat = lambda c: pl.BlockSpec((TT, c), lambda i: (i, 0))
    vtile = pl.BlockSpec((1, NSA_GROUPS, VROWS, TT), lambda i: (i, 0, 0, 0))
    return pl.pallas_call(
        functools.partial(_nsa_inproj_body, seq_tiles),
        grid=(nt,),
        in_specs=[
            nat(D_MODEL),
            full(_NSA_T_ROWS, D_MODEL),
            full(D_MODEL, _NSA_N_COLS),
            full(N_BRANCH * N_HEADS, 1),
            pl.BlockSpec((1, HEAD_DIM, TT), lambda i: (i % seq_tiles, 0, 0)),
            pl.BlockSpec((1, HEAD_DIM, TT), lambda i: (i % seq_tiles, 0, 0)),
            pl.BlockSpec((TT, width), lambda i: (i % seq_tiles, 0)),
            pl.BlockSpec((TT, width), lambda i: (i % seq_tiles, 0)),
        ],
        out_specs=[tile3(HQ), tile3(HQ), vtile, vtile, tile3(N_BRANCH * N_HEADS),
                   nat(width), nat(width), nat(NSA_KV), nat(NSA_KV)],
        out_shape=[
            jax.ShapeDtypeStruct((nt, HQ, TT), BF16),
            jax.ShapeDtypeStruct((nt, HQ, TT), BF16),
            jax.ShapeDtypeStruct((nt, NSA_GROUPS, VROWS, TT), BF16),
            jax.ShapeDtypeStruct((nt, NSA_GROUPS, VROWS, TT), BF16),
            jax.ShapeDtypeStruct((nt, N_BRANCH * N_HEADS, TT), F32),
            jax.ShapeDtypeStruct((n, width), BF16),
            jax.ShapeDtypeStruct((n, width), BF16),
            jax.ShapeDtypeStruct((n, NSA_KV), F32),
            jax.ShapeDtypeStruct((n, NSA_KV), F32),
        ],
        compiler_params=_cparams(("parallel",)),
        name="nsa_inproj",
    )(x, wt, wn, bg, cost, sint, cosn, sinn)


def _pad_heads(w, n_heads):
    w3 = w.reshape(w.shape[0], n_heads, HEAD_DIM)
    return jnp.concatenate([w3, jnp.zeros_like(w3)], axis=-1).reshape(w.shape[0], n_heads * AUG)


def _nsa_inproj_weights(w_in, b_gate):
    cuts = [HQ + i * NSA_KV for i in range(7)]
    wq, wkc, wvc, wks, wvs, wkw, wvw, wg = jnp.split(w_in, cuts, axis=1)
    wt = jnp.concatenate([wq, wvs, wvw, wg], axis=1).T.astype(BF16)
    wn = jnp.concatenate([wkc, wvc, _pad_heads(wks, NSA_GROUPS), _pad_heads(wkw, NSA_GROUPS)],
                         axis=1).astype(BF16)
    return wt, wn, b_gate.reshape(-1, 1)


def _rope_tables(seq_len):
    inv = ROPE_THETA ** (-jnp.arange(0, HEAD_DIM, 2, dtype=F32) / HEAD_DIM)
    ang = jnp.arange(seq_len, dtype=F32)[:, None] * inv[None, :]
    ang = jnp.concatenate([ang, ang], axis=-1)
    cos, sin = jnp.cos(ang), jnp.sin(ang)
    seq_tiles = seq_len // TT

    def transposed(t):
        return t.reshape(seq_tiles, TT, HEAD_DIM).transpose(0, 2, 1)

    def natural(t):
        return jnp.tile(jnp.concatenate([t, jnp.zeros_like(t)], axis=1), (1, NSA_GROUPS))

    return transposed(cos), transposed(sin), natural(cos), natural(sin)


def _compress_body(hk_ref, hv_ref, pe_ref, w1k_ref, b1k_ref, w2k_ref, w1vt_ref, b1v_ref, w2vt_ref,
                   kc_ref, vct_ref):
    nc = hk_ref.shape[1]
    pe_top, pe_bot = pe_ref[0:1], pe_ref[1:2]
    hk = hk_ref[0]
    top = _dot((hk + pe_top).astype(BF16), w1k_ref[0])
    bot = _dot((hk + pe_bot).astype(BF16), w1k_ref[1])
    hid = _gelu_tanh(top + pltpu.roll(bot, nc - 1, axis=0) + b1k_ref[...])
    kc_ref[0] = _dot(hid.astype(BF16), w2k_ref[...]).astype(BF16)

    hv = hv_ref[0]
    top_t = _dot_nt(w1vt_ref[0], (hv + pe_top).astype(BF16))
    bot_t = _dot_nt(w1vt_ref[1], (hv + pe_bot).astype(BF16))
    hid_t = _gelu_tanh(top_t + pltpu.roll(bot_t, nc - 1, axis=1) + b1v_ref[...])
    vct_ref[0] = _dot(w2vt_ref[...], hid_t.astype(BF16)).astype(BF16)


def _compress(hk, hv, pe2, w1k, b1k, w2k, w1vt, b1v, w2vt):
    bg, nc, hw = hk.shape
    cst = lambda shape: pl.BlockSpec(shape, lambda i: (0,) * len(shape))
    return pl.pallas_call(
        _compress_body,
        grid=(bg,),
        in_specs=[
            pl.BlockSpec((1, nc, hw), lambda i: (i, 0, 0)),
            pl.BlockSpec((1, nc, hw), lambda i: (i, 0, 0)),
            cst((2, hw)),
            cst((2, hw, PHI_HIDDEN)), cst((1, PHI_HIDDEN)), cst((PHI_HIDDEN, HEAD_DIM)),
            cst((2, PHI_HIDDEN, hw)), cst((PHI_HIDDEN, 1)), cst((HEAD_DIM, PHI_HIDDEN)),
        ],
        out_specs=[pl.BlockSpec((1, nc, HEAD_DIM), lambda i: (i, 0, 0)),
                   pl.BlockSpec((1, HEAD_DIM, nc), lambda i: (i, 0, 0))],
        out_shape=[jax.ShapeDtypeStruct((bg, nc, HEAD_DIM), BF16),
                   jax.ShapeDtypeStruct((bg, HEAD_DIM, nc), BF16)],
        compiler_params=_cparams(("parallel",)),
        name="nsa_compress",
    )(hk, hv, pe2, w1k, b1k, w2k, w1vt, b1v, w2vt)


def _slab_max(s):
    return jnp.max(s.reshape(-1, 8, s.shape[-1]), axis=0)


def _slab_sum(p):
    return jnp.sum(p.reshape(-1, 8, p.shape[-1]), axis=0)


def _paired_sweep(nq, qi, q_of, k_tile, v_tile, bias_row, s_ref, c_ref):
    tpos = lax.broadcasted_iota(jnp.int32, (1, TT), 1)
    kpos = lax.broadcasted_iota(jnp.int32, (TT, 1), 0)
    causal = kpos <= tpos
    half = nq // 2
    slots = []
    for s in range(nq + 1):
        if s < half:
            lo = s <= qi
            slots.append((lo, jnp.where(lo, s, s - qi - 1), jnp.where(lo, 0, 1)))
        else:
            slots.append((False, s - qi - 1, 1))

    slot_max, bias = [], []
    for s, (lo, kv, x) in enumerate(slots):
        sc = _dot(k_tile(kv), q_of(x))
        if s == nq:
            sc = jnp.where(causal, sc, MASK_BIAS)
        s_ref[s] = sc
        b = bias_row(lo, kv)
        bias.append(b)
        slot_max.append(_slab_max(sc) if b is None else _slab_max(sc) + b)
    diag = jnp.where(causal, s_ref[qi], MASK_BIAS)
    s_ref[qi] = diag
    m_lo = _slab_max(diag)
    m_hi = slot_max[nq]
    for s in range(nq):
        if s < half:
            m_lo = jnp.maximum(m_lo, jnp.where(s < qi, slot_max[s], MASK_BIAS))
            m_hi = jnp.maximum(m_hi, jnp.where(s > qi, slot_max[s], MASK_BIAS))
        else:
            m_hi = jnp.maximum(m_hi, slot_max[s])
    m_lo = jnp.max(m_lo, axis=0, keepdims=True)
    m_hi = jnp.max(m_hi, axis=0, keepdims=True)

    acc_lo = jnp.zeros((OROWS, TT), F32)
    acc_hi = jnp.zeros((OROWS, TT), F32)
    for s, (lo, kv, x) in enumerate(slots):
        m_row = m_hi if lo is False else jnp.where(lo, m_lo, m_hi)
        if bias[s] is not None:
            m_row = m_row - bias[s]
        p = jnp.exp2(s_ref[s] - m_row)
        pv = _dot(v_tile(kv), p.astype(BF16))[0:OROWS]
        if lo is False:
            acc_hi = acc_hi + pv
        else:
            c_ref[s] = pv
    for s in range(half):
        lo, c = slots[s][0], c_ref[s]
        acc_lo = acc_lo + jnp.where(lo, c, 0.0)
        acc_hi = acc_hi + jnp.where(lo, 0.0, c)
    o_lo = acc_lo[0:HEAD_DIM] * (1.0 / acc_lo[HEAD_DIM:HEAD_DIM + 1])
    o_hi = acc_hi[0:HEAD_DIM] * (1.0 / acc_hi[HEAD_DIM:HEAD_DIM + 1])
    return o_lo, o_hi


def _nsa_side(n_sb, qt, clamp, qn_ref, qr_ref, kc_ref, vct_ref, ovt_ref, kw_ref, vw_ref):
    t0 = qt * TT
    tpos = t0 + lax.broadcasted_iota(jnp.int32, (1, TT), 1)
    kpos0 = lax.broadcasted_iota(jnp.int32, (TT, 1), 0)
    nc = kc_ref.shape[1]

    kc = kc_ref[0]
    cmp_end = lax.broadcasted_iota(jnp.int32, (nc, 1), 0) * CMP_STRIDE + (CMP_LEN - 1)
    cmask = cmp_end <= tpos
    psum = jnp.zeros((nc, TT), F32)
    o_cmp = []
    for r in range(NSA_REP):
        s = _dot(kc, qn_ref[0, r * HEAD_DIM:(r + 1) * HEAD_DIM, :])
        s = jnp.where(cmask, s, -jnp.inf)
        m = jnp.max(_slab_max(s), axis=0, keepdims=True)
        m = jnp.where(m == -jnp.inf, 0.0, m)
        e = jnp.exp2(s - m)
        d = jnp.sum(_slab_sum(e), axis=0, keepdims=True)
        p = e * (1.0 / jnp.where(d > 0, d, 1.0))
        psum = psum + p
        o_cmp.append(_dot(vct_ref[0], p.astype(BF16)))
    p_hi = psum.astype(BF16)
    p_lo = (psum - p_hi.astype(F32)).astype(BF16)
    imp = _dot(ovt_ref[...], p_hi) + _dot(ovt_ref[...], p_lo)

    nrow = ovt_ref.shape[0]
    blk = lax.broadcasted_iota(jnp.int32, (nrow, 1), 0)
    cur = tpos // SEL_LEN
    forced = (blk == 0) | (blk == cur) | (blk == cur - 1)
    valid = blk * SEL_LEN <= tpos
    score = jnp.where(forced, FORCE_SCORE, jnp.where(valid, imp, -1.0))
    score = jnp.where(blk < n_sb, score, -2.0)
    rank = jnp.zeros((nrow, TT), F32)
    for k in range(n_sb):
        sk = score[k:k + 1, :]
        beats = (sk > score) | ((sk == score) & (blk > k))
        rank = rank + jnp.where(beats, 1.0, 0.0)
    sel_bias = jnp.where(rank < min(N_SEL, n_sb), 0.0, MASK_BIAS).astype(BF16)

    j1 = jnp.maximum(qt - 1, 0) if clamp else qt - 1
    j2 = jnp.maximum(qt - 2, 0) if clamp else qt - 2
    in_window = tpos - (j2 * TT + kpos0) < WINDOW
    masks = (kpos0 <= tpos - t0,
             jnp.broadcast_to(qt >= 1, (TT, TT)) if clamp else None,
             ((qt >= 2) & in_window) if clamp else in_window)
    k_tiles = [kw_ref[0, pl.ds(pl.multiple_of(j * TT, TT), TT), :][:, 0:HEAD_DIM] for j in (qt, j1, j2)]
    v_tiles = [vw_ref[0, j] for j in (qt, j1, j2)]
    o_win = []
    for r in range(NSA_REP):
        q = qr_ref[0, r * HEAD_DIM:(r + 1) * HEAD_DIM, :]
        scores = []
        for k, mask in zip(k_tiles, masks):
            s = _dot(k, q)
            scores.append(s if mask is None else jnp.where(mask, s, MASK_BIAS))
        m = functools.reduce(jnp.maximum, [_slab_max(s) for s in scores])
        m = jnp.max(m, axis=0, keepdims=True)
        acc = sum(_dot(v, jnp.exp2(s - m).astype(BF16))[0:OROWS] for v, s in zip(v_tiles, scores))
        o_win.append(acc[0:HEAD_DIM] * (1.0 / acc[HEAD_DIM:HEAD_DIM + 1]))
    return o_cmp, o_win, sel_bias


def _nsa_attn_body(nq, n_sb, qnl_ref, qnh_ref, qrl_ref, qrh_ref, kc_ref, vct_ref, ovt_ref, ks_ref, vs_ref,
                   kw_ref, vw_ref, gl_ref, gh_ref, olo_ref, ohi_ref, q_ref, *scratch):
    qi = pl.program_id(2)
    s_refs, c_refs = scratch[:NSA_REP], scratch[NSA_REP:]
    sides = []
    for x, (qt, qn_ref, qr_ref) in enumerate(((qi, qnl_ref, qrl_ref), (nq - 1 - qi, qnh_ref, qrh_ref))):
        o_cmp, o_win, sel_bias = _nsa_side(n_sb, qt, x == 0 or nq < 6, qn_ref, qr_ref, kc_ref, vct_ref,
                                           ovt_ref, kw_ref, vw_ref)
        for r in range(NSA_REP):
            q_ref[x, r, 0:HEAD_DIM, :] = qr_ref[0, r * HEAD_DIM:(r + 1) * HEAD_DIM, :]
            q_ref[x, r, HEAD_DIM:AUG, :] = sel_bias
        sides.append((o_cmp, o_win))

    k_tile = lambda kv: ks_ref[0, pl.ds(pl.multiple_of(kv * TT, TT), TT), :]
    v_tile = lambda kv: vs_ref[0, kv]
    o_sel = [_paired_sweep(nq, qi, lambda x, r=r: q_ref[x, r], k_tile, v_tile, lambda lo, kv: None,
                           s_refs[r], c_refs[r]) for r in range(NSA_REP)]

    for x, (g_ref, o_ref) in enumerate(((gl_ref, olo_ref), (gh_ref, ohi_ref))):
        o_cmp, o_win = sides[x]
        g = g_ref[0, 0]
        rows = []
        for r in range(NSA_REP):
            gc = g[r * N_BRANCH + 0:r * N_BRANCH + 1]
            gs = g[r * N_BRANCH + 1:r * N_BRANCH + 2]
            gw = g[r * N_BRANCH + 2:r * N_BRANCH + 3]
            rows.append(gc * o_cmp[r] + gs * o_sel[r][x] + gw * o_win[r])
        o_ref[0] = jnp.concatenate(rows, axis=0).T.astype(BF16)


def _nsa_attn(batch, seq_len, qn, qr, kc, vct, ovt, ksa, vs, kwa, vw, gt):
    nq = seq_len // TT
    nc = kc.shape[1]
    n_sb = seq_len // SEL_LEN
    g_rows = NSA_REP * N_BRANCH
    half = nq // 2
    ksa3 = ksa.reshape(batch, seq_len, NSA_GROUPS * AUG)
    kwa3 = kwa.reshape(batch, seq_len, NSA_GROUPS * AUG)
    vs5 = vs.reshape(batch, nq, NSA_GROUPS, VROWS, TT)
    vw5 = vw.reshape(batch, nq, NSA_GROUPS, VROWS, TT)
    gt4 = gt.reshape(batch * nq, NSA_GROUPS, g_rows, TT)
    lo_tile = lambda b, q: b * nq + q
    hi_tile = lambda b, q: b * nq + nq - 1 - q
    q_lo = pl.BlockSpec((1, NSA_REP * HEAD_DIM, TT), lambda b, g, q: (lo_tile(b, q), g, 0))
    q_hi = pl.BlockSpec((1, NSA_REP * HEAD_DIM, TT), lambda b, g, q: (hi_tile(b, q), g, 0))
    kspec = pl.BlockSpec((1, seq_len, AUG), lambda b, g, q: (b, 0, g))
    vspec = pl.BlockSpec((1, nq, None, VROWS, TT), lambda b, g, q: (b, 0, g, 0, 0))
    out = jax.ShapeDtypeStruct((batch, seq_len // 2, HQ), BF16)
    return pl.pallas_call(
        functools.partial(_nsa_attn_body, nq, n_sb),
        grid=(batch, NSA_GROUPS, half),
        in_specs=[
            q_lo, q_hi, q_lo, q_hi,
            pl.BlockSpec((1, nc, HEAD_DIM), lambda b, g, q: (b * NSA_GROUPS + g, 0, 0)),
            pl.BlockSpec((1, HEAD_DIM, nc), lambda b, g, q: (b * NSA_GROUPS + g, 0, 0)),
            pl.BlockSpec(ovt.shape, lambda b, g, q: (0, 0)),
            kspec, vspec, kspec, vspec,
            pl.BlockSpec((1, 1, g_rows, TT), lambda b, g, q: (lo_tile(b, q), g, 0, 0)),
            pl.BlockSpec((1, 1, g_rows, TT), lambda b, g, q: (hi_tile(b, q), g, 0, 0)),
        ],
        out_specs=[pl.BlockSpec((1, TT, NSA_REP * HEAD_DIM), lambda b, g, q: (b, q, g)),
                   pl.BlockSpec((1, TT, NSA_REP * HEAD_DIM), lambda b, g, q: (b, half - 1 - q, g))],
        out_shape=[out, out],
        scratch_shapes=([pltpu.VMEM((2, NSA_REP, AUG, TT), BF16)]
                        + [pltpu.VMEM((nq + 1, TT, TT), F32)] * NSA_REP
                        + [pltpu.VMEM((half, OROWS, TT), F32)] * NSA_REP),
        compiler_params=_cparams(("parallel", "parallel", "arbitrary")),
        name="nsa_attn",
    )(qn, qn, qr, qr, kc, vct, ovt, ksa3, vs5, kwa3, vw5, gt4, gt4)


def _overlap_t(seq_len, nc):
    n_cmp = (seq_len - CMP_LEN) // CMP_STRIDE + 1
    n_sb = seq_len // SEL_LEN
    cmp_start = jnp.arange(n_cmp) * CMP_STRIDE
    sel_start = jnp.arange(n_sb) * SEL_LEN
    ov = jnp.clip(jnp.minimum(cmp_start[:, None] + CMP_LEN, sel_start[None, :] + SEL_LEN)
                  - jnp.maximum(cmp_start[:, None], sel_start[None, :]), 0, None).astype(F32) / CMP_LEN
    out = jnp.zeros((SEL_LEN, nc), F32).at[:n_sb, :n_cmp].set(ov.T)
    return out.astype(BF16)


def _nsa_attention(x, batch, seq_len, w_in, b_gate, pe, k_w1, k_b1, k_w2, v_w1, v_b1, v_w2, tables):
    wt, wn, bg = _nsa_inproj_weights(w_in, b_gate)
    qn, qr, vs, vw, gt, ksa, kwa, kc, vc = _nsa_inproj(x, seq_len, wt, wn, bg, *tables)

    nc = seq_len // CMP_STRIDE
    hw = CMP_STRIDE * HEAD_DIM

    def half_blocks(t):
        t = t.reshape(batch, seq_len, NSA_GROUPS, HEAD_DIM).transpose(0, 2, 1, 3)
        return t.reshape(batch * NSA_GROUPS, nc, hw)

    kcmp, vcmp_t = _compress(
        half_blocks(kc), half_blocks(vc), pe.reshape(2, hw),
        k_w1.reshape(2, hw, PHI_HIDDEN).astype(BF16), k_b1.reshape(1, -1), k_w2.astype(BF16),
        v_w1.reshape(2, hw, PHI_HIDDEN).transpose(0, 2, 1).astype(BF16), v_b1.reshape(-1, 1),
        v_w2.T.astype(BF16))
    return _nsa_attn(batch, seq_len, qn, qr, kcmp, vcmp_t, _overlap_t(seq_len, nc), ksa, vs, kwa, vw, gt)


def _nsa_layer(x, batch, seq_len, w_in, b_gate, pe, k_w1, k_b1, k_w2, v_w1, v_b1, v_w2, w_o,
               ln_g, ln_b, tables):
    a_lo, a_hi = _nsa_attention(x, batch, seq_len, w_in, b_gate, pe, k_w1, k_b1, k_w2, v_w1, v_b1, v_w2,
                                tables)
    return _proj_ln(a_lo, a_hi, w_o.astype(BF16), x, ln_g.reshape(1, -1), ln_b.reshape(1, -1))


_FOX_T_ROWS = 2 * HQ + N_HEADS
_FOX_N_COLS = N_HEADS * AUG + 128
_N_PIECE = 3


def _fox_inproj_body(seq_tiles, x_ref, wt_ref, wn_ref, bft_ref, bfn_ref, place_ref, ones_ref, route_ref,
                     qa_ref, ka_ref, vt_ref, off_ref, run_ref):
    i = pl.program_id(0)
    xb = x_ref[...].astype(BF16)
    t = _dot_nt(wt_ref[...], xb)
    vt_ref[0, :, 0:HEAD_DIM, :] = t[HQ:2 * HQ].reshape(N_HEADS, HEAD_DIM, TT).astype(BF16)
    vt_ref[0, :, HEAD_DIM:VROWS, :] = jnp.ones((N_HEADS, VROWS - HEAD_DIM, TT), BF16)
    lf_t = _log_sigmoid(t[2 * HQ:] + bft_ref[...]) * LOG2E
    n = _dot(xb, wn_ref[...])
    kw = N_HEADS * AUG
    lf_n = _log_sigmoid(n[:, kw:kw + _N_PIECE * N_HEADS] + bfn_ref[...]) * LOG2E

    r_i = lax.broadcasted_iota(jnp.int32, (TT, TT), 0)
    c_i = lax.broadcasted_iota(jnp.int32, (TT, TT), 1)
    upper = ((r_i > 0) & (r_i <= c_i)).astype(BF16)
    lower = ((c_i > 0) & (c_i <= r_i)).astype(BF16)
    a_t = sum(_dot(p, upper) for p in _split3(lf_t))
    a_n = sum(_dot(lower, p) for p in _split3(lf_n))

    q3 = (t[0:HQ] * QSCALE).reshape(N_HEADS, HEAD_DIM, TT)
    qa_ref[0, :, 0:HEAD_DIM, :] = q3.astype(BF16)
    stacked = jnp.concatenate(list(_split3(a_t)) + [jnp.ones((N_HEADS, TT), BF16)], axis=0)
    for h in range(N_HEADS):
        qa_ref[0, h, HEAD_DIM:AUG, :] = _dot(route_ref[h], stacked).astype(BF16)

    b1, b2, b3 = _split3(-a_n)
    grp = lax.broadcasted_iota(jnp.int32, (TT, _N_PIECE * N_HEADS), 1) // N_HEADS
    bsel = jnp.where(grp == 0, b1, jnp.where(grp == 1, b2, b3))
    ka_ref[...] = (n[:, 0:kw] + _dot(bsel, place_ref[...]) + ones_ref[...]).astype(BF16)

    @pl.when(i % seq_tiles == 0)
    def _():
        run_ref[...] = jnp.zeros_like(run_ref)

    first = lf_t[:, 0:1]
    off_ref[0] = jnp.broadcast_to(run_ref[:, 0:1] + first, (N_HEADS, TT))
    run_ref[...] = run_ref[...] + (a_t[:, TT - 1:TT] + first)


def _fox_inproj(x, seq_len, wt, wn, bft, bfn, place, ones, route):
    n = x.shape[0]
    nt = n // TT
    kw = N_HEADS * AUG
    full = lambda r, c: pl.BlockSpec((r, c), lambda i: (0, 0))
    return pl.pallas_call(
        functools.partial(_fox_inproj_body, seq_len // TT),
        grid=(nt,),
        in_specs=[
            pl.BlockSpec((TT, D_MODEL), lambda i: (i, 0)),
            full(_FOX_T_ROWS, D_MODEL), full(D_MODEL, _FOX_N_COLS),
            full(N_HEADS, 1), full(1, _N_PIECE * N_HEADS),
            full(_N_PIECE * N_HEADS, kw), full(1, kw),
            pl.BlockSpec((N_HEADS, AUG - HEAD_DIM, AUG - HEAD_DIM), lambda i: (0, 0, 0)),
        ],
        out_specs=[
            pl.BlockSpec((1, N_HEADS, AUG, TT), lambda i: (i, 0, 0, 0)),
            pl.BlockSpec((TT, kw), lambda i: (i, 0)),
            pl.BlockSpec((1, N_HEADS, VROWS, TT), lambda i: (i, 0, 0, 0)),
            pl.BlockSpec((1, N_HEADS, TT), lambda i: (i, 0, 0)),
        ],
        out_shape=[
            jax.ShapeDtypeStruct((nt, N_HEADS, AUG, TT), BF16),
            jax.ShapeDtypeStruct((n, kw), BF16),
            jax.ShapeDtypeStruct((nt, N_HEADS, VROWS, TT), BF16),
            jax.ShapeDtypeStruct((nt, N_HEADS, TT), F32),
        ],
        scratch_shapes=[pltpu.VMEM((N_HEADS, 128), F32)],
        compiler_params=_cparams(("arbitrary",)),
        name="fox_inproj",
    )(x, wt, wn, bft, bfn, place, ones, route)


def _fox_inproj_weights(w_in, b_f):
    wq, wk, wv, wf = jnp.split(w_in, [HQ, 2 * HQ, 3 * HQ], axis=1)
    wt = jnp.concatenate([wq, wv, wf], axis=1).T.astype(BF16)
    pad = jnp.zeros((D_MODEL, _FOX_N_COLS - N_HEADS * AUG - _N_PIECE * N_HEADS), F32)
    wn = jnp.concatenate([_pad_heads(wk, N_HEADS)] + [wf] * _N_PIECE + [pad], axis=1).astype(BF16)
    bft = b_f.reshape(-1, 1)
    bfn = jnp.tile(b_f, _N_PIECE).reshape(1, -1)
    rows = jnp.arange(_N_PIECE * N_HEADS)
    cols = (rows % N_HEADS) * AUG + HEAD_DIM + _N_PIECE + rows // N_HEADS
    place = jnp.zeros((_N_PIECE * N_HEADS, N_HEADS * AUG), F32).at[rows, cols].set(1.0).astype(BF16)
    lane = jnp.arange(N_HEADS * AUG) % AUG
    ones = ((lane >= HEAD_DIM) & (lane < HEAD_DIM + _N_PIECE)).astype(F32).reshape(1, -1)
    hh = jnp.arange(N_HEADS)
    route = jnp.zeros((N_HEADS, AUG - HEAD_DIM, AUG - HEAD_DIM), F32)
    for k in range(_N_PIECE):
        route = route.at[hh, k, k * N_HEADS + hh].set(1.0)
        route = route.at[hh, _N_PIECE + k, _N_PIECE * N_HEADS].set(1.0)
    return wt, wn, bft, bfn, place, ones, route.astype(BF16)


def _fox_attn_body(nq, qlo_ref, qhi_ref, ka_ref, vt_ref, off_ref, olo_ref, ohi_ref, q_ref, *scratch):
    hg = pl.program_id(1)
    qi = pl.program_id(2)
    s_refs, c_refs = scratch[:HEAD_BLOCK], scratch[HEAD_BLOCK:]
    q_ref[0] = qlo_ref[0]
    q_ref[1] = qhi_ref[0]
    for r in range(HEAD_BLOCK):
        h = hg * HEAD_BLOCK + r

        def off_row(t, h=h):
            return off_ref[0, pl.ds(t, 1), pl.ds(h, 1), :].reshape(1, TT)

        off_lo, off_hi = off_row(qi), off_row(nq - 1 - qi)

        def bias_row(lo, kv, off_lo=off_lo, off_hi=off_hi, off_row=off_row):
            base = off_hi if lo is False else jnp.where(lo, off_lo, off_hi)
            return base - off_row(kv)

        def k_tile(kv, r=r):
            return ka_ref[0, pl.ds(pl.multiple_of(kv * TT, TT), TT), r * AUG:(r + 1) * AUG]

        def v_tile(kv, r=r):
            return vt_ref[0, kv, r]

        o_lo, o_hi = _paired_sweep(nq, qi, lambda x, r=r: q_ref[x, r], k_tile, v_tile, bias_row,
                                   s_refs[r], c_refs[r])
        olo_ref[0, r * HEAD_DIM:(r + 1) * HEAD_DIM, :] = o_lo.astype(BF16)
        ohi_ref[0, r * HEAD_DIM:(r + 1) * HEAD_DIM, :] = o_hi.astype(BF16)


def _fox_attn(batch, seq_len, qa, ka, vt, off):
    nq = seq_len // TT
    half = nq // 2
    hb = HEAD_BLOCK
    ka3 = ka.reshape(batch, seq_len, N_HEADS * AUG)
    vt5 = vt.reshape(batch, nq, N_HEADS, VROWS, TT)
    off4 = off.reshape(batch, nq, N_HEADS, TT)
    out = jax.ShapeDtypeStruct((batch * half, HQ, TT), BF16)
    return pl.pallas_call(
        functools.partial(_fox_attn_body, nq),
        grid=(batch, N_HEADS // hb, half),
        in_specs=[
            pl.BlockSpec((1, hb, AUG, TT), lambda b, h, q: (b * nq + q, h, 0, 0)),
            pl.BlockSpec((1, hb, AUG, TT), lambda b, h, q: (b * nq + nq - 1 - q, h, 0, 0)),
            pl.BlockSpec((1, seq_len, hb * AUG), lambda b, h, q: (b, 0, h)),
            pl.BlockSpec((1, nq, hb, VROWS, TT), lambda b, h, q: (b, 0, h, 0, 0)),
            pl.BlockSpec((1, nq, N_HEADS, TT), lambda b, h, q: (b, 0, 0, 0)),
        ],
        out_specs=[pl.BlockSpec((1, hb * HEAD_DIM, TT), lambda b, h, q: (b * half + q, h, 0)),
                   pl.BlockSpec((1, hb * HEAD_DIM, TT), lambda b, h, q: (b * half + half - 1 - q, h, 0))],
        out_shape=[out, out],
        scratch_shapes=([pltpu.VMEM((2, hb, AUG, TT), BF16)]
                        + [pltpu.VMEM((nq + 1, TT, TT), F32)] * hb
                        + [pltpu.VMEM((half, OROWS, TT), F32)] * hb),
        compiler_params=_cparams(("parallel", "parallel", "arbitrary")),
        name="fox_attn",
    )(qa, qa, ka3, vt5, off4)


def _proj_ln_t_body(half, lo_ref, hi_ref, w_ref, x_ref, g_ref, b_ref, o_ref):
    in_lo = (pl.program_id(0) % (2 * half)) < half
    at = jnp.where(in_lo, lo_ref[0], hi_ref[0])
    y = _dot(at.T, w_ref[...])
    o_ref[...] = _layer_norm(ALPHA * x_ref[...] + y, g_ref[...], b_ref[...])


def _proj_ln_t(at_lo, at_hi, seq_len, w, x, g, b):
    k = at_lo.shape[1]
    n, d = x.shape
    nq = seq_len // TT
    half = nq // 2
    return pl.pallas_call(
        functools.partial(_proj_ln_t_body, half),
        grid=(n // TT,),
        in_specs=[
            pl.BlockSpec((1, k, TT), lambda i: ((i // nq) * half + jnp.minimum(i % nq, half - 1), 0, 0)),
            pl.BlockSpec((1, k, TT), lambda i: ((i // nq) * half + jnp.maximum(i % nq - half, 0), 0, 0)),
            pl.BlockSpec((k, d), lambda i: (0, 0)),
            pl.BlockSpec((TT, d), lambda i: (i, 0)),
            pl.BlockSpec((1, d), lambda i: (0, 0)),
            pl.BlockSpec((1, d), lambda i: (0, 0)),
        ],
        out_specs=pl.BlockSpec((TT, d), lambda i: (i, 0)),
        out_shape=jax.ShapeDtypeStruct((n, d), F32),
        compiler_params=_cparams(("parallel",)),
        name="proj_ln_t",
    )(at_lo, at_hi, w, x, g, b)


def _fox_layer(x, batch, seq_len, w_in, b_f, w_o, ln_g, ln_b):
    qa, ka, vt, off = _fox_inproj(x, seq_len, *_fox_inproj_weights(w_in, b_f))
    at_lo, at_hi = _fox_attn(batch, seq_len, qa, ka, vt, off)
    return _proj_ln_t(at_lo, at_hi, seq_len, w_o.astype(BF16), x, ln_g.reshape(1, -1), ln_b.reshape(1, -1))


def kernel(x, nsa_w_in, nsa_b_gate, nsa_pe, nsa_phik_w1, nsa_phik_b1, nsa_phik_w2, nsa_phiv_w1,
           nsa_phiv_b1, nsa_phiv_w2, nsa_w_o, fox_w_in, fox_b_f, fox_w_o, ffn_w_up, ffn_conv_w,
           ffn_conv_b, ffn_w_down, ln1_g, ln1_b, ln2_g, ln2_b):
    batch, seq_len, d = x.shape
    assert d == D_MODEL and seq_len % FFN_TM == 0 and seq_len % TT == 0
    assert seq_len // SEL_LEN <= AUG - HEAD_DIM
    tables = _rope_tables(seq_len)
    h = x.reshape(batch * seq_len, d)
    for i in range(DEPTH):
        j = i // 2
        if i % 2 == 0:
            h = _nsa_layer(h, batch, seq_len, nsa_w_in[j], nsa_b_gate[j], nsa_pe[j], nsa_phik_w1[j],
                           nsa_phik_b1[j], nsa_phik_w2[j], nsa_phiv_w1[j], nsa_phiv_b1[j],
                           nsa_phiv_w2[j], nsa_w_o[j], ln1_g[i], ln1_b[i], tables)
        else:
            h = _fox_layer(h, batch, seq_len, fox_w_in[j], fox_b_f[j], fox_w_o[j], ln1_g[i], ln1_b[i])
        h = _ffn(h, seq_len, *_ffn_weights(ffn_w_up[i], ffn_conv_w[i], ffn_conv_b[i], ffn_w_down[i]),
                 ln2_g[i].reshape(1, -1), ln2_b[i].reshape(1, -1))
    return h.reshape(batch, seq_len, d)
```

```python
import functools
import math

import jax
import jax.numpy as jnp
from jax import lax
from jax.experimental import pallas as pl
from jax.experimental.pallas import tpu as pltpu

F32 = jnp.float32
BF16 = jnp.bfloat16

D_MODEL = 1024
DEPTH = 4
HEAD_DIM = 64
N_HEADS = 16
HQ = N_HEADS * HEAD_DIM
NSA_GROUPS = 4
NSA_REP = 4
NSA_KV = NSA_GROUPS * HEAD_DIM
CMP_LEN = 32
CMP_STRIDE = 16
SEL_LEN = 64
N_SEL = 16
WINDOW = 512
PHI_HIDDEN = 256
N_BRANCH = 3
FORCE_SCORE = 1e4
D_FF = 2816
ROPE_THETA = 10000.0
ALPHA = (2 * DEPTH) ** 0.25
LN_EPS = 1e-5
SCALE = HEAD_DIM ** -0.5
LOG2E = math.log2(math.e)
QSCALE = SCALE * LOG2E
VROWS = 80
OROWS = 72

TT = 256
AUG = 128
MASK_BIAS = -1e30
FF_CHUNK = 256
FFN_TM = 512
HEAD_BLOCK = 4
HALO = 16
VMEM_LIMIT = 56 * 1024 * 1024


def _cparams(sem):
    return pltpu.CompilerParams(dimension_semantics=sem, vmem_limit_bytes=VMEM_LIMIT)


def _layer_norm(z, g, b):
    mu = jnp.mean(z, axis=-1, keepdims=True)
    zc = z - mu
    var = jnp.mean(zc * zc, axis=-1, keepdims=True)
    return zc * lax.rsqrt(var + LN_EPS) * g + b


def _gelu_tanh(x):
    c = math.sqrt(2.0 / math.pi)
    return x * (0.5 * (1.0 + jnp.tanh(c * (x + 0.044715 * (x * x * x)))))


def _log_sigmoid(z):
    return -(jnp.maximum(-z, 0.0) + jnp.log1p(jnp.exp(-jnp.abs(z))))


def _dot(a, b):
    return jnp.dot(a, b, preferred_element_type=F32)


def _dot_nt(a, b):
    return lax.dot_general(a, b, (((1,), (1,)), ((), ())), preferred_element_type=F32)


def _split3(x):
    p1 = x.astype(BF16)
    r1 = x - p1.astype(F32)
    p2 = r1.astype(BF16)
    p3 = (r1 - p2.astype(F32)).astype(BF16)
    return p1, p2, p3


def _proj_ln_body(half, lo_ref, hi_ref, w_ref, x_ref, g_ref, b_ref, o_ref):
    in_lo = (pl.program_id(0) % (2 * half)) < half
    y = _dot(jnp.where(in_lo, lo_ref[0], hi_ref[0]), w_ref[...])
    o_ref[...] = _layer_norm(ALPHA * x_ref[...] + y, g_ref[...], b_ref[...])


def _proj_ln(a_lo, a_hi, w, x, g, b):
    _, s_half, k = a_lo.shape
    n, d = x.shape
    tm = FFN_TM
    half = s_half // tm
    return pl.pallas_call(
        functools.partial(_proj_ln_body, half),
        grid=(n // tm,),
        in_specs=[
            pl.BlockSpec((1, tm, k), lambda i: (i // (2 * half), jnp.minimum(i % (2 * half), half - 1), 0)),
            pl.BlockSpec((1, tm, k), lambda i: (i // (2 * half), jnp.maximum(i % (2 * half) - half, 0), 0)),
            pl.BlockSpec((k, d), lambda i: (0, 0)),
            pl.BlockSpec((tm, d), lambda i: (i, 0)),
            pl.BlockSpec((1, d), lambda i: (0, 0)),
            pl.BlockSpec((1, d), lambda i: (0, 0)),
        ],
        out_specs=pl.BlockSpec((tm, d), lambda i: (i, 0)),
        out_shape=jax.ShapeDtypeStruct((n, d), F32),
        compiler_params=_cparams(("parallel",)),
        name="proj_ln",
    )(a_lo, a_hi, w, x, g, b)


GELU_C0 = math.sqrt(2.0 / math.pi)
GELU_C1 = GELU_C0 * 0.044715


def _ffn_body(seq_tiles, x_ref, xp_ref, wa_ref, wb_ref, cwa_ref, cwb_ref, wd_ref, g_ref, b_ref, o_ref,
              perm_ref):
    i = pl.program_id(0)
    tm, d = x_ref.shape
    nj = tm // 8
    n_lane = d // 128
    for cb in range(n_lane):
        perm_ref[cb] = x_ref[:, cb * 128:(cb + 1) * 128]
    x = jnp.concatenate(
        [jnp.concatenate([perm_ref[cb, pl.ds(j, 8, stride=nj), :] for cb in range(n_lane)], axis=1)
         for j in range(nj)], axis=0)
    starts_seq = (i % seq_tiles) == 0
    halo = jnp.where(starts_seq, 0.0, xp_ref[...]).astype(BF16)
    xcat = jnp.concatenate([halo, x.astype(BF16)], axis=0)
    n_chunks = wa_ref.shape[0]
    first_sublane = lax.broadcasted_iota(jnp.int32, (8, 1), 0) == 0

    def up(c):
        return _dot(xcat, wa_ref[c]), _dot(xcat, wb_ref[c])

    def conv(h, cw):
        h3 = h[HALO:].reshape(nj, 8, h.shape[-1])

        def wrap(slab, halo_row):
            return jnp.where(first_sublane, halo_row, pltpu.roll(slab, 1, axis=0))[None]

        prev1 = wrap(h3[nj - 1], h[HALO - 1:HALO])
        prev2 = wrap(h3[nj - 2], h[HALO - 2:HALO - 1])
        s1 = jnp.concatenate([prev1, h3[:-1]], axis=0)
        s2 = jnp.concatenate([prev2, prev1, h3[:-2]], axis=0)
        return (cw[0:1] * s2 + cw[1:2] * s1 + cw[2:3] * h3 + cw[3:4]).reshape(tm, h.shape[-1])

    y = None
    nxt = up(0)
    for c in range(n_chunks):
        cur, nxt = nxt, (up(c + 1) if c + 1 < n_chunks else None)
        ha = conv(cur[0], cwa_ref[c])
        hb = conv(cur[1], cwb_ref[c])
        u = ha * (GELU_C0 + GELU_C1 * (ha * ha))
        gated = ((ha + ha * jnp.tanh(u)) * hb).astype(BF16)
        part = _dot(gated, wd_ref[c])
        y = part if y is None else y + part
    out = _layer_norm(ALPHA * x + y, g_ref[...], b_ref[...])
    for j in range(nj):
        for cb in range(n_lane):
            perm_ref[cb, pl.ds(j, 8, stride=nj), :] = out[j * 8:(j + 1) * 8, cb * 128:(cb + 1) * 128]
    for cb in range(n_lane):
        o_ref[:, cb * 128:(cb + 1) * 128] = perm_ref[cb]


def _ffn(x, seq_len, wa, wb, cwa, cwb, wd, g, b):
    n, d = x.shape
    tm = FFN_TM
    nc = wa.shape[0]
    cf = wa.shape[2]
    hblk = tm // HALO
    return pl.pallas_call(
        functools.partial(_ffn_body, seq_len // tm),
        grid=(n // tm,),
        in_specs=[
            pl.BlockSpec((tm, d), lambda i: (i, 0)),
            pl.BlockSpec((HALO, d), lambda i: (jnp.maximum(i * hblk - 1, 0), 0)),
            pl.BlockSpec((nc, d, cf), lambda i: (0, 0, 0)),
            pl.BlockSpec((nc, d, cf), lambda i: (0, 0, 0)),
            pl.BlockSpec((nc, 4, cf), lambda i: (0, 0, 0)),
            pl.BlockSpec((nc, 4, cf), lambda i: (0, 0, 0)),
            pl.BlockSpec((nc, cf, d), lambda i: (0, 0, 0)),
            pl.BlockSpec((1, d), lambda i: (0, 0)),
            pl.BlockSpec((1, d), lambda i: (0, 0)),
        ],
        out_specs=pl.BlockSpec((tm, d), lambda i: (i, 0)),
        out_shape=jax.ShapeDtypeStruct((n, d), F32),
        scratch_shapes=[pltpu.VMEM((d // 128, tm, 128), F32)],
        compiler_params=_cparams(("parallel",)),
        name="conv_ffn",
    )(x, x, wa, wb, cwa, cwb, wd, g, b)


def _ffn_weights(w_up, conv_w, conv_b, w_down):
    nc = D_FF // FF_CHUNK

    def up(w):
        return w.reshape(D_MODEL, nc, FF_CHUNK).transpose(1, 0, 2).astype(BF16)

    def taps(cw, cb):
        t = jnp.concatenate([cw, cb[None]], axis=0)
        return t.reshape(4, nc, FF_CHUNK).transpose(1, 0, 2)

    wa, wb = up(w_up[:, :D_FF]), up(w_up[:, D_FF:])
    cwa = taps(conv_w[:, :D_FF], conv_b[:D_FF])
    cwb = 0.5 * taps(conv_w[:, D_FF:], conv_b[D_FF:])
    wd = w_down.reshape(nc, FF_CHUNK, D_MODEL).astype(BF16)
    return wa, wb, cwa, cwb, wd


_NSA_T_ROWS = HQ + 2 * NSA_KV + N_BRANCH * N_HEADS
_NSA_N_COLS = 2 * NSA_KV + 2 * NSA_GROUPS * AUG


def _nsa_inproj_body(seq_tiles, x_ref, wt_ref, wn_ref, bg_ref, cost_ref, sint_ref, cosn_ref, sinn_ref,
                     qn_ref, qr_ref, vs_ref, vw_ref, gt_ref, ksa_ref, kwa_ref, kc_ref, vc_ref):
    i = pl.program_id(0)
    xb = x_ref[...].astype(BF16)
    t = _dot_nt(wt_ref[...], xb)
    q = t[0:HQ] * QSCALE
    qn_ref[0] = q.astype(BF16)
    q3 = q.reshape(N_HEADS, HEAD_DIM, TT)
    half = HEAD_DIM // 2
    rot = jnp.concatenate([-q3[:, half:], q3[:, :half]], axis=1)
    qr = q3 * cost_ref[0][None] + rot * sint_ref[0][None]
    qr_ref[0] = qr.reshape(HQ, TT).astype(BF16)
    ones = jnp.ones((NSA_GROUPS, VROWS - HEAD_DIM, TT), BF16)
    for v_ref, lo in ((vs_ref, HQ), (vw_ref, HQ + NSA_KV)):
        v_ref[0, :, 0:HEAD_DIM, :] = t[lo:lo + NSA_KV].reshape(NSA_GROUPS, HEAD_DIM, TT).astype(BF16)
        v_ref[0, :, HEAD_DIM:VROWS, :] = ones
    gt_ref[0] = jax.nn.sigmoid(t[HQ + 2 * NSA_KV:] + bg_ref[...])

    n = _dot(xb, wn_ref[...])
    kc_ref[...] = n[:, 0:NSA_KV]
    vc_ref[...] = n[:, NSA_KV:2 * NSA_KV]
    width = NSA_GROUPS * AUG
    lane = lax.broadcasted_iota(jnp.int32, (TT, width), 1) % AUG
    cosn, sinn = cosn_ref[...], sinn_ref[...]

    def rope_nat(k):
        rh = jnp.where(lane < half, -pltpu.roll(k, width - half, axis=1), pltpu.roll(k, half, axis=1))
        return k * cosn + rh * sinn

    ks = rope_nat(n[:, 2 * NSA_KV:2 * NSA_KV + width])
    kw = rope_nat(n[:, 2 * NSA_KV + width:])
    row = lax.broadcasted_iota(jnp.int32, (TT, width), 0)
    blk = ((i % seq_tiles) * TT + row) // SEL_LEN
    onehot = (lane - HEAD_DIM == blk).astype(F32)
    ksa_ref[...] = (ks + onehot).astype(BF16)
    kwa_ref[...] = kw.astype(BF16)


def _nsa_inproj(x, seq_len, wt, wn, bg, cost, sint, cosn, sinn):
    n = x.shape[0]
    nt = n // TT
    seq_tiles = seq_len // TT
    width = NSA_GROUPS * AUG
    full = lambda r, c: pl.BlockSpec((r, c), lambda i: (0, 0))
    tile3 = lambda r: pl.BlockSpec((1, r, TT), lambda i: (i, 0, 0))
    nat = lambda c: pl.BlockSpec((TT, c), lambda i: (i, 0))
    vtile = pl.BlockSpec((1, NSA_GROUPS, VROWS, TT), lambda i: (i, 0, 0, 0))
    return pl.pallas_call(
        functools.partial(_nsa_inproj_body, seq_tiles),
        grid=(nt,),
        in_specs=[
            nat(D_MODEL),
            full(_NSA_T_ROWS, D_MODEL),
            full(D_MODEL, _NSA_N_COLS),
            full(N_BRANCH * N_HEADS, 1),
            pl.BlockSpec((1, HEAD_DIM, TT), lambda i: (i % seq_tiles, 0, 0)),
            pl.BlockSpec((1, HEAD_DIM, TT), lambda i: (i % seq_tiles, 0, 0)),
            pl.BlockSpec((TT, width), lambda i: (i % seq_tiles, 0)),
            pl.BlockSpec((TT, width), lambda i: (i % seq_tiles, 0)),
        ],
        out_specs=[tile3(HQ), tile3(HQ), vtile, vtile, tile3(N_BRANCH * N_HEADS),
                   nat(width), nat(width), nat(NSA_KV), nat(NSA_KV)],
        out_shape=[
            jax.ShapeDtypeStruct((nt, HQ, TT), BF16),
            jax.ShapeDtypeStruct((nt, HQ, TT), BF16),
            jax.ShapeDtypeStruct((nt, NSA_GROUPS, VROWS, TT), BF16),
            jax.ShapeDtypeStruct((nt, NSA_GROUPS, VROWS, TT), BF16),
            jax.ShapeDtypeStruct((nt, N_BRANCH * N_HEADS, TT), F32),
            jax.ShapeDtypeStruct((n, width), BF16),
            jax.ShapeDtypeStruct((n, width), BF16),
            jax.ShapeDtypeStruct((n, NSA_KV), F32),
            jax.ShapeDtypeStruct((n, NSA_KV), F32),
        ],
        compiler_params=_cparams(("parallel",)),
        name="nsa_inproj",
    )(x, wt, wn, bg, cost, sint, cosn, sinn)


def _pad_heads(w, n_heads):
    w3 = w.reshape(w.shape[0], n_heads, HEAD_DIM)
    return jnp.concatenate([w3, jnp.zeros_like(w3)], axis=-1).reshape(w.shape[0], n_heads * AUG)


def _nsa_inproj_weights(w_in, b_gate):
    cuts = [HQ + i * NSA_KV for i in range(7)]
    wq, wkc, wvc, wks, wvs, wkw, wvw, wg = jnp.split(w_in, cuts, axis=1)
    wt = jnp.concatenate([wq, wvs, wvw, wg], axis=1).T.astype(BF16)
    wn = jnp.concatenate([wkc, wvc, _pad_heads(wks, NSA_GROUPS), _pad_heads(wkw, NSA_GROUPS)],
                         axis=1).astype(BF16)
    return wt, wn, b_gate.reshape(-1, 1)


def _rope_tables(seq_len):
    inv = ROPE_THETA ** (-jnp.arange(0, HEAD_DIM, 2, dtype=F32) / HEAD_DIM)
    ang = jnp.arange(seq_len, dtype=F32)[:, None] * inv[None, :]
    ang = jnp.concatenate([ang, ang], axis=-1)
    cos, sin = jnp.cos(ang), jnp.sin(ang)
    seq_tiles = seq_len // TT

    def transposed(t):
        return t.reshape(seq_tiles, TT, HEAD_DIM).transpose(0, 2, 1)

    def natural(t):
        return jnp.tile(jnp.concatenate([t, jnp.zeros_like(t)], axis=1), (1, NSA_GROUPS))

    return transposed(cos), transposed(sin), natural(cos), natural(sin)


def _compress_body(hk_ref, hv_ref, pe_ref, w1k_ref, b1k_ref, w2k_ref, w1vt_ref, b1v_ref, w2vt_ref,
                   kc_ref, vct_ref):
    nc = hk_ref.shape[1]
    pe_top, pe_bot = pe_ref[0:1], pe_ref[1:2]
    hk = hk_ref[0]
    top = _dot((hk + pe_top).astype(BF16), w1k_ref[0])
    bot = _dot((hk + pe_bot).astype(BF16), w1k_ref[1])
    hid = _gelu_tanh(top + pltpu.roll(bot, nc - 1, axis=0) + b1k_ref[...])
    kc_ref[0] = _dot(hid.astype(BF16), w2k_ref[...]).astype(BF16)

    hv = hv_ref[0]
    top_t = _dot_nt(w1vt_ref[0], (hv + pe_top).astype(BF16))
    bot_t = _dot_nt(w1vt_ref[1], (hv + pe_bot).astype(BF16))
    hid_t = _gelu_tanh(top_t + pltpu.roll(bot_t, nc - 1, axis=1) + b1v_ref[...])
    vct_ref[0] = _dot(w2vt_ref[...], hid_t.astype(BF16)).astype(BF16)


def _compress(hk, hv, pe2, w1k, b1k, w2k, w1vt, b1v, w2vt):
    bg, nc, hw = hk.shape
    cst = lambda shape: pl.BlockSpec(shape, lambda i: (0,) * len(shape))
    return pl.pallas_call(
        _compress_body,
        grid=(bg,),
        in_specs=[
            pl.BlockSpec((1, nc, hw), lambda i: (i, 0, 0)),
            pl.BlockSpec((1, nc, hw), lambda i: (i, 0, 0)),
            cst((2, hw)),
            cst((2, hw, PHI_HIDDEN)), cst((1, PHI_HIDDEN)), cst((PHI_HIDDEN, HEAD_DIM)),
            cst((2, PHI_HIDDEN, hw)), cst((PHI_HIDDEN, 1)), cst((HEAD_DIM, PHI_HIDDEN)),
        ],
        out_specs=[pl.BlockSpec((1, nc, HEAD_DIM), lambda i: (i, 0, 0)),
                   pl.BlockSpec((1, HEAD_DIM, nc), lambda i: (i, 0, 0))],
        out_shape=[jax.ShapeDtypeStruct((bg, nc, HEAD_DIM), BF16),
                   jax.ShapeDtypeStruct((bg, HEAD_DIM, nc), BF16)],
        compiler_params=_cparams(("parallel",)),
        name="nsa_compress",
    )(hk, hv, pe2, w1k, b1k, w2k, w1vt, b1v, w2vt)


def _slab_max(s):
    return jnp.max(s.reshape(-1, 8, s.shape[-1]), axis=0)


def _slab_min(s):
    return jnp.min(s.reshape(-1, 8, s.shape[-1]), axis=0)


def _slab_sum(p):
    return jnp.sum(p.reshape(-1, 8, p.shape[-1]), axis=0)


def _paired_sweep(nq, qi, q_of, k_tile, v_tile, bias_row, s_ref, c_ref):
    tpos = lax.broadcasted_iota(jnp.int32, (1, TT), 1)
    kpos = lax.broadcasted_iota(jnp.int32, (TT, 1), 0)
    causal = kpos <= tpos
    half = nq // 2
    slots = []
    for s in range(nq + 1):
        if s < half:
            lo = s <= qi
            slots.append((lo, jnp.where(lo, s, s - qi - 1), jnp.where(lo, 0, 1)))
        else:
            slots.append((False, s - qi - 1, 1))

    slot_max, bias = [], []
    for s, (lo, kv, x) in enumerate(slots):
        sc = _dot(k_tile(kv), q_of(x))
        if s == nq:
            sc = jnp.where(causal, sc, MASK_BIAS)
        s_ref[s] = sc
        b = bias_row(lo, kv)
        bias.append(b)
        slot_max.append(_slab_max(sc) if b is None else _slab_max(sc) + b)
        yield 'A'
    diag = jnp.where(causal, s_ref[qi], MASK_BIAS)
    s_ref[qi] = diag
    m_lo = _slab_max(diag)
    m_hi = slot_max[nq]
    for s in range(nq):
        if s < half:
            m_lo = jnp.maximum(m_lo, jnp.where(s < qi, slot_max[s], MASK_BIAS))
            m_hi = jnp.maximum(m_hi, jnp.where(s > qi, slot_max[s], MASK_BIAS))
        else:
            m_hi = jnp.maximum(m_hi, slot_max[s])
    m_lo = jnp.max(m_lo, axis=0, keepdims=True)
    m_hi = jnp.max(m_hi, axis=0, keepdims=True)
    yield 'M'

    acc_lo = jnp.zeros((OROWS, TT), F32)
    acc_hi = jnp.zeros((OROWS, TT), F32)
    for s, (lo, kv, x) in enumerate(slots):
        m_row = m_hi if lo is False else jnp.where(lo, m_lo, m_hi)
        if bias[s] is not None:
            m_row = m_row - bias[s]
        p = jnp.exp2(s_ref[s] - m_row)
        pv = _dot(v_tile(kv), p.astype(BF16))[0:OROWS]
        if lo is False:
            acc_hi = acc_hi + pv
        else:
            c_ref[s] = pv
        yield 'B'
    for s in range(half):
        lo, c = slots[s][0], c_ref[s]
        acc_lo = acc_lo + jnp.where(lo, c, 0.0)
        acc_hi = acc_hi + jnp.where(lo, 0.0, c)
    o_lo = acc_lo[0:HEAD_DIM] * (1.0 / acc_lo[HEAD_DIM:HEAD_DIM + 1])
    o_hi = acc_hi[0:HEAD_DIM] * (1.0 / acc_hi[HEAD_DIM:HEAD_DIM + 1])
    return o_lo, o_hi


def _interleave(sweeps):
    outs = [None] * len(sweeps)

    def step(i):
        try:
            return next(sweeps[i])
        except StopIteration as done:
            outs[i] = done.value
            return 'END'

    tok = None
    while tok != 'M':
        tok = step(0)
    for r in range(len(sweeps)):
        cur = None
        nxt = None if r + 1 < len(sweeps) else 'M'
        while cur != 'END' or nxt != 'M':
            if nxt != 'M':
                nxt = step(r + 1)
            if cur != 'END':
                cur = step(r)
    return outs


def _nsa_side(n_sb, qt, clamp, qn_ref, qr_ref, kc_ref, vct_ref, ovt_ref, kw_ref, vw_ref):
    t0 = qt * TT
    tpos = t0 + lax.broadcasted_iota(jnp.int32, (1, TT), 1)
    kpos0 = lax.broadcasted_iota(jnp.int32, (TT, 1), 0)
    nc = kc_ref.shape[1]

    kc = kc_ref[0]
    cmp_end = lax.broadcasted_iota(jnp.int32, (nc, 1), 0) * CMP_STRIDE + (CMP_LEN - 1)
    cmask = cmp_end <= tpos
    psum = jnp.zeros((nc, TT), F32)
    o_cmp = []
    for r in range(NSA_REP):
        s = _dot(kc, qn_ref[0, r * HEAD_DIM:(r + 1) * HEAD_DIM, :])
        s = jnp.where(cmask, s, -jnp.inf)
        m = jnp.max(_slab_max(s), axis=0, keepdims=True)
        m = jnp.where(m == -jnp.inf, 0.0, m)
        e = jnp.exp2(s - m)
        d = jnp.sum(_slab_sum(e), axis=0, keepdims=True)
        p = e * (1.0 / jnp.where(d > 0, d, 1.0))
        psum = psum + p
        o_cmp.append(_dot(vct_ref[0], p.astype(BF16)))
        yield
    p_hi = psum.astype(BF16)
    p_lo = (psum - p_hi.astype(F32)).astype(BF16)
    imp = _dot(ovt_ref[...], p_hi) + _dot(ovt_ref[...], p_lo)

    nrow = ovt_ref.shape[0]
    blk = lax.broadcasted_iota(jnp.int32, (nrow, 1), 0)
    cur = tpos // SEL_LEN
    forced = (blk == 0) | (blk == cur) | (blk == cur - 1)
    valid = blk * SEL_LEN <= tpos
    score = jnp.where(forced, FORCE_SCORE, jnp.where(valid, imp, -1.0))
    score = jnp.where(blk < n_sb, score, -2.0)
    blk_f = blk.astype(F32)
    sel_bias = jnp.full((nrow, TT), MASK_BIAS, F32)

    j1 = jnp.maximum(qt - 1, 0) if clamp else qt - 1
    j2 = jnp.maximum(qt - 2, 0) if clamp else qt - 2
    in_window = tpos - (j2 * TT + kpos0) < WINDOW
    masks = (kpos0 <= tpos - t0,
             jnp.broadcast_to(qt >= 1, (TT, TT)) if clamp else None,
             ((qt >= 2) & in_window) if clamp else in_window)
    k_tiles = [kw_ref[0, pl.ds(pl.multiple_of(j * TT, TT), TT), :][:, 0:HEAD_DIM] for j in (qt, j1, j2)]
    v_tiles = [vw_ref[0, j] for j in (qt, j1, j2)]
    o_win = []

    def window_head(r):
        q = qr_ref[0, r * HEAD_DIM:(r + 1) * HEAD_DIM, :]
        scores = []
        for k, mask in zip(k_tiles, masks):
            s = _dot(k, q)
            scores.append(s if mask is None else jnp.where(mask, s, MASK_BIAS))
        m = functools.reduce(jnp.maximum, [_slab_max(s) for s in scores])
        m = jnp.max(m, axis=0, keepdims=True)
        acc = sum(_dot(v, jnp.exp2(s - m).astype(BF16))[0:OROWS] for v, s in zip(v_tiles, scores))
        o_win.append(acc[0:HEAD_DIM] * (1.0 / acc[HEAD_DIM:HEAD_DIM + 1]))

    n_round = min(N_SEL, n_sb)
    per_head = max(n_round // NSA_REP, 1)
    for i in range(n_round):
        top = jnp.max(_slab_max(score), axis=0, keepdims=True)
        first = jnp.min(_slab_min(jnp.where(score == top, blk_f, float(nrow))), axis=0, keepdims=True)
        chosen = blk_f == first
        sel_bias = jnp.where(chosen, 0.0, sel_bias)
        score = jnp.where(chosen, -3.0, score)
        if i % per_head == per_head - 1 and len(o_win) < NSA_REP:
            window_head(len(o_win))
        yield
    while len(o_win) < NSA_REP:
        window_head(len(o_win))
    return o_cmp, o_win, sel_bias.astype(BF16)


def _nsa_attn_body(nq, n_sb, qnl_ref, qnh_ref, qrl_ref, qrh_ref, kc_ref, vct_ref, ovt_ref, ks_ref, vs_ref,
                   kw_ref, vw_ref, gl_ref, gh_ref, olo_ref, ohi_ref, q_ref, *scratch):
    qi = pl.program_id(2)
    s_refs, c_refs = scratch[:NSA_REP], scratch[NSA_REP:]
    side_in = ((qi, qnl_ref, qrl_ref), (nq - 1 - qi, qnh_ref, qrh_ref))
    gens = [_nsa_side(n_sb, qt, x == 0 or nq < 6, qn_ref, qr_ref, kc_ref, vct_ref, ovt_ref, kw_ref, vw_ref)
            for x, (qt, qn_ref, qr_ref) in enumerate(side_in)]
    sides = [None, None]
    while None in sides:
        for x, gen in enumerate(gens):
            if sides[x] is None:
                try:
                    next(gen)
                except StopIteration as done:
                    sides[x] = done.value
    for x, (_, _, qr_ref) in enumerate(side_in):
        for r in range(NSA_REP):
            q_ref[x, r, 0:HEAD_DIM, :] = qr_ref[0, r * HEAD_DIM:(r + 1) * HEAD_DIM, :]
            q_ref[x, r, HEAD_DIM:AUG, :] = sides[x][2]

    k_tile = lambda kv: ks_ref[0, pl.ds(pl.multiple_of(kv * TT, TT), TT), :]
    v_tile = lambda kv: vs_ref[0, kv]
    o_sel = _interleave([_paired_sweep(nq, qi, lambda x, r=r: q_ref[x, r], k_tile, v_tile,
                                       lambda lo, kv: None, s_refs[r], c_refs[r]) for r in range(NSA_REP)])

    for x, (g_ref, o_ref) in enumerate(((gl_ref, olo_ref), (gh_ref, ohi_ref))):
        o_cmp, o_win, _ = sides[x]
        g = g_ref[0, 0]
        rows = []
        for r in range(NSA_REP):
            gc = g[r * N_BRANCH + 0:r * N_BRANCH + 1]
            gs = g[r * N_BRANCH + 1:r * N_BRANCH + 2]
            gw = g[r * N_BRANCH + 2:r * N_BRANCH + 3]
            rows.append(gc * o_cmp[r] + gs * o_sel[r][x] + gw * o_win[r])
        o_ref[0] = jnp.concatenate(rows, axis=0).T.astype(BF16)


def _nsa_attn(batch, seq_len, qn, qr, kc, vct, ovt, ksa, vs, kwa, vw, gt):
    nq = seq_len // TT
    nc = kc.shape[1]
    n_sb = seq_len // SEL_LEN
    g_rows = NSA_REP * N_BRANCH
    half = nq // 2
    ksa3 = ksa.reshape(batch, seq_len, NSA_GROUPS * AUG)
    kwa3 = kwa.reshape(batch, seq_len, NSA_GROUPS * AUG)
    vs5 = vs.reshape(batch, nq, NSA_GROUPS, VROWS, TT)
    vw5 = vw.reshape(batch, nq, NSA_GROUPS, VROWS, TT)
    gt4 = gt.reshape(batch * nq, NSA_GROUPS, g_rows, TT)
    lo_tile = lambda b, q: b * nq + q
    hi_tile = lambda b, q: b * nq + nq - 1 - q
    q_lo = pl.BlockSpec((1, NSA_REP * HEAD_DIM, TT), lambda b, g, q: (lo_tile(b, q), g, 0))
    q_hi = pl.BlockSpec((1, NSA_REP * HEAD_DIM, TT), lambda b, g, q: (hi_tile(b, q), g, 0))
    kspec = pl.BlockSpec((1, seq_len, AUG), lambda b, g, q: (b, 0, g))
    vspec = pl.BlockSpec((1, nq, None, VROWS, TT), lambda b, g, q: (b, 0, g, 0, 0))
    out = jax.ShapeDtypeStruct((batch, seq_len // 2, HQ), BF16)
    return pl.pallas_call(
        functools.partial(_nsa_attn_body, nq, n_sb),
        grid=(batch, NSA_GROUPS, half),
        in_specs=[
            q_lo, q_hi, q_lo, q_hi,
            pl.BlockSpec((1, nc, HEAD_DIM), lambda b, g, q: (b * NSA_GROUPS + g, 0, 0)),
            pl.BlockSpec((1, HEAD_DIM, nc), lambda b, g, q: (b * NSA_GROUPS + g, 0, 0)),
            pl.BlockSpec(ovt.shape, lambda b, g, q: (0, 0)),
            kspec, vspec, kspec, vspec,
            pl.BlockSpec((1, 1, g_rows, TT), lambda b, g, q: (lo_tile(b, q), g, 0, 0)),
            pl.BlockSpec((1, 1, g_rows, TT), lambda b, g, q: (hi_tile(b, q), g, 0, 0)),
        ],
        out_specs=[pl.BlockSpec((1, TT, NSA_REP * HEAD_DIM), lambda b, g, q: (b, q, g)),
                   pl.BlockSpec((1, TT, NSA_REP * HEAD_DIM), lambda b, g, q: (b, half - 1 - q, g))],
        out_shape=[out, out],
        scratch_shapes=([pltpu.VMEM((2, NSA_REP, AUG, TT), BF16)]
                        + [pltpu.VMEM((nq + 1, TT, TT), F32)] * NSA_REP
                        + [pltpu.VMEM((half, OROWS, TT), F32)] * NSA_REP),
        compiler_params=_cparams(("parallel", "parallel", "arbitrary")),
        name="nsa_attn",
    )(qn, qn, qr, qr, kc, vct, ovt, ksa3, vs5, kwa3, vw5, gt4, gt4)


def _overlap_t(seq_len, nc):
    n_cmp = (seq_len - CMP_LEN) // CMP_STRIDE + 1
    n_sb = seq_len // SEL_LEN
    cmp_start = jnp.arange(n_cmp) * CMP_STRIDE
    sel_start = jnp.arange(n_sb) * SEL_LEN
    ov = jnp.clip(jnp.minimum(cmp_start[:, None] + CMP_LEN, sel_start[None, :] + SEL_LEN)
                  - jnp.maximum(cmp_start[:, None], sel_start[None, :]), 0, None).astype(F32) / CMP_LEN
    out = jnp.zeros((SEL_LEN, nc), F32).at[:n_sb, :n_cmp].set(ov.T)
    return out.astype(BF16)


def _nsa_attention(x, batch, seq_len, w_in, b_gate, pe, k_w1, k_b1, k_w2, v_w1, v_b1, v_w2, tables):
    wt, wn, bg = _nsa_inproj_weights(w_in, b_gate)
    qn, qr, vs, vw, gt, ksa, kwa, kc, vc = _nsa_inproj(x, seq_len, wt, wn, bg, *tables)

    nc = seq_len // CMP_STRIDE
    hw = CMP_STRIDE * HEAD_DIM

    def half_blocks(t):
        t = t.reshape(batch, seq_len, NSA_GROUPS, HEAD_DIM).transpose(0, 2, 1, 3)
        return t.reshape(batch * NSA_GROUPS, nc, hw)

    kcmp, vcmp_t = _compress(
        half_blocks(kc), half_blocks(vc), pe.reshape(2, hw),
        k_w1.reshape(2, hw, PHI_HIDDEN).astype(BF16), k_b1.reshape(1, -1), k_w2.astype(BF16),
        v_w1.reshape(2, hw, PHI_HIDDEN).transpose(0, 2, 1).astype(BF16), v_b1.reshape(-1, 1),
        v_w2.T.astype(BF16))
    return _nsa_attn(batch, seq_len, qn, qr, kcmp, vcmp_t, _overlap_t(seq_len, nc), ksa, vs, kwa, vw, gt)


def _nsa_layer(x, batch, seq_len, w_in, b_gate, pe, k_w1, k_b1, k_w2, v_w1, v_b1, v_w2, w_o,
               ln_g, ln_b, tables):
    a_lo, a_hi = _nsa_attention(x, batch, seq_len, w_in, b_gate, pe, k_w1, k_b1, k_w2, v_w1, v_b1, v_w2,
                                tables)
    return _proj_ln(a_lo, a_hi, w_o.astype(BF16), x, ln_g.reshape(1, -1), ln_b.reshape(1, -1))


_FOX_T_ROWS = 2 * HQ + N_HEADS
_FOX_N_COLS = N_HEADS * AUG + 128
_N_PIECE = 3


def _fox_inproj_body(seq_tiles, x_ref, wt_ref, wn_ref, bft_ref, bfn_ref, place_ref, ones_ref, route_ref,
                     qa_ref, ka_ref, vt_ref, off_ref, run_ref):
    i = pl.program_id(0)
    xb = x_ref[...].astype(BF16)
    t = _dot_nt(wt_ref[...], xb)
    vt_ref[0, :, 0:HEAD_DIM, :] = t[HQ:2 * HQ].reshape(N_HEADS, HEAD_DIM, TT).astype(BF16)
    vt_ref[0, :, HEAD_DIM:VROWS, :] = jnp.ones((N_HEADS, VROWS - HEAD_DIM, TT), BF16)
    lf_t = _log_sigmoid(t[2 * HQ:] + bft_ref[...]) * LOG2E
    n = _dot(xb, wn_ref[...])
    kw = N_HEADS * AUG
    lf_n = _log_sigmoid(n[:, kw:kw + _N_PIECE * N_HEADS] + bfn_ref[...]) * LOG2E

    r_i = lax.broadcasted_iota(jnp.int32, (TT, TT), 0)
    c_i = lax.broadcasted_iota(jnp.int32, (TT, TT), 1)
    upper = ((r_i > 0) & (r_i <= c_i)).astype(BF16)
    lower = ((c_i > 0) & (c_i <= r_i)).astype(BF16)
    a_t = sum(_dot(p, upper) for p in _split3(lf_t))
    a_n = sum(_dot(lower, p) for p in _split3(lf_n))

    q3 = (t[0:HQ] * QSCALE).reshape(N_HEADS, HEAD_DIM, TT)
    qa_ref[0, :, 0:HEAD_DIM, :] = q3.astype(BF16)
    stacked = jnp.concatenate(list(_split3(a_t)) + [jnp.ones((N_HEADS, TT), BF16)], axis=0)
    for h in range(N_HEADS):
        qa_ref[0, h, HEAD_DIM:AUG, :] = _dot(route_ref[h], stacked).astype(BF16)

    b1, b2, b3 = _split3(-a_n)
    grp = lax.broadcasted_iota(jnp.int32, (TT, _N_PIECE * N_HEADS), 1) // N_HEADS
    bsel = jnp.where(grp == 0, b1, jnp.where(grp == 1, b2, b3))
    ka_ref[...] = (n[:, 0:kw] + _dot(bsel, place_ref[...]) + ones_ref[...]).astype(BF16)

    @pl.when(i % seq_tiles == 0)
    def _():
        run_ref[...] = jnp.zeros_like(run_ref)

    first = lf_t[:, 0:1]
    off_ref[0] = jnp.broadcast_to(run_ref[:, 0:1] + first, (N_HEADS, TT))
    run_ref[...] = run_ref[...] + (a_t[:, TT - 1:TT] + first)


def _fox_inproj(x, seq_len, wt, wn, bft, bfn, place, ones, route):
    n = x.shape[0]
    nt = n // TT
    kw = N_HEADS * AUG
    full = lambda r, c: pl.BlockSpec((r, c), lambda i: (0, 0))
    return pl.pallas_call(
        functools.partial(_fox_inproj_body, seq_len // TT),
        grid=(nt,),
        in_specs=[
            pl.BlockSpec((TT, D_MODEL), lambda i: (i, 0)),
            full(_FOX_T_ROWS, D_MODEL), full(D_MODEL, _FOX_N_COLS),
            full(N_HEADS, 1), full(1, _N_PIECE * N_HEADS),
            full(_N_PIECE * N_HEADS, kw), full(1, kw),
            pl.BlockSpec((N_HEADS, AUG - HEAD_DIM, AUG - HEAD_DIM), lambda i: (0, 0, 0)),
        ],
        out_specs=[
            pl.BlockSpec((1, N_HEADS, AUG, TT), lambda i: (i, 0, 0, 0)),
            pl.BlockSpec((TT, kw), lambda i: (i, 0)),
            pl.BlockSpec((1, N_HEADS, VROWS, TT), lambda i: (i, 0, 0, 0)),
            pl.BlockSpec((1, N_HEADS, TT), lambda i: (i, 0, 0)),
        ],
        out_shape=[
            jax.ShapeDtypeStruct((nt, N_HEADS, AUG, TT), BF16),
            jax.ShapeDtypeStruct((n, kw), BF16),
            jax.ShapeDtypeStruct((nt, N_HEADS, VROWS, TT), BF16),
            jax.ShapeDtypeStruct((nt, N_HEADS, TT), F32),
        ],
        scratch_shapes=[pltpu.VMEM((N_HEADS, 128), F32)],
        compiler_params=_cparams(("arbitrary",)),
        name="fox_inproj",
    )(x, wt, wn, bft, bfn, place, ones, route)


def _fox_inproj_weights(w_in, b_f):
    wq, wk, wv, wf = jnp.split(w_in, [HQ, 2 * HQ, 3 * HQ], axis=1)
    wt = jnp.concatenate([wq, wv, wf], axis=1).T.astype(BF16)
    pad = jnp.zeros((D_MODEL, _FOX_N_COLS - N_HEADS * AUG - _N_PIECE * N_HEADS), F32)
    wn = jnp.concatenate([_pad_heads(wk, N_HEADS)] + [wf] * _N_PIECE + [pad], axis=1).astype(BF16)
    bft = b_f.reshape(-1, 1)
    bfn = jnp.tile(b_f, _N_PIECE).reshape(1, -1)
    rows = jnp.arange(_N_PIECE * N_HEADS)
    cols = (rows % N_HEADS) * AUG + HEAD_DIM + _N_PIECE + rows // N_HEADS
    place = jnp.zeros((_N_PIECE * N_HEADS, N_HEADS * AUG), F32).at[rows, cols].set(1.0).astype(BF16)
    lane = jnp.arange(N_HEADS * AUG) % AUG
    ones = ((lane >= HEAD_DIM) & (lane < HEAD_DIM + _N_PIECE)).astype(F32).reshape(1, -1)
    hh = jnp.arange(N_HEADS)
    route = jnp.zeros((N_HEADS, AUG - HEAD_DIM, AUG - HEAD_DIM), F32)
    for k in range(_N_PIECE):
        route = route.at[hh, k, k * N_HEADS + hh].set(1.0)
        route = route.at[hh, _N_PIECE + k, _N_PIECE * N_HEADS].set(1.0)
    return wt, wn, bft, bfn, place, ones, route.astype(BF16)


def _fox_attn_body(nq, qlo_ref, qhi_ref, ka_ref, vt_ref, off_ref, olo_ref, ohi_ref, q_ref, *scratch):
    hg = pl.program_id(1)
    qi = pl.program_id(2)
    s_refs, c_refs = scratch[:HEAD_BLOCK], scratch[HEAD_BLOCK:]
    q_ref[0] = qlo_ref[0]
    q_ref[1] = qhi_ref[0]
    sweeps = []
    for r in range(HEAD_BLOCK):
        h = hg * HEAD_BLOCK + r

        def off_row(t, h=h):
            return off_ref[0, pl.ds(t, 1), pl.ds(h, 1), :].reshape(1, TT)

        off_lo, off_hi = off_row(qi), off_row(nq - 1 - qi)

        def bias_row(lo, kv, off_lo=off_lo, off_hi=off_hi, off_row=off_row):
            base = off_hi if lo is False else jnp.where(lo, off_lo, off_hi)
            return base - off_row(kv)

        def k_tile(kv, r=r):
            return ka_ref[0, pl.ds(pl.multiple_of(kv * TT, TT), TT), r * AUG:(r + 1) * AUG]

        def v_tile(kv, r=r):
            return vt_ref[0, kv, r]

        sweeps.append(_paired_sweep(nq, qi, lambda x, r=r: q_ref[x, r], k_tile, v_tile, bias_row,
                                    s_refs[r], c_refs[r]))
    for r, (o_lo, o_hi) in enumerate(_interleave(sweeps)):
        olo_ref[0, r * HEAD_DIM:(r + 1) * HEAD_DIM, :] = o_lo.astype(BF16)
        ohi_ref[0, r * HEAD_DIM:(r + 1) * HEAD_DIM, :] = o_hi.astype(BF16)


def _fox_attn(batch, seq_len, qa, ka, vt, off):
    nq = seq_len // TT
    half = nq // 2
    hb = HEAD_BLOCK
    ka3 = ka.reshape(batch, seq_len, N_HEADS * AUG)
    vt5 = vt.reshape(batch, nq, N_HEADS, VROWS, TT)
    off4 = off.reshape(batch, nq, N_HEADS, TT)
    out = jax.ShapeDtypeStruct((batch * half, HQ, TT), BF16)
    return pl.pallas_call(
        functools.partial(_fox_attn_body, nq),
        grid=(batch, N_HEADS // hb, half),
        in_specs=[
            pl.BlockSpec((1, hb, AUG, TT), lambda b, h, q: (b * nq + q, h, 0, 0)),
            pl.BlockSpec((1, hb, AUG, TT), lambda b, h, q: (b * nq + nq - 1 - q, h, 0, 0)),
            pl.BlockSpec((1, seq_len, hb * AUG), lambda b, h, q: (b, 0, h)),
            pl.BlockSpec((1, nq, hb, VROWS, TT), lambda b, h, q: (b, 0, h, 0, 0)),
            pl.BlockSpec((1, nq, N_HEADS, TT), lambda b, h, q: (b, 0, 0, 0)),
        ],
        out_specs=[pl.BlockSpec((1, hb * HEAD_DIM, TT), lambda b, h, q: (b * half + q, h, 0)),
                   pl.BlockSpec((1, hb * HEAD_DIM, TT), lambda b, h, q: (b * half + half - 1 - q, h, 0))],
        out_shape=[out, out],
        scratch_shapes=([pltpu.VMEM((2, hb, AUG, TT), BF16)]
                        + [pltpu.VMEM((nq + 1, TT, TT), F32)] * hb
                        + [pltpu.VMEM((half, OROWS, TT), F32)] * hb),
        compiler_params=_cparams(("parallel", "parallel", "arbitrary")),
        name="fox_attn",
    )(qa, qa, ka3, vt5, off4)


def _proj_ln_t_body(half, lo_ref, hi_ref, w_ref, x_ref, g_ref, b_ref, o_ref):
    in_lo = (pl.program_id(0) % (2 * half)) < half
    at = jnp.where(in_lo, lo_ref[0], hi_ref[0])
    y = _dot(at.T, w_ref[...])
    o_ref[...] = _layer_norm(ALPHA * x_ref[...] + y, g_ref[...], b_ref[...])


def _proj_ln_t(at_lo, at_hi, seq_len, w, x, g, b):
    k = at_lo.shape[1]
    n, d = x.shape
    nq = seq_len // TT
    half = nq // 2
    return pl.pallas_call(
        functools.partial(_proj_ln_t_body, half),
        grid=(n // TT,),
        in_specs=[
            pl.BlockSpec((1, k, TT), lambda i: ((i // nq) * half + jnp.minimum(i % nq, half - 1), 0, 0)),
            pl.BlockSpec((1, k, TT), lambda i: ((i // nq) * half + jnp.maximum(i % nq - half, 0), 0, 0)),
            pl.BlockSpec((k, d), lambda i: (0, 0)),
            pl.BlockSpec((TT, d), lambda i: (i, 0)),
            pl.BlockSpec((1, d), lambda i: (0, 0)),
            pl.BlockSpec((1, d), lambda i: (0, 0)),
        ],
        out_specs=pl.BlockSpec((TT, d), lambda i: (i, 0)),
        out_shape=jax.ShapeDtypeStruct((n, d), F32),
        compiler_params=_cparams(("parallel",)),
        name="proj_ln_t",
    )(at_lo, at_hi, w, x, g, b)


def _fox_layer(x, batch, seq_len, w_in, b_f, w_o, ln_g, ln_b):
    qa, ka, vt, off = _fox_inproj(x, seq_len, *_fox_inproj_weights(w_in, b_f))
    at_lo, at_hi = _fox_attn(batch, seq_len, qa, ka, vt, off)
    return _proj_ln_t(at_lo, at_hi, seq_len, w_o.astype(BF16), x, ln_g.reshape(1, -1), ln_b.reshape(1, -1))


def kernel(x, nsa_w_in, nsa_b_gate, nsa_pe, nsa_phik_w1, nsa_phik_b1, nsa_phik_w2, nsa_phiv_w1,
           nsa_phiv_b1, nsa_phiv_w2, nsa_w_o, fox_w_in, fox_b_f, fox_w_o, ffn_w_up, ffn_conv_w,
           ffn_conv_b, ffn_w_down, ln1_g, ln1_b, ln2_g, ln2_b):
    batch, seq_len, d = x.shape
    assert d == D_MODEL and seq_len % FFN_TM == 0 and seq_len % TT == 0
    assert seq_len // SEL_LEN <= AUG - HEAD_DIM
    tables = _rope_tables(seq_len)
    h = x.reshape(batch * seq_len, d)
    for i in range(DEPTH):
        j = i // 2
        if i % 2 == 0:
            h = _nsa_layer(h, batch, seq_len, nsa_w_in[j], nsa_b_gate[j], nsa_pe[j], nsa_phik_w1[j],
                           nsa_phik_b1[j], nsa_phik_w2[j], nsa_phiv_w1[j], nsa_phiv_b1[j],
                           nsa_phiv_w2[j], nsa_w_o[j], ln1_g[i], ln1_b[i], tables)
        else:
            h = _fox_layer(h, batch, seq_len, fox_w_in[j], fox_b_f[j], fox_w_o[j], ln1_g[i], ln1_b[i])
        h = _ffn(h, seq_len, *_ffn_weights(ffn_w_up[i], ffn_conv_w[i], ffn_conv_b[i], ffn_w_down[i]),
                 ln2_g[i].reshape(1, -1), ln2_b[i].reshape(1, -1))
    return h.reshape(batch, seq_len, d)
```

```python
import functools
import math

import jax
import jax.numpy as jnp
from jax import lax
from jax.experimental import pallas as pl
from jax.experimental.pallas import tpu as pltpu

F32 = jnp.float32
BF16 = jnp.bfloat16

D_MODEL = 1024
DEPTH = 4
HEAD_DIM = 64
N_HEADS = 16
HQ = N_HEADS * HEAD_DIM
NSA_GROUPS = 4
NSA_REP = 4
NSA_KV = NSA_GROUPS * HEAD_DIM
CMP_LEN = 32
CMP_STRIDE = 16
SEL_LEN = 64
N_SEL = 16
WINDOW = 512
PHI_HIDDEN = 256
N_BRANCH = 3
FORCE_SCORE = 1e4
D_FF = 2816
ROPE_THETA = 10000.0
ALPHA = (2 * DEPTH) ** 0.25
LN_EPS = 1e-5
SCALE = HEAD_DIM ** -0.5
LOG2E = math.log2(math.e)
QSCALE = SCALE * LOG2E
VROWS = 80
OROWS = 72

TT = 256
AUG = 128
MASK_BIAS = -1e30
FF_CHUNK = 256
FFN_TM = 256
PROJ_TM = 512
HEAD_BLOCK = 4
HALO = 16
VMEM_LIMIT = 56 * 1024 * 1024


def _cparams(sem):
    return pltpu.CompilerParams(dimension_semantics=sem, vmem_limit_bytes=VMEM_LIMIT)


def _layer_norm(z, g, b):
    mu = jnp.mean(z, axis=-1, keepdims=True)
    zc = z - mu
    var = jnp.mean(zc * zc, axis=-1, keepdims=True)
    return zc * lax.rsqrt(var + LN_EPS) * g + b


def _gelu_tanh(x):
    c = math.sqrt(2.0 / math.pi)
    return x * (0.5 * (1.0 + jnp.tanh(c * (x + 0.044715 * (x * x * x)))))


def _log_sigmoid(z):
    return -(jnp.maximum(-z, 0.0) + jnp.log1p(jnp.exp(-jnp.abs(z))))


def _dot(a, b):
    return jnp.dot(a, b, preferred_element_type=F32)


def _dot_nt(a, b):
    return lax.dot_general(a, b, (((1,), (1,)), ((), ())), preferred_element_type=F32)


def _split3(x):
    p1 = x.astype(BF16)
    r1 = x - p1.astype(F32)
    p2 = r1.astype(BF16)
    p3 = (r1 - p2.astype(F32)).astype(BF16)
    return p1, p2, p3


def _proj_ln_body(half, lo_ref, hi_ref, w_ref, x_ref, g_ref, b_ref, o_ref):
    in_lo = (pl.program_id(0) % (2 * half)) < half
    y = _dot(jnp.where(in_lo, lo_ref[0], hi_ref[0]), w_ref[...])
    o_ref[...] = _layer_norm(ALPHA * x_ref[...] + y, g_ref[...], b_ref[...])


def _proj_ln(a_lo, a_hi, w, x, g, b):
    _, s_half, k = a_lo.shape
    n, d = x.shape
    tm = PROJ_TM
    half = s_half // tm
    return pl.pallas_call(
        functools.partial(_proj_ln_body, half),
        grid=(n // tm,),
        in_specs=[
            pl.BlockSpec((1, tm, k), lambda i: (i // (2 * half), jnp.minimum(i % (2 * half), half - 1), 0)),
            pl.BlockSpec((1, tm, k), lambda i: (i // (2 * half), jnp.maximum(i % (2 * half) - half, 0), 0)),
            pl.BlockSpec((k, d), lambda i: (0, 0)),
            pl.BlockSpec((tm, d), lambda i: (i, 0)),
            pl.BlockSpec((1, d), lambda i: (0, 0)),
            pl.BlockSpec((1, d), lambda i: (0, 0)),
        ],
        out_specs=pl.BlockSpec((tm, d), lambda i: (i, 0)),
        out_shape=jax.ShapeDtypeStruct((n, d), F32),
        compiler_params=_cparams(("parallel",)),
        name="proj_ln",
    )(a_lo, a_hi, w, x, g, b)


GELU_C0 = math.sqrt(2.0 / math.pi)
GELU_C1 = GELU_C0 * 0.044715


def _ffn_body(seq_tiles, x_ref, xp_ref, wa_ref, wb_ref, cwa_ref, cwb_ref, wd_ref, g_ref, b_ref, o_ref,
              perm_ref):
    i = pl.program_id(0)
    tm, d = x_ref.shape
    nj = tm // 8
    n_lane = d // 128
    for cb in range(n_lane):
        perm_ref[cb] = x_ref[:, cb * 128:(cb + 1) * 128]
    x = jnp.concatenate(
        [jnp.concatenate([perm_ref[cb, pl.ds(j, 8, stride=nj), :] for cb in range(n_lane)], axis=1)
         for j in range(nj)], axis=0)
    starts_seq = (i % seq_tiles) == 0
    halo = jnp.where(starts_seq, 0.0, xp_ref[...]).astype(BF16)
    xcat = jnp.concatenate([halo, x.astype(BF16)], axis=0)
    n_chunks = wa_ref.shape[0]
    first_sublane = lax.broadcasted_iota(jnp.int32, (8, 1), 0) == 0

    def up(c):
        return _dot(xcat, wa_ref[c]), _dot(xcat, wb_ref[c])

    def conv(h, cw):
        h3 = h[HALO:].reshape(nj, 8, h.shape[-1])

        def wrap(slab, halo_row):
            return jnp.where(first_sublane, halo_row, pltpu.roll(slab, 1, axis=0))[None]

        prev1 = wrap(h3[nj - 1], h[HALO - 1:HALO])
        prev2 = wrap(h3[nj - 2], h[HALO - 2:HALO - 1])
        s1 = jnp.concatenate([prev1, h3[:-1]], axis=0)
        s2 = jnp.concatenate([prev2, prev1, h3[:-2]], axis=0)
        return (cw[0:1] * s2 + cw[1:2] * s1 + cw[2:3] * h3 + cw[3:4]).reshape(tm, h.shape[-1])

    y = None
    nxt = up(0)
    for c in range(n_chunks):
        cur, nxt = nxt, (up(c + 1) if c + 1 < n_chunks else None)
        ha = conv(cur[0], cwa_ref[c])
        hb = conv(cur[1], cwb_ref[c])
        u = ha * (GELU_C0 + GELU_C1 * (ha * ha))
        gated = ((ha + ha * jnp.tanh(u)) * hb).astype(BF16)
        part = _dot(gated, wd_ref[c])
        y = part if y is None else y + part
    out = _layer_norm(ALPHA * x + y, g_ref[...], b_ref[...])
    for j in range(nj):
        for cb in range(n_lane):
            perm_ref[cb, pl.ds(j, 8, stride=nj), :] = out[j * 8:(j + 1) * 8, cb * 128:(cb + 1) * 128]
    for cb in range(n_lane):
        o_ref[:, cb * 128:(cb + 1) * 128] = perm_ref[cb]


def _ffn(x, seq_len, wa, wb, cwa, cwb, wd, g, b):
    n, d = x.shape
    tm = FFN_TM
    nc = wa.shape[0]
    cf = wa.shape[2]
    hblk = tm // HALO
    return pl.pallas_call(
        functools.partial(_ffn_body, seq_len // tm),
        grid=(n // tm,),
        in_specs=[
            pl.BlockSpec((tm, d), lambda i: (i, 0)),
            pl.BlockSpec((HALO, d), lambda i: (jnp.maximum(i * hblk - 1, 0), 0)),
            pl.BlockSpec((nc, d, cf), lambda i: (0, 0, 0)),
            pl.BlockSpec((nc, d, cf), lambda i: (0, 0, 0)),
            pl.BlockSpec((nc, 4, cf), lambda i: (0, 0, 0)),
            pl.BlockSpec((nc, 4, cf), lambda i: (0, 0, 0)),
            pl.BlockSpec((nc, cf, d), lambda i: (0, 0, 0)),
            pl.BlockSpec((1, d), lambda i: (0, 0)),
            pl.BlockSpec((1, d), lambda i: (0, 0)),
        ],
        out_specs=pl.BlockSpec((tm, d), lambda i: (i, 0)),
        out_shape=jax.ShapeDtypeStruct((n, d), F32),
        scratch_shapes=[pltpu.VMEM((d // 128, tm, 128), F32)],
        compiler_params=_cparams(("parallel",)),
        name="conv_ffn",
    )(x, x, wa, wb, cwa, cwb, wd, g, b)


def _ffn_weights(w_up, conv_w, conv_b, w_down):
    nc = D_FF // FF_CHUNK

    def up(w):
        return w.reshape(D_MODEL, nc, FF_CHUNK).transpose(1, 0, 2).astype(BF16)

    def taps(cw, cb):
        t = jnp.concatenate([cw, cb[None]], axis=0)
        return t.reshape(4, nc, FF_CHUNK).transpose(1, 0, 2)

    wa, wb = up(w_up[:, :D_FF]), up(w_up[:, D_FF:])
    cwa = taps(conv_w[:, :D_FF], conv_b[:D_FF])
    cwb = 0.5 * taps(conv_w[:, D_FF:], conv_b[D_FF:])
    wd = w_down.reshape(nc, FF_CHUNK, D_MODEL).astype(BF16)
    return wa, wb, cwa, cwb, wd


_NSA_T_ROWS = HQ + 2 * NSA_KV + N_BRANCH * N_HEADS
_NSA_N_COLS = 2 * NSA_KV + 2 * NSA_GROUPS * AUG


def _nsa_inproj_body(seq_tiles, x_ref, wt_ref, wn_ref, bg_ref, cost_ref, sint_ref, cosn_ref, sinn_ref,
                     qn_ref, qr_ref, vs_ref, vw_ref, gt_ref, ksa_ref, kwa_ref, kc_ref, vc_ref):
    i = pl.program_id(0)
    xb = x_ref[...].astype(BF16)
    t = _dot_nt(wt_ref[...], xb)
    q = t[0:HQ] * QSCALE
    qn_ref[0] = q.astype(BF16)
    q3 = q.reshape(N_HEADS, HEAD_DIM, TT)
    half = HEAD_DIM // 2
    rot = jnp.concatenate([-q3[:, half:], q3[:, :half]], axis=1)
    qr = q3 * cost_ref[0][None] + rot * sint_ref[0][None]
    qr_ref[0] = qr.reshape(HQ, TT).astype(BF16)
    ones = jnp.ones((NSA_GROUPS, VROWS - HEAD_DIM, TT), BF16)
    for v_ref, lo in ((vs_ref, HQ), (vw_ref, HQ + NSA_KV)):
        v_ref[0, :, 0:HEAD_DIM, :] = t[lo:lo + NSA_KV].reshape(NSA_GROUPS, HEAD_DIM, TT).astype(BF16)
        v_ref[0, :, HEAD_DIM:VROWS, :] = ones
    gt_ref[0] = jax.nn.sigmoid(t[HQ + 2 * NSA_KV:] + bg_ref[...])

    n = _dot(xb, wn_ref[...])
    kc_ref[...] = n[:, 0:NSA_KV]
    vc_ref[...] = n[:, NSA_KV:2 * NSA_KV]
    width = NSA_GROUPS * AUG
    lane = lax.broadcasted_iota(jnp.int32, (TT, width), 1) % AUG
    cosn, sinn = cosn_ref[...], sinn_ref[...]

    def rope_nat(k):
        rh = jnp.where(lane < half, -pltpu.roll(k, width - half, axis=1), pltpu.roll(k, half, axis=1))
        return k * cosn + rh * sinn

    ks = rope_nat(n[:, 2 * NSA_KV:2 * NSA_KV + width])
    kw = rope_nat(n[:, 2 * NSA_KV + width:])
    row = lax.broadcasted_iota(jnp.int32, (TT, width), 0)
    blk = ((i % seq_tiles) * TT + row) // SEL_LEN
    onehot = (lane - HEAD_DIM == blk).astype(F32)
    ksa_ref[...] = (ks + onehot).astype(BF16)
    kwa_ref[...] = kw.astype(BF16)


def _nsa_inproj(x, seq_len, wt, wn, bg, cost, sint, cosn, sinn):
    n = x.shape[0]
    nt = n // TT
    seq_tiles = seq_len // TT
    width = NSA_GROUPS * AUG
    full = lambda r, c: pl.BlockSpec((r, c), lambda i: (0, 0))
    tile3 = lambda r: pl.BlockSpec((1, r, TT), lambda i: (i, 0, 0))
    nat = lambda c: pl.BlockSpec((TT, c), lambda i: (i, 0))
    vtile = pl.BlockSpec((1, NSA_GROUPS, VROWS, TT), lambda i: (i, 0, 0, 0))
    return pl.pallas_call(
        functools.partial(_nsa_inproj_body, seq_tiles),
        grid=(nt,),
        in_specs=[
            nat(D_MODEL),
            full(_NSA_T_ROWS, D_MODEL),
            full(D_MODEL, _NSA_N_COLS),
            full(N_BRANCH * N_HEADS, 1),
            pl.BlockSpec((1, HEAD_DIM, TT), lambda i: (i % seq_tiles, 0, 0)),
            pl.BlockSpec((1, HEAD_DIM, TT), lambda i: (i % seq_tiles, 0, 0)),
            pl.BlockSpec((TT, width), lambda i: (i % seq_tiles, 0)),
            pl.BlockSpec((TT, width), lambda i: (i % seq_tiles, 0)),
        ],
        out_specs=[tile3(HQ), tile3(HQ), vtile, vtile, tile3(N_BRANCH * N_HEADS),
                   nat(width), nat(width), nat(NSA_KV), nat(NSA_KV)],
        out_shape=[
            jax.ShapeDtypeStruct((nt, HQ, TT), BF16),
            jax.ShapeDtypeStruct((nt, HQ, TT), BF16),
            jax.ShapeDtypeStruct((nt, NSA_GROUPS, VROWS, TT), BF16),
            jax.ShapeDtypeStruct((nt, NSA_GROUPS, VROWS, TT), BF16),
            jax.ShapeDtypeStruct((nt, N_BRANCH * N_HEADS, TT), F32),
            jax.ShapeDtypeStruct((n, width), BF16),
            jax.ShapeDtypeStruct((n, width), BF16),
            jax.ShapeDtypeStruct((n, NSA_KV), F32),
            jax.ShapeDtypeStruct((n, NSA_KV), F32),
        ],
        compiler_params=_cparams(("parallel",)),
        name="nsa_inproj",
    )(x, wt, wn, bg, cost, sint, cosn, sinn)


def _pad_heads(w, n_heads):
    w3 = w.reshape(w.shape[0], n_heads, HEAD_DIM)
    return jnp.concatenate([w3, jnp.zeros_like(w3)], axis=-1).reshape(w.shape[0], n_heads * AUG)


def _nsa_inproj_weights(w_in, b_gate):
    cuts = [HQ + i * NSA_KV for i in range(7)]
    wq, wkc, wvc, wks, wvs, wkw, wvw, wg = jnp.split(w_in, cuts, axis=1)
    wt = jnp.concatenate([wq, wvs, wvw, wg], axis=1).T.astype(BF16)
    wn = jnp.concatenate([wkc, wvc, _pad_heads(wks, NSA_GROUPS), _pad_heads(wkw, NSA_GROUPS)],
                         axis=1).astype(BF16)
    return wt, wn, b_gate.reshape(-1, 1)


def _rope_tables(seq_len):
    inv = ROPE_THETA ** (-jnp.arange(0, HEAD_DIM, 2, dtype=F32) / HEAD_DIM)
    ang = jnp.arange(seq_len, dtype=F32)[:, None] * inv[None, :]
    ang = jnp.concatenate([ang, ang], axis=-1)
    cos, sin = jnp.cos(ang), jnp.sin(ang)
    seq_tiles = seq_len // TT

    def transposed(t):
        return t.reshape(seq_tiles, TT, HEAD_DIM).transpose(0, 2, 1)

    def natural(t):
        return jnp.tile(jnp.concatenate([t, jnp.zeros_like(t)], axis=1), (1, NSA_GROUPS))

    return transposed(cos), transposed(sin), natural(cos), natural(sin)


def _compress_body(hk_ref, hv_ref, pe_ref, w1k_ref, b1k_ref, w2k_ref, w1vt_ref, b1v_ref, w2vt_ref,
                   kc_ref, vct_ref):
    nc = hk_ref.shape[1]
    pe_top, pe_bot = pe_ref[0:1], pe_ref[1:2]
    hk = hk_ref[0]
    top = _dot((hk + pe_top).astype(BF16), w1k_ref[0])
    bot = _dot((hk + pe_bot).astype(BF16), w1k_ref[1])
    hid = _gelu_tanh(top + pltpu.roll(bot, nc - 1, axis=0) + b1k_ref[...])
    kc_ref[0] = _dot(hid.astype(BF16), w2k_ref[...]).astype(BF16)

    hv = hv_ref[0]
    top_t = _dot_nt(w1vt_ref[0], (hv + pe_top).astype(BF16))
    bot_t = _dot_nt(w1vt_ref[1], (hv + pe_bot).astype(BF16))
    hid_t = _gelu_tanh(top_t + pltpu.roll(bot_t, nc - 1, axis=1) + b1v_ref[...])
    vct_ref[0] = _dot(w2vt_ref[...], hid_t.astype(BF16)).astype(BF16)


def _compress(hk, hv, pe2, w1k, b1k, w2k, w1vt, b1v, w2vt):
    bg, nc, hw = hk.shape
    cst = lambda shape: pl.BlockSpec(shape, lambda i: (0,) * len(shape))
    return pl.pallas_call(
        _compress_body,
        grid=(bg,),
        in_specs=[
            pl.BlockSpec((1, nc, hw), lambda i: (i, 0, 0)),
            pl.BlockSpec((1, nc, hw), lambda i: (i, 0, 0)),
            cst((2, hw)),
            cst((2, hw, PHI_HIDDEN)), cst((1, PHI_HIDDEN)), cst((PHI_HIDDEN, HEAD_DIM)),
            cst((2, PHI_HIDDEN, hw)), cst((PHI_HIDDEN, 1)), cst((HEAD_DIM, PHI_HIDDEN)),
        ],
        out_specs=[pl.BlockSpec((1, nc, HEAD_DIM), lambda i: (i, 0, 0)),
                   pl.BlockSpec((1, HEAD_DIM, nc), lambda i: (i, 0, 0))],
        out_shape=[jax.ShapeDtypeStruct((bg, nc, HEAD_DIM), BF16),
                   jax.ShapeDtypeStruct((bg, HEAD_DIM, nc), BF16)],
        compiler_params=_cparams(("parallel",)),
        name="nsa_compress",
    )(hk, hv, pe2, w1k, b1k, w2k, w1vt, b1v, w2vt)


def _slab_max(s):
    return jnp.max(s.reshape(-1, 8, s.shape[-1]), axis=0)


def _slab_min(s):
    return jnp.min(s.reshape(-1, 8, s.shape[-1]), axis=0)


def _slab_sum(p):
    return jnp.sum(p.reshape(-1, 8, p.shape[-1]), axis=0)


def _paired_sweep(nq, qi, q_of, k_tile, v_tile, bias_row, s_ref, c_ref):
    tpos = lax.broadcasted_iota(jnp.int32, (1, TT), 1)
    kpos = lax.broadcasted_iota(jnp.int32, (TT, 1), 0)
    causal = kpos <= tpos
    half = nq // 2
    slots = []
    for s in range(nq + 1):
        if s < half:
            lo = s <= qi
            slots.append((lo, jnp.where(lo, s, s - qi - 1), jnp.where(lo, 0, 1)))
        else:
            slots.append((False, s - qi - 1, 1))

    slot_max, bias = [], []
    for s, (lo, kv, x) in enumerate(slots):
        sc = _dot(k_tile(kv), q_of(x))
        if s == nq:
            sc = jnp.where(causal, sc, MASK_BIAS)
        s_ref[s] = sc
        b = bias_row(lo, kv)
        bias.append(b)
        slot_max.append(_slab_max(sc) if b is None else _slab_max(sc) + b)
        yield 'A'
    diag = jnp.where(causal, s_ref[qi], MASK_BIAS)
    s_ref[qi] = diag
    m_lo = _slab_max(diag)
    m_hi = slot_max[nq]
    for s in range(nq):
        if s < half:
            m_lo = jnp.maximum(m_lo, jnp.where(s < qi, slot_max[s], MASK_BIAS))
            m_hi = jnp.maximum(m_hi, jnp.where(s > qi, slot_max[s], MASK_BIAS))
        else:
            m_hi = jnp.maximum(m_hi, slot_max[s])
    m_lo = jnp.max(m_lo, axis=0, keepdims=True)
    m_hi = jnp.max(m_hi, axis=0, keepdims=True)
    yield 'M'

    acc_lo = jnp.zeros((OROWS, TT), F32)
    acc_hi = jnp.zeros((OROWS, TT), F32)
    for s, (lo, kv, x) in enumerate(slots):
        m_row = m_hi if lo is False else jnp.where(lo, m_lo, m_hi)
        if bias[s] is not None:
            m_row = m_row - bias[s]
        p = jnp.exp2(s_ref[s] - m_row)
        pv = _dot(v_tile(kv), p.astype(BF16))[0:OROWS]
        if lo is False:
            acc_hi = acc_hi + pv
        else:
            c_ref[s] = pv
        yield 'B'
    for s in range(half):
        lo, c = slots[s][0], c_ref[s]
        acc_lo = acc_lo + jnp.where(lo, c, 0.0)
        acc_hi = acc_hi + jnp.where(lo, 0.0, c)
    o_lo = acc_lo[0:HEAD_DIM] * (1.0 / acc_lo[HEAD_DIM:HEAD_DIM + 1])
    o_hi = acc_hi[0:HEAD_DIM] * (1.0 / acc_hi[HEAD_DIM:HEAD_DIM + 1])
    return o_lo, o_hi


def _interleave(sweeps):
    outs = [None] * len(sweeps)

    def step(i):
        try:
            return next(sweeps[i])
        except StopIteration as done:
            outs[i] = done.value
            return 'END'

    tok = None
    while tok != 'M':
        tok = step(0)
    for r in range(len(sweeps)):
        cur = None
        nxt = None if r + 1 < len(sweeps) else 'M'
        while cur != 'END' or nxt != 'M':
            if nxt != 'M':
                nxt = step(r + 1)
            if cur != 'END':
                cur = step(r)
    return outs


def _nsa_side(n_sb, qt, clamp, x, qn_ref, qr_ref, kc_ref, vct_ref, ovt_ref, kw_ref, vw_ref, s_refs):
    t0 = qt * TT
    tpos = t0 + lax.broadcasted_iota(jnp.int32, (1, TT), 1)
    kpos0 = lax.broadcasted_iota(jnp.int32, (TT, 1), 0)
    nc = kc_ref.shape[1]

    kc = kc_ref[0]
    cmp_end = lax.broadcasted_iota(jnp.int32, (nc, 1), 0) * CMP_STRIDE + (CMP_LEN - 1)
    cmask = cmp_end <= tpos
    c_slots = [s_refs[r].at[6 + x, 0:nc] for r in range(NSA_REP)]
    maxima, inv_sums = [], []
    for r in range(NSA_REP):
        s = jnp.where(cmask, _dot(kc, qn_ref[0, r * HEAD_DIM:(r + 1) * HEAD_DIM, :]), -jnp.inf)
        c_slots[r][...] = s
        m = jnp.max(_slab_max(s), axis=0, keepdims=True)
        maxima.append(jnp.where(m == -jnp.inf, 0.0, m))
        yield
    for r in range(NSA_REP):
        e = jnp.exp2(c_slots[r][...] - maxima[r])
        c_slots[r][...] = e
        d = jnp.sum(_slab_sum(e), axis=0, keepdims=True)
        inv_sums.append(1.0 / jnp.where(d > 0, d, 1.0))
        yield
    psum = jnp.zeros((nc, TT), F32)
    o_cmp = []
    for r in range(NSA_REP):
        p = c_slots[r][...] * inv_sums[r]
        psum = psum + p
        o_cmp.append(_dot(vct_ref[0], p.astype(BF16)))
        yield
    p_hi = psum.astype(BF16)
    p_lo = (psum - p_hi.astype(F32)).astype(BF16)
    imp = _dot(ovt_ref[...], p_hi) + _dot(ovt_ref[...], p_lo)

    nrow = ovt_ref.shape[0]
    blk = lax.broadcasted_iota(jnp.int32, (nrow, 1), 0)
    cur = tpos // SEL_LEN
    forced = (blk == 0) | (blk == cur) | (blk == cur - 1)
    valid = blk * SEL_LEN <= tpos
    score = jnp.where(forced, FORCE_SCORE, jnp.where(valid, imp, -1.0))
    score = jnp.where(blk < n_sb, score, -2.0)
    blk_f = blk.astype(F32)
    sel_bias = jnp.full((nrow, TT), MASK_BIAS, F32)

    j1 = jnp.maximum(qt - 1, 0) if clamp else qt - 1
    j2 = jnp.maximum(qt - 2, 0) if clamp else qt - 2
    in_window = tpos - (j2 * TT + kpos0) < WINDOW
    masks = (kpos0 <= tpos - t0,
             jnp.broadcast_to(qt >= 1, (TT, TT)) if clamp else None,
             ((qt >= 2) & in_window) if clamp else in_window)
    k_tiles = [kw_ref[0, pl.ds(pl.multiple_of(j * TT, TT), TT), :][:, 0:HEAD_DIM] for j in (qt, j1, j2)]
    v_tiles = [vw_ref[0, j] for j in (qt, j1, j2)]
    w_max, o_win = [], []

    def window_scores(r):
        q = qr_ref[0, r * HEAD_DIM:(r + 1) * HEAD_DIM, :]
        maxima = []
        for t, (k, mask) in enumerate(zip(k_tiles, masks)):
            s = _dot(k, q)
            s = s if mask is None else jnp.where(mask, s, MASK_BIAS)
            s_refs[r][3 * x + t] = s
            maxima.append(_slab_max(s))
        w_max.append(jnp.max(functools.reduce(jnp.maximum, maxima), axis=0, keepdims=True))

    def window_output(r):
        acc = sum(_dot(v, jnp.exp2(s_refs[r][3 * x + t] - w_max[r]).astype(BF16))[0:OROWS]
                  for t, v in enumerate(v_tiles))
        o_win.append(acc[0:HEAD_DIM] * (1.0 / acc[HEAD_DIM:HEAD_DIM + 1]))

    window_steps = ([functools.partial(window_scores, r) for r in range(NSA_REP)]
                    + [functools.partial(window_output, r) for r in range(NSA_REP)])
    n_round = min(N_SEL, n_sb)
    per_step = max(n_round // len(window_steps), 1)

    def over_blocks(pair_op, v):
        slabs = v.reshape(-1, 8, TT)
        v = functools.reduce(pair_op, [slabs[i] for i in range(slabs.shape[0])])
        for shift in (4, 2, 1):
            v = pair_op(v, pltpu.roll(v, shift, axis=0))
        return jnp.tile(v, (nrow // 8, 1))

    for i in range(n_round):
        top = over_blocks(jnp.maximum, score)
        first = over_blocks(jnp.minimum, jnp.where(score == top, blk_f, float(nrow)))
        chosen = blk_f == first
        sel_bias = jnp.where(chosen, 0.0, sel_bias)
        score = jnp.where(chosen, -3.0, score)
        if i % per_step == per_step - 1 and window_steps:
            window_steps.pop(0)()
        yield
    while window_steps:
        window_steps.pop(0)()
    return o_cmp, o_win, sel_bias.astype(BF16)


def _nsa_attn_body(nq, n_sb, qnl_ref, qnh_ref, qrl_ref, qrh_ref, kc_ref, vct_ref, ovt_ref, ks_ref, vs_ref,
                   kw_ref, vw_ref, gl_ref, gh_ref, olo_ref, ohi_ref, q_ref, *scratch):
    qi = pl.program_id(2)
    s_refs, c_refs = scratch[:NSA_REP], scratch[NSA_REP:]
    side_in = ((qi, qnl_ref, qrl_ref), (nq - 1 - qi, qnh_ref, qrh_ref))
    gens = [_nsa_side(n_sb, qt, x == 0 or nq < 6, x, qn_ref, qr_ref, kc_ref, vct_ref, ovt_ref, kw_ref, vw_ref,
                      s_refs) for x, (qt, qn_ref, qr_ref) in enumerate(side_in)]
    sides = [None, None]
    while None in sides:
        for x, gen in enumerate(gens):
            if sides[x] is None:
                try:
                    next(gen)
                except StopIteration as done:
                    sides[x] = done.value
    for x, (_, _, qr_ref) in enumerate(side_in):
        for r in range(NSA_REP):
            q_ref[x, r, 0:HEAD_DIM, :] = qr_ref[0, r * HEAD_DIM:(r + 1) * HEAD_DIM, :]
            q_ref[x, r, HEAD_DIM:AUG, :] = sides[x][2]

    k_tile = lambda kv: ks_ref[0, pl.ds(pl.multiple_of(kv * TT, TT), TT), :]
    v_tile = lambda kv: vs_ref[0, kv]
    o_sel = _interleave([_paired_sweep(nq, qi, lambda x, r=r: q_ref[x, r], k_tile, v_tile,
                                       lambda lo, kv: None, s_refs[r], c_refs[r]) for r in range(NSA_REP)])

    for x, (g_ref, o_ref) in enumerate(((gl_ref, olo_ref), (gh_ref, ohi_ref))):
        o_cmp, o_win, _ = sides[x]
        g = g_ref[0, 0]
        rows = []
        for r in range(NSA_REP):
            gc = g[r * N_BRANCH + 0:r * N_BRANCH + 1]
            gs = g[r * N_BRANCH + 1:r * N_BRANCH + 2]
            gw = g[r * N_BRANCH + 2:r * N_BRANCH + 3]
            rows.append(gc * o_cmp[r] + gs * o_sel[r][x] + gw * o_win[r])
        o_ref[0] = jnp.concatenate(rows, axis=0).T.astype(BF16)


def _nsa_attn(batch, seq_len, qn, qr, kc, vct, ovt, ksa, vs, kwa, vw, gt):
    nq = seq_len // TT
    nc = kc.shape[1]
    n_sb = seq_len // SEL_LEN
    g_rows = NSA_REP * N_BRANCH
    half = nq // 2
    ksa3 = ksa.reshape(batch, seq_len, NSA_GROUPS * AUG)
    kwa3 = kwa.reshape(batch, seq_len, NSA_GROUPS * AUG)
    vs5 = vs.reshape(batch, nq, NSA_GROUPS, VROWS, TT)
    vw5 = vw.reshape(batch, nq, NSA_GROUPS, VROWS, TT)
    gt4 = gt.reshape(batch * nq, NSA_GROUPS, g_rows, TT)
    lo_tile = lambda b, q: b * nq + q
    hi_tile = lambda b, q: b * nq + nq - 1 - q
    q_lo = pl.BlockSpec((1, NSA_REP * HEAD_DIM, TT), lambda b, g, q: (lo_tile(b, q), g, 0))
    q_hi = pl.BlockSpec((1, NSA_REP * HEAD_DIM, TT), lambda b, g, q: (hi_tile(b, q), g, 0))
    kspec = pl.BlockSpec((1, seq_len, AUG), lambda b, g, q: (b, 0, g))
    vspec = pl.BlockSpec((1, nq, None, VROWS, TT), lambda b, g, q: (b, 0, g, 0, 0))
    out = jax.ShapeDtypeStruct((batch, seq_len // 2, HQ), BF16)
    return pl.pallas_call(
        functools.partial(_nsa_attn_body, nq, n_sb),
        grid=(batch, NSA_GROUPS, half),
        in_specs=[
            q_lo, q_hi, q_lo, q_hi,
            pl.BlockSpec((1, nc, HEAD_DIM), lambda b, g, q: (b * NSA_GROUPS + g, 0, 0)),
            pl.BlockSpec((1, HEAD_DIM, nc), lambda b, g, q: (b * NSA_GROUPS + g, 0, 0)),
            pl.BlockSpec(ovt.shape, lambda b, g, q: (0, 0)),
            kspec, vspec, kspec, vspec,
            pl.BlockSpec((1, 1, g_rows, TT), lambda b, g, q: (lo_tile(b, q), g, 0, 0)),
            pl.BlockSpec((1, 1, g_rows, TT), lambda b, g, q: (hi_tile(b, q), g, 0, 0)),
        ],
        out_specs=[pl.BlockSpec((1, TT, NSA_REP * HEAD_DIM), lambda b, g, q: (b, q, g)),
                   pl.BlockSpec((1, TT, NSA_REP * HEAD_DIM), lambda b, g, q: (b, half - 1 - q, g))],
        out_shape=[out, out],
        scratch_shapes=([pltpu.VMEM((2, NSA_REP, AUG, TT), BF16)]
                        + [pltpu.VMEM((nq + 1, TT, TT), F32)] * NSA_REP
                        + [pltpu.VMEM((half, OROWS, TT), F32)] * NSA_REP),
        compiler_params=_cparams(("parallel", "parallel", "arbitrary")),
        name="nsa_attn",
    )(qn, qn, qr, qr, kc, vct, ovt, ksa3, vs5, kwa3, vw5, gt4, gt4)


def _overlap_t(seq_len, nc):
    n_cmp = (seq_len - CMP_LEN) // CMP_STRIDE + 1
    n_sb = seq_len // SEL_LEN
    cmp_start = jnp.arange(n_cmp) * CMP_STRIDE
    sel_start = jnp.arange(n_sb) * SEL_LEN
    ov = jnp.clip(jnp.minimum(cmp_start[:, None] + CMP_LEN, sel_start[None, :] + SEL_LEN)
                  - jnp.maximum(cmp_start[:, None], sel_start[None, :]), 0, None).astype(F32) / CMP_LEN
    out = jnp.zeros((SEL_LEN, nc), F32).at[:n_sb, :n_cmp].set(ov.T)
    return out.astype(BF16)


def _nsa_attention(x, batch, seq_len, w_in, b_gate, pe, k_w1, k_b1, k_w2, v_w1, v_b1, v_w2, tables):
    wt, wn, bg = _nsa_inproj_weights(w_in, b_gate)
    qn, qr, vs, vw, gt, ksa, kwa, kc, vc = _nsa_inproj(x, seq_len, wt, wn, bg, *tables)

    nc = seq_len // CMP_STRIDE
    hw = CMP_STRIDE * HEAD_DIM

    def half_blocks(t):
        t = t.reshape(batch, seq_len, NSA_GROUPS, HEAD_DIM).transpose(0, 2, 1, 3)
        return t.reshape(batch * NSA_GROUPS, nc, hw)

    kcmp, vcmp_t = _compress(
        half_blocks(kc), half_blocks(vc), pe.reshape(2, hw),
        k_w1.reshape(2, hw, PHI_HIDDEN).astype(BF16), k_b1.reshape(1, -1), k_w2.astype(BF16),
        v_w1.reshape(2, hw, PHI_HIDDEN).transpose(0, 2, 1).astype(BF16), v_b1.reshape(-1, 1),
        v_w2.T.astype(BF16))
    return _nsa_attn(batch, seq_len, qn, qr, kcmp, vcmp_t, _overlap_t(seq_len, nc), ksa, vs, kwa, vw, gt)


def _nsa_layer(x, batch, seq_len, w_in, b_gate, pe, k_w1, k_b1, k_w2, v_w1, v_b1, v_w2, w_o,
               ln_g, ln_b, tables):
    a_lo, a_hi = _nsa_attention(x, batch, seq_len, w_in, b_gate, pe, k_w1, k_b1, k_w2, v_w1, v_b1, v_w2,
                                tables)
    return _proj_ln(a_lo, a_hi, w_o.astype(BF16), x, ln_g.reshape(1, -1), ln_b.reshape(1, -1))


_FOX_T_ROWS = 2 * HQ + N_HEADS
_FOX_N_COLS = N_HEADS * AUG + 128
_N_PIECE = 3


def _fox_inproj_body(seq_tiles, x_ref, wt_ref, wn_ref, bft_ref, bfn_ref, place_ref, ones_ref, route_ref,
                     qa_ref, ka_ref, vt_ref, off_ref, run_ref):
    i = pl.program_id(0)
    xb = x_ref[...].astype(BF16)
    t = _dot_nt(wt_ref[...], xb)
    vt_ref[0, :, 0:HEAD_DIM, :] = t[HQ:2 * HQ].reshape(N_HEADS, HEAD_DIM, TT).astype(BF16)
    vt_ref[0, :, HEAD_DIM:VROWS, :] = jnp.ones((N_HEADS, VROWS - HEAD_DIM, TT), BF16)
    lf_t = _log_sigmoid(t[2 * HQ:] + bft_ref[...]) * LOG2E
    n = _dot(xb, wn_ref[...])
    kw = N_HEADS * AUG
    lf_n = _log_sigmoid(n[:, kw:kw + _N_PIECE * N_HEADS] + bfn_ref[...]) * LOG2E

    r_i = lax.broadcasted_iota(jnp.int32, (TT, TT), 0)
    c_i = lax.broadcasted_iota(jnp.int32, (TT, TT), 1)
    upper = ((r_i > 0) & (r_i <= c_i)).astype(BF16)
    lower = ((c_i > 0) & (c_i <= r_i)).astype(BF16)
    a_t = sum(_dot(p, upper) for p in _split3(lf_t))
    a_n = sum(_dot(lower, p) for p in _split3(lf_n))

    q3 = (t[0:HQ] * QSCALE).reshape(N_HEADS, HEAD_DIM, TT)
    qa_ref[0, :, 0:HEAD_DIM, :] = q3.astype(BF16)
    stacked = jnp.concatenate(list(_split3(a_t)) + [jnp.ones((N_HEADS, TT), BF16)], axis=0)
    for h in range(N_HEADS):
        qa_ref[0, h, HEAD_DIM:AUG, :] = _dot(route_ref[h], stacked).astype(BF16)

    b1, b2, b3 = _split3(-a_n)
    grp = lax.broadcasted_iota(jnp.int32, (TT, _N_PIECE * N_HEADS), 1) // N_HEADS
    bsel = jnp.where(grp == 0, b1, jnp.where(grp == 1, b2, b3))
    ka_ref[...] = (n[:, 0:kw] + _dot(bsel, place_ref[...]) + ones_ref[...]).astype(BF16)

    @pl.when(i % seq_tiles == 0)
    def _():
        run_ref[...] = jnp.zeros_like(run_ref)

    first = lf_t[:, 0:1]
    off_ref[0] = jnp.broadcast_to(run_ref[:, 0:1] + first, (N_HEADS, TT))
    run_ref[...] = run_ref[...] + (a_t[:, TT - 1:TT] + first)


def _fox_inproj(x, seq_len, wt, wn, bft, bfn, place, ones, route):
    n = x.shape[0]
    nt = n // TT
    kw = N_HEADS * AUG
    full = lambda r, c: pl.BlockSpec((r, c), lambda i: (0, 0))
    return pl.pallas_call(
        functools.partial(_fox_inproj_body, seq_len // TT),
        grid=(nt,),
        in_specs=[
            pl.BlockSpec((TT, D_MODEL), lambda i: (i, 0)),
            full(_FOX_T_ROWS, D_MODEL), full(D_MODEL, _FOX_N_COLS),
            full(N_HEADS, 1), full(1, _N_PIECE * N_HEADS),
            full(_N_PIECE * N_HEADS, kw), full(1, kw),
            pl.BlockSpec((N_HEADS, AUG - HEAD_DIM, AUG - HEAD_DIM), lambda i: (0, 0, 0)),
        ],
        out_specs=[
            pl.BlockSpec((1, N_HEADS, AUG, TT), lambda i: (i, 0, 0, 0)),
            pl.BlockSpec((TT, kw), lambda i: (i, 0)),
            pl.BlockSpec((1, N_HEADS, VROWS, TT), lambda i: (i, 0, 0, 0)),
            pl.BlockSpec((1, N_HEADS, TT), lambda i: (i, 0, 0)),
        ],
        out_shape=[
            jax.ShapeDtypeStruct((nt, N_HEADS, AUG, TT), BF16),
            jax.ShapeDtypeStruct((n, kw), BF16),
            jax.ShapeDtypeStruct((nt, N_HEADS, VROWS, TT), BF16),
            jax.ShapeDtypeStruct((nt, N_HEADS, TT), F32),
        ],
        scratch_shapes=[pltpu.VMEM((N_HEADS, 128), F32)],
        compiler_params=_cparams(("arbitrary",)),
        name="fox_inproj",
    )(x, wt, wn, bft, bfn, place, ones, route)


def _fox_inproj_weights(w_in, b_f):
    wq, wk, wv, wf = jnp.split(w_in, [HQ, 2 * HQ, 3 * HQ], axis=1)
    wt = jnp.concatenate([wq, wv, wf], axis=1).T.astype(BF16)
    pad = jnp.zeros((D_MODEL, _FOX_N_COLS - N_HEADS * AUG - _N_PIECE * N_HEADS), F32)
    wn = jnp.concatenate([_pad_heads(wk, N_HEADS)] + [wf] * _N_PIECE + [pad], axis=1).astype(BF16)
    bft = b_f.reshape(-1, 1)
    bfn = jnp.tile(b_f, _N_PIECE).reshape(1, -1)
    rows = jnp.arange(_N_PIECE * N_HEADS)
    cols = (rows % N_HEADS) * AUG + HEAD_DIM + _N_PIECE + rows // N_HEADS
    place = jnp.zeros((_N_PIECE * N_HEADS, N_HEADS * AUG), F32).at[rows, cols].set(1.0).astype(BF16)
    lane = jnp.arange(N_HEADS * AUG) % AUG
    ones = ((lane >= HEAD_DIM) & (lane < HEAD_DIM + _N_PIECE)).astype(F32).reshape(1, -1)
    hh = jnp.arange(N_HEADS)
    route = jnp.zeros((N_HEADS, AUG - HEAD_DIM, AUG - HEAD_DIM), F32)
    for k in range(_N_PIECE):
        route = route.at[hh, k, k * N_HEADS + hh].set(1.0)
        route = route.at[hh, _N_PIECE + k, _N_PIECE * N_HEADS].set(1.0)
    return wt, wn, bft, bfn, place, ones, route.astype(BF16)


def _fox_attn_body(nq, qlo_ref, qhi_ref, ka_ref, vt_ref, off_ref, olo_ref, ohi_ref, q_ref, *scratch):
    hg = pl.program_id(1)
    qi = pl.program_id(2)
    s_refs, c_refs = scratch[:HEAD_BLOCK], scratch[HEAD_BLOCK:]
    q_ref[0] = qlo_ref[0]
    q_ref[1] = qhi_ref[0]
    sweeps = []
    for r in range(HEAD_BLOCK):
        h = hg * HEAD_BLOCK + r

        def off_row(t, h=h):
            return off_ref[0, pl.ds(t, 1), pl.ds(h, 1), :].reshape(1, TT)

        off_lo, off_hi = off_row(qi), off_row(nq - 1 - qi)

        def bias_row(lo, kv, off_lo=off_lo, off_hi=off_hi, off_row=off_row):
            base = off_hi if lo is False else jnp.where(lo, off_lo, off_hi)
            return base - off_row(kv)

        def k_tile(kv, r=r):
            return ka_ref[0, pl.ds(pl.multiple_of(kv * TT, TT), TT), r * AUG:(r + 1) * AUG]

        def v_tile(kv, r=r):
            return vt_ref[0, kv, r]

        sweeps.append(_paired_sweep(nq, qi, lambda x, r=r: q_ref[x, r], k_tile, v_tile, bias_row,
                                    s_refs[r], c_refs[r]))
    for r, (o_lo, o_hi) in enumerate(_interleave(sweeps)):
        olo_ref[0, r * HEAD_DIM:(r + 1) * HEAD_DIM, :] = o_lo.astype(BF16)
        ohi_ref[0, r * HEAD_DIM:(r + 1) * HEAD_DIM, :] = o_hi.astype(BF16)


def _fox_attn(batch, seq_len, qa, ka, vt, off):
    nq = seq_len // TT
    half = nq // 2
    hb = HEAD_BLOCK
    ka3 = ka.reshape(batch, seq_len, N_HEADS * AUG)
    vt5 = vt.reshape(batch, nq, N_HEADS, VROWS, TT)
    off4 = off.reshape(batch, nq, N_HEADS, TT)
    out = jax.ShapeDtypeStruct((batch * half, HQ, TT), BF16)
    return pl.pallas_call(
        functools.partial(_fox_attn_body, nq),
        grid=(batch, N_HEADS // hb, half),
        in_specs=[
            pl.BlockSpec((1, hb, AUG, TT), lambda b, h, q: (b * nq + q, h, 0, 0)),
            pl.BlockSpec((1, hb, AUG, TT), lambda b, h, q: (b * nq + nq - 1 - q, h, 0, 0)),
            pl.BlockSpec((1, seq_len, hb * AUG), lambda b, h, q: (b, 0, h)),
            pl.BlockSpec((1, nq, hb, VROWS, TT), lambda b, h, q: (b, 0, h, 0, 0)),
            pl.BlockSpec((1, nq, N_HEADS, TT), lambda b, h, q: (b, 0, 0, 0)),
        ],
        out_specs=[pl.BlockSpec((1, hb * HEAD_DIM, TT), lambda b, h, q: (b * half + q, h, 0)),
                   pl.BlockSpec((1, hb * HEAD_DIM, TT), lambda b, h, q: (b * half + half - 1 - q, h, 0))],
        out_shape=[out, out],
        scratch_shapes=([pltpu.VMEM((2, hb, AUG, TT), BF16)]
                        + [pltpu.VMEM((nq + 1, TT, TT), F32)] * hb
                        + [pltpu.VMEM((half, OROWS, TT), F32)] * hb),
        compiler_params=_cparams(("parallel", "parallel", "arbitrary")),
        name="fox_attn",
    )(qa, qa, ka3, vt5, off4)


def _proj_ln_t_body(half, lo_ref, hi_ref, w_ref, x_ref, g_ref, b_ref, o_ref):
    in_lo = (pl.program_id(0) % (2 * half)) < half
    at = jnp.where(in_lo, lo_ref[0], hi_ref[0])
    y = _dot(at.T, w_ref[...])
    o_ref[...] = _layer_norm(ALPHA * x_ref[...] + y, g_ref[...], b_ref[...])


def _proj_ln_t(at_lo, at_hi, seq_len, w, x, g, b):
    k = at_lo.shape[1]
    n, d = x.shape
    nq = seq_len // TT
    half = nq // 2
    return pl.pallas_call(
        functools.partial(_proj_ln_t_body, half),
        grid=(n // TT,),
        in_specs=[
            pl.BlockSpec((1, k, TT), lambda i: ((i // nq) * half + jnp.minimum(i % nq, half - 1), 0, 0)),
            pl.BlockSpec((1, k, TT), lambda i: ((i // nq) * half + jnp.maximum(i % nq - half, 0), 0, 0)),
            pl.BlockSpec((k, d), lambda i: (0, 0)),
            pl.BlockSpec((TT, d), lambda i: (i, 0)),
            pl.BlockSpec((1, d), lambda i: (0, 0)),
            pl.BlockSpec((1, d), lambda i: (0, 0)),
        ],
        out_specs=pl.BlockSpec((TT, d), lambda i: (i, 0)),
        out_shape=jax.ShapeDtypeStruct((n, d), F32),
        compiler_params=_cparams(("parallel",)),
        name="proj_ln_t",
    )(at_lo, at_hi, w, x, g, b)


def _fox_layer(x, batch, seq_len, w_in, b_f, w_o, ln_g, ln_b):
    qa, ka, vt, off = _fox_inproj(x, seq_len, *_fox_inproj_weights(w_in, b_f))
    at_lo, at_hi = _fox_attn(batch, seq_len, qa, ka, vt, off)
    return _proj_ln_t(at_lo, at_hi, seq_len, w_o.astype(BF16), x, ln_g.reshape(1, -1), ln_b.reshape(1, -1))


def kernel(x, nsa_w_in, nsa_b_gate, nsa_pe, nsa_phik_w1, nsa_phik_b1, nsa_phik_w2, nsa_phiv_w1,
           nsa_phiv_b1, nsa_phiv_w2, nsa_w_o, fox_w_in, fox_b_f, fox_w_o, ffn_w_up, ffn_conv_w,
           ffn_conv_b, ffn_w_down, ln1_g, ln1_b, ln2_g, ln2_b):
    batch, seq_len, d = x.shape
    assert d == D_MODEL and seq_len % (2 * PROJ_TM) == 0 and seq_len % FFN_TM == 0 and seq_len % (2 * TT) == 0
    assert seq_len // SEL_LEN <= AUG - HEAD_DIM
    tables = _rope_tables(seq_len)
    h = x.reshape(batch * seq_len, d)
    for i in range(DEPTH):
        j = i // 2
        if i % 2 == 0:
            h = _nsa_layer(h, batch, seq_len, nsa_w_in[j], nsa_b_gate[j], nsa_pe[j], nsa_phik_w1[j],
                           nsa_phik_b1[j], nsa_phik_w2[j], nsa_phiv_w1[j], nsa_phiv_b1[j],
                           nsa_phiv_w2[j], nsa_w_o[j], ln1_g[i], ln1_b[i], tables)
        else:
            h = _fox_layer(h, batch, seq_len, fox_w_in[j], fox_b_f[j], fox_w_o[j], ln1_g[i], ln1_b[i])
        h = _ffn(h, seq_len, *_ffn_weights(ffn_w_up[i], ffn_conv_w[i], ffn_conv_b[i], ffn_w_down[i]),
                 ln2_g[i].reshape(1, -1), ln2_b[i].reshape(1, -1))
    return h.reshape(batch, seq_len, d)
```

```python
import functools
import math

import jax
import jax.numpy as jnp
from jax import lax
from jax.experimental import pallas as pl
from jax.experimental.pallas import tpu as pltpu

F32 = jnp.float32
BF16 = jnp.bfloat16

D_MODEL = 1024
DEPTH = 4
HEAD_DIM = 64
N_HEADS = 16
HQ = N_HEADS * HEAD_DIM
NSA_GROUPS = 4
NSA_REP = 4
NSA_KV = NSA_GROUPS * HEAD_DIM
CMP_LEN = 32
CMP_STRIDE = 16
SEL_LEN = 64
N_SEL = 16
WINDOW = 512
PHI_HIDDEN = 256
N_BRANCH = 3
FORCE_SCORE = 1e4
D_FF = 2816
ROPE_THETA = 10000.0
ALPHA = (2 * DEPTH) ** 0.25
LN_EPS = 1e-5
SCALE = HEAD_DIM ** -0.5
LOG2E = math.log2(math.e)
QSCALE = SCALE * LOG2E
VROWS = 80
OROWS = 72

TT = 256
AUG = 128
MASK_BIAS = -1e30
FF_CHUNK = 256
FFN_TM = 256
PROJ_TM = 512
HEAD_BLOCK = 4
SWEEP_WIDTH = 1
HALO = 16
VMEM_LIMIT = 56 * 1024 * 1024


def _cparams(sem):
    return pltpu.CompilerParams(dimension_semantics=sem, vmem_limit_bytes=VMEM_LIMIT)


def _layer_norm(z, g, b):
    mu = jnp.mean(z, axis=-1, keepdims=True)
    zc = z - mu
    var = jnp.mean(zc * zc, axis=-1, keepdims=True)
    return zc * lax.rsqrt(var + LN_EPS) * g + b


def _gelu_tanh(x):
    c = math.sqrt(2.0 / math.pi)
    return x * (0.5 * (1.0 + jnp.tanh(c * (x + 0.044715 * (x * x * x)))))


def _log_sigmoid(z):
    return -(jnp.maximum(-z, 0.0) + jnp.log1p(jnp.exp(-jnp.abs(z))))


def _dot(a, b):
    return jnp.dot(a, b, preferred_element_type=F32)


def _dot_nt(a, b):
    return lax.dot_general(a, b, (((1,), (1,)), ((), ())), preferred_element_type=F32)


def _split3(x):
    p1 = x.astype(BF16)
    r1 = x - p1.astype(F32)
    p2 = r1.astype(BF16)
    p3 = (r1 - p2.astype(F32)).astype(BF16)
    return p1, p2, p3


def _proj_ln_body(half, lo_ref, hi_ref, w_ref, x_ref, g_ref, b_ref, o_ref):
    in_lo = (pl.program_id(0) % (2 * half)) < half
    y = _dot(jnp.where(in_lo, lo_ref[0], hi_ref[0]), w_ref[...])
    o_ref[...] = _layer_norm(ALPHA * x_ref[...] + y, g_ref[...], b_ref[...])


def _proj_ln(a_lo, a_hi, w, x, g, b):
    _, s_half, k = a_lo.shape
    n, d = x.shape
    tm = PROJ_TM
    half = s_half // tm
    return pl.pallas_call(
        functools.partial(_proj_ln_body, half),
        grid=(n // tm,),
        in_specs=[
            pl.BlockSpec((1, tm, k), lambda i: (i // (2 * half), jnp.minimum(i % (2 * half), half - 1), 0)),
            pl.BlockSpec((1, tm, k), lambda i: (i // (2 * half), jnp.maximum(i % (2 * half) - half, 0), 0)),
            pl.BlockSpec((k, d), lambda i: (0, 0)),
            pl.BlockSpec((tm, d), lambda i: (i, 0)),
            pl.BlockSpec((1, d), lambda i: (0, 0)),
            pl.BlockSpec((1, d), lambda i: (0, 0)),
        ],
        out_specs=pl.BlockSpec((tm, d), lambda i: (i, 0)),
        out_shape=jax.ShapeDtypeStruct((n, d), F32),
        compiler_params=_cparams(("parallel",)),
        name="proj_ln",
    )(a_lo, a_hi, w, x, g, b)


GELU_C0 = math.sqrt(2.0 / math.pi)
GELU_C1 = GELU_C0 * 0.044715


def _ffn_body(seq_tiles, x_ref, xp_ref, wa_ref, wb_ref, cwa_ref, cwb_ref, wd_ref, g_ref, b_ref, o_ref,
              perm_ref):
    i = pl.program_id(0)
    tm, d = x_ref.shape
    nj = tm // 8
    n_lane = d // 128
    for cb in range(n_lane):
        perm_ref[cb] = x_ref[:, cb * 128:(cb + 1) * 128]
    x = jnp.concatenate(
        [jnp.concatenate([perm_ref[cb, pl.ds(j, 8, stride=nj), :] for cb in range(n_lane)], axis=1)
         for j in range(nj)], axis=0)
    starts_seq = (i % seq_tiles) == 0
    halo = jnp.where(starts_seq, 0.0, xp_ref[...]).astype(BF16)
    xcat = jnp.concatenate([halo, x.astype(BF16)], axis=0)
    n_chunks = wa_ref.shape[0]
    first_sublane = lax.broadcasted_iota(jnp.int32, (8, 1), 0) == 0

    def up(c):
        return _dot(xcat, wa_ref[c]), _dot(xcat, wb_ref[c])

    def conv(h, cw):
        h3 = h[HALO:].reshape(nj, 8, h.shape[-1])

        def wrap(slab, halo_row):
            return jnp.where(first_sublane, halo_row, pltpu.roll(slab, 1, axis=0))[None]

        prev1 = wrap(h3[nj - 1], h[HALO - 1:HALO])
        prev2 = wrap(h3[nj - 2], h[HALO - 2:HALO - 1])
        s1 = jnp.concatenate([prev1, h3[:-1]], axis=0)
        s2 = jnp.concatenate([prev2, prev1, h3[:-2]], axis=0)
        return (cw[0:1] * s2 + cw[1:2] * s1 + cw[2:3] * h3 + cw[3:4]).reshape(tm, h.shape[-1])

    y = None
    nxt = up(0)
    for c in range(n_chunks):
        cur, nxt = nxt, (up(c + 1) if c + 1 < n_chunks else None)
        ha = conv(cur[0], cwa_ref[c])
        hb = conv(cur[1], cwb_ref[c])
        u = ha * (GELU_C0 + GELU_C1 * (ha * ha))
        gated = ((ha + ha * jnp.tanh(u)) * hb).astype(BF16)
        part = _dot(gated, wd_ref[c])
        y = part if y is None else y + part
    out = _layer_norm(ALPHA * x + y, g_ref[...], b_ref[...])
    for j in range(nj):
        for cb in range(n_lane):
            perm_ref[cb, pl.ds(j, 8, stride=nj), :] = out[j * 8:(j + 1) * 8, cb * 128:(cb + 1) * 128]
    for cb in range(n_lane):
        o_ref[:, cb * 128:(cb + 1) * 128] = perm_ref[cb]


def _ffn(x, seq_len, wa, wb, cwa, cwb, wd, g, b):
    n, d = x.shape
    tm = FFN_TM
    nc = wa.shape[0]
    cf = wa.shape[2]
    hblk = tm // HALO
    return pl.pallas_call(
        functools.partial(_ffn_body, seq_len // tm),
        grid=(n // tm,),
        in_specs=[
            pl.BlockSpec((tm, d), lambda i: (i, 0)),
            pl.BlockSpec((HALO, d), lambda i: (jnp.maximum(i * hblk - 1, 0), 0)),
            pl.BlockSpec((nc, d, cf), lambda i: (0, 0, 0)),
            pl.BlockSpec((nc, d, cf), lambda i: (0, 0, 0)),
            pl.BlockSpec((nc, 4, cf), lambda i: (0, 0, 0)),
            pl.BlockSpec((nc, 4, cf), lambda i: (0, 0, 0)),
            pl.BlockSpec((nc, cf, d), lambda i: (0, 0, 0)),
            pl.BlockSpec((1, d), lambda i: (0, 0)),
            pl.BlockSpec((1, d), lambda i: (0, 0)),
        ],
        out_specs=pl.BlockSpec((tm, d), lambda i: (i, 0)),
        out_shape=jax.ShapeDtypeStruct((n, d), F32),
        scratch_shapes=[pltpu.VMEM((d // 128, tm, 128), F32)],
        compiler_params=_cparams(("parallel",)),
        name="conv_ffn",
    )(x, x, wa, wb, cwa, cwb, wd, g, b)


def _ffn_weights(w_up, conv_w, conv_b, w_down):
    nc = D_FF // FF_CHUNK

    def up(w):
        return w.reshape(D_MODEL, nc, FF_CHUNK).transpose(1, 0, 2).astype(BF16)

    def taps(cw, cb):
        t = jnp.concatenate([cw, cb[None]], axis=0)
        return t.reshape(4, nc, FF_CHUNK).transpose(1, 0, 2)

    wa, wb = up(w_up[:, :D_FF]), up(w_up[:, D_FF:])
    cwa = taps(conv_w[:, :D_FF], conv_b[:D_FF])
    cwb = 0.5 * taps(conv_w[:, D_FF:], conv_b[D_FF:])
    wd = w_down.reshape(nc, FF_CHUNK, D_MODEL).astype(BF16)
    return wa, wb, cwa, cwb, wd


_NSA_T_ROWS = HQ + 2 * NSA_KV + N_BRANCH * N_HEADS
_NSA_N_COLS = 2 * NSA_KV + 2 * NSA_GROUPS * AUG


def _nsa_inproj_body(seq_tiles, x_ref, wt_ref, wn_ref, bg_ref, cost_ref, sint_ref, cosn_ref, sinn_ref,
                     qn_ref, qr_ref, vs_ref, vw_ref, gt_ref, ksa_ref, kwa_ref, kc_ref, vc_ref):
    i = pl.program_id(0)
    xb = x_ref[...].astype(BF16)
    t = _dot_nt(wt_ref[...], xb)
    n = _dot(xb, wn_ref[...])
    q = t[0:HQ] * QSCALE
    qn_ref[0] = q.astype(BF16)
    q3 = q.reshape(N_HEADS, HEAD_DIM, TT)
    half = HEAD_DIM // 2
    rot = jnp.concatenate([-q3[:, half:], q3[:, :half]], axis=1)
    qr = q3 * cost_ref[0][None] + rot * sint_ref[0][None]
    qr_ref[0] = qr.reshape(HQ, TT).astype(BF16)
    ones = jnp.ones((NSA_GROUPS, VROWS - HEAD_DIM, TT), BF16)
    for v_ref, lo in ((vs_ref, HQ), (vw_ref, HQ + NSA_KV)):
        v_ref[0, :, 0:HEAD_DIM, :] = t[lo:lo + NSA_KV].reshape(NSA_GROUPS, HEAD_DIM, TT).astype(BF16)
        v_ref[0, :, HEAD_DIM:VROWS, :] = ones
    gt_ref[0] = jax.nn.sigmoid(t[HQ + 2 * NSA_KV:] + bg_ref[...])

    kc_ref[...] = n[:, 0:NSA_KV]
    vc_ref[...] = n[:, NSA_KV:2 * NSA_KV]
    width = NSA_GROUPS * AUG
    lane = lax.broadcasted_iota(jnp.int32, (TT, width), 1) % AUG
    cosn, sinn = cosn_ref[...], sinn_ref[...]

    def rope_nat(k):
        rh = jnp.where(lane < half, -pltpu.roll(k, width - half, axis=1), pltpu.roll(k, half, axis=1))
        return k * cosn + rh * sinn

    ks = rope_nat(n[:, 2 * NSA_KV:2 * NSA_KV + width])
    kw = rope_nat(n[:, 2 * NSA_KV + width:])
    row = lax.broadcasted_iota(jnp.int32, (TT, width), 0)
    blk = ((i % seq_tiles) * TT + row) // SEL_LEN
    onehot = (lane - HEAD_DIM == blk).astype(F32)
    ksa_ref[...] = (ks + onehot).astype(BF16)
    kwa_ref[...] = kw.astype(BF16)


def _nsa_inproj(x, seq_len, wt, wn, bg, cost, sint, cosn, sinn):
    n = x.shape[0]
    nt = n // TT
    seq_tiles = seq_len // TT
    width = NSA_GROUPS * AUG
    full = lambda r, c: pl.BlockSpec((r, c), lambda i: (0, 0))
    tile3 = lambda r: pl.BlockSpec((1, r, TT), lambda i: (i, 0, 0))
    nat = lambda c: pl.BlockSpec((TT, c), lambda i: (i, 0))
    vtile = pl.BlockSpec((1, NSA_GROUPS, VROWS, TT), lambda i: (i, 0, 0, 0))
    return pl.pallas_call(
        functools.partial(_nsa_inproj_body, seq_tiles),
        grid=(nt,),
        in_specs=[
            nat(D_MODEL),
            full(_NSA_T_ROWS, D_MODEL),
            full(D_MODEL, _NSA_N_COLS),
            full(N_BRANCH * N_HEADS, 1),
            pl.BlockSpec((1, HEAD_DIM, TT), lambda i: (i % seq_tiles, 0, 0)),
            pl.BlockSpec((1, HEAD_DIM, TT), lambda i: (i % seq_tiles, 0, 0)),
            pl.BlockSpec((TT, width), lambda i: (i % seq_tiles, 0)),
            pl.BlockSpec((TT, width), lambda i: (i % seq_tiles, 0)),
        ],
        out_specs=[tile3(HQ), tile3(HQ), vtile, vtile, tile3(N_BRANCH * N_HEADS),
                   nat(width), nat(width), nat(NSA_KV), nat(NSA_KV)],
        out_shape=[
            jax.ShapeDtypeStruct((nt, HQ, TT), BF16),
            jax.ShapeDtypeStruct((nt, HQ, TT), BF16),
            jax.ShapeDtypeStruct((nt, NSA_GROUPS, VROWS, TT), BF16),
            jax.ShapeDtypeStruct((nt, NSA_GROUPS, VROWS, TT), BF16),
            jax.ShapeDtypeStruct((nt, N_BRANCH * N_HEADS, TT), F32),
            jax.ShapeDtypeStruct((n, width), BF16),
            jax.ShapeDtypeStruct((n, width), BF16),
            jax.ShapeDtypeStruct((n, NSA_KV), F32),
            jax.ShapeDtypeStruct((n, NSA_KV), F32),
        ],
        compiler_params=_cparams(("parallel",)),
        name="nsa_inproj",
    )(x, wt, wn, bg, cost, sint, cosn, sinn)


def _pad_heads(w, n_heads):
    w3 = w.reshape(w.shape[0], n_heads, HEAD_DIM)
    return jnp.concatenate([w3, jnp.zeros_like(w3)], axis=-1).reshape(w.shape[0], n_heads * AUG)


def _nsa_inproj_weights(w_in, b_gate):
    cuts = [HQ + i * NSA_KV for i in range(7)]
    wq, wkc, wvc, wks, wvs, wkw, wvw, wg = jnp.split(w_in, cuts, axis=1)
    wt = jnp.concatenate([wq, wvs, wvw, wg], axis=1).T.astype(BF16)
    wn = jnp.concatenate([wkc, wvc, _pad_heads(wks, NSA_GROUPS), _pad_heads(wkw, NSA_GROUPS)],
                         axis=1).astype(BF16)
    return wt, wn, b_gate.reshape(-1, 1)


def _rope_tables(seq_len):
    inv = ROPE_THETA ** (-jnp.arange(0, HEAD_DIM, 2, dtype=F32) / HEAD_DIM)
    ang = jnp.arange(seq_len, dtype=F32)[:, None] * inv[None, :]
    ang = jnp.concatenate([ang, ang], axis=-1)
    cos, sin = jnp.cos(ang), jnp.sin(ang)
    seq_tiles = seq_len // TT

    def transposed(t):
        return t.reshape(seq_tiles, TT, HEAD_DIM).transpose(0, 2, 1)

    def natural(t):
        return jnp.tile(jnp.concatenate([t, jnp.zeros_like(t)], axis=1), (1, NSA_GROUPS))

    return transposed(cos), transposed(sin), natural(cos), natural(sin)


def _compress_body(hk_ref, hv_ref, pe_ref, w1k_ref, b1k_ref, w2k_ref, w1vt_ref, b1v_ref, w2vt_ref,
                   kc_ref, vct_ref):
    nc = hk_ref.shape[1]
    pe_top, pe_bot = pe_ref[0:1], pe_ref[1:2]
    hk = hk_ref[0]
    top = _dot((hk + pe_top).astype(BF16), w1k_ref[0])
    bot = _dot((hk + pe_bot).astype(BF16), w1k_ref[1])
    hid = _gelu_tanh(top + pltpu.roll(bot, nc - 1, axis=0) + b1k_ref[...])
    kc_ref[0] = _dot(hid.astype(BF16), w2k_ref[...]).astype(BF16)

    hv = hv_ref[0]
    top_t = _dot_nt(w1vt_ref[0], (hv + pe_top).astype(BF16))
    bot_t = _dot_nt(w1vt_ref[1], (hv + pe_bot).astype(BF16))
    hid_t = _gelu_tanh(top_t + pltpu.roll(bot_t, nc - 1, axis=1) + b1v_ref[...])
    vct_ref[0] = _dot(w2vt_ref[...], hid_t.astype(BF16)).astype(BF16)


def _compress(hk, hv, pe2, w1k, b1k, w2k, w1vt, b1v, w2vt):
    bg, nc, hw = hk.shape
    cst = lambda shape: pl.BlockSpec(shape, lambda i: (0,) * len(shape))
    return pl.pallas_call(
        _compress_body,
        grid=(bg,),
        in_specs=[
            pl.BlockSpec((1, nc, hw), lambda i: (i, 0, 0)),
            pl.BlockSpec((1, nc, hw), lambda i: (i, 0, 0)),
            cst((2, hw)),
            cst((2, hw, PHI_HIDDEN)), cst((1, PHI_HIDDEN)), cst((PHI_HIDDEN, HEAD_DIM)),
            cst((2, PHI_HIDDEN, hw)), cst((PHI_HIDDEN, 1)), cst((HEAD_DIM, PHI_HIDDEN)),
        ],
        out_specs=[pl.BlockSpec((1, nc, HEAD_DIM), lambda i: (i, 0, 0)),
                   pl.BlockSpec((1, HEAD_DIM, nc), lambda i: (i, 0, 0))],
        out_shape=[jax.ShapeDtypeStruct((bg, nc, HEAD_DIM), BF16),
                   jax.ShapeDtypeStruct((bg, HEAD_DIM, nc), BF16)],
        compiler_params=_cparams(("parallel",)),
        name="nsa_compress",
    )(hk, hv, pe2, w1k, b1k, w2k, w1vt, b1v, w2vt)


def _slab_max(s):
    return jnp.max(s.reshape(-1, 8, s.shape[-1]), axis=0)


def _slab_min(s):
    return jnp.min(s.reshape(-1, 8, s.shape[-1]), axis=0)


def _slab_sum(p):
    return jnp.sum(p.reshape(-1, 8, p.shape[-1]), axis=0)


def _paired_sweep(nq, qi, q_of, k_tile, v_tile, bias_row, s_ref, c_ref):
    tpos = lax.broadcasted_iota(jnp.int32, (1, TT), 1)
    kpos = lax.broadcasted_iota(jnp.int32, (TT, 1), 0)
    causal = kpos <= tpos
    half = nq // 2
    slots = []
    for s in range(nq + 1):
        if s < half:
            lo = s <= qi
            slots.append((lo, jnp.where(lo, s, s - qi - 1), jnp.where(lo, 0, 1)))
        else:
            slots.append((False, s - qi - 1, 1))

    slot_max, bias = [], []
    for s, (lo, kv, x) in enumerate(slots):
        sc = _dot(k_tile(kv), q_of(x))
        if s == nq:
            sc = jnp.where(causal, sc, MASK_BIAS)
        s_ref[s] = sc
        b = bias_row(lo, kv)
        bias.append(b)
        slot_max.append(_slab_max(sc) if b is None else _slab_max(sc) + b)
        yield 'A'
    diag = jnp.where(causal, s_ref[qi], MASK_BIAS)
    s_ref[qi] = diag
    m_lo = _slab_max(diag)
    m_hi = slot_max[nq]
    for s in range(nq):
        if s < half:
            m_lo = jnp.maximum(m_lo, jnp.where(s < qi, slot_max[s], MASK_BIAS))
            m_hi = jnp.maximum(m_hi, jnp.where(s > qi, slot_max[s], MASK_BIAS))
        else:
            m_hi = jnp.maximum(m_hi, slot_max[s])
    m_lo = jnp.max(m_lo, axis=0, keepdims=True)
    m_hi = jnp.max(m_hi, axis=0, keepdims=True)
    yield 'M'

    acc_lo = jnp.zeros((OROWS, TT), F32)
    acc_hi = jnp.zeros((OROWS, TT), F32)
    for s, (lo, kv, x) in enumerate(slots):
        m_row = m_hi if lo is False else jnp.where(lo, m_lo, m_hi)
        if bias[s] is not None:
            m_row = m_row - bias[s]
        p = jnp.exp2(s_ref[s] - m_row)
        pv = _dot(v_tile(kv), p.astype(BF16))[0:OROWS]
        if lo is False:
            acc_hi = acc_hi + pv
        else:
            c_ref[s] = pv
        yield 'B'
    for s in range(half):
        lo, c = slots[s][0], c_ref[s]
        acc_lo = acc_lo + jnp.where(lo, c, 0.0)
        acc_hi = acc_hi + jnp.where(lo, 0.0, c)
    o_lo = acc_lo[0:HEAD_DIM] * (1.0 / acc_lo[HEAD_DIM:HEAD_DIM + 1])
    o_hi = acc_hi[0:HEAD_DIM] * (1.0 / acc_hi[HEAD_DIM:HEAD_DIM + 1])
    return o_lo, o_hi


def _interleave(sweeps, width=SWEEP_WIDTH):
    outs = [None] * len(sweeps)
    groups = [list(range(i, min(i + width, len(sweeps)))) for i in range(0, len(sweeps), width)]

    def step(group, until):
        done = True
        for i in group:
            if state[i] == until:
                continue
            try:
                state[i] = next(sweeps[i])
            except StopIteration as fin:
                outs[i] = fin.value
                state[i] = 'END'
            done = done and state[i] == until
        return done

    state = [None] * len(sweeps)
    while not step(groups[0], 'M'):
        pass
    for g, group in enumerate(groups):
        nxt = groups[g + 1] if g + 1 < len(groups) else []
        cur_done, nxt_done = False, not nxt
        while not (cur_done and nxt_done):
            if not nxt_done:
                nxt_done = step(nxt, 'M')
            if not cur_done:
                cur_done = step(group, 'END')
    return outs


def _nsa_side(n_sb, qt, clamp, x, qn_ref, qr_ref, kc_ref, vct_ref, ovt_ref, kw_ref, vw_ref, s_refs):
    t0 = qt * TT
    tpos = t0 + lax.broadcasted_iota(jnp.int32, (1, TT), 1)
    kpos0 = lax.broadcasted_iota(jnp.int32, (TT, 1), 0)
    nc = kc_ref.shape[1]

    kc = kc_ref[0]
    cmp_end = lax.broadcasted_iota(jnp.int32, (nc, 1), 0) * CMP_STRIDE + (CMP_LEN - 1)
    cmask = cmp_end <= tpos
    c_slots = [s_refs[r].at[6 + x, 0:nc] for r in range(NSA_REP)]
    maxima, inv_sums = [], []
    for r in range(NSA_REP):
        s = jnp.where(cmask, _dot(kc, qn_ref[0, r * HEAD_DIM:(r + 1) * HEAD_DIM, :]), -jnp.inf)
        c_slots[r][...] = s
        m = jnp.max(_slab_max(s), axis=0, keepdims=True)
        maxima.append(jnp.where(m == -jnp.inf, 0.0, m))
        yield
    for r in range(NSA_REP):
        e = jnp.exp2(c_slots[r][...] - maxima[r])
        c_slots[r][...] = e
        d = jnp.sum(_slab_sum(e), axis=0, keepdims=True)
        inv_sums.append(1.0 / jnp.where(d > 0, d, 1.0))
        yield
    psum = jnp.zeros((nc, TT), F32)
    o_cmp = []
    for r in range(NSA_REP):
        p = c_slots[r][...] * inv_sums[r]
        psum = psum + p
        o_cmp.append(_dot(vct_ref[0], p.astype(BF16)))
        yield
    p_hi = psum.astype(BF16)
    p_lo = (psum - p_hi.astype(F32)).astype(BF16)
    imp = _dot(ovt_ref[...], p_hi) + _dot(ovt_ref[...], p_lo)

    nrow = ovt_ref.shape[0]
    blk = lax.broadcasted_iota(jnp.int32, (nrow, 1), 0)
    cur = tpos // SEL_LEN
    forced = (blk == 0) | (blk == cur) | (blk == cur - 1)
    valid = blk * SEL_LEN <= tpos
    score = jnp.where(forced, FORCE_SCORE, jnp.where(valid, imp, -1.0))
    score = jnp.where(blk < n_sb, score, -2.0)
    blk_f = blk.astype(F32)
    sel_bias = jnp.full((nrow, TT), MASK_BIAS, F32)

    j1 = jnp.maximum(qt - 1, 0) if clamp else qt - 1
    j2 = jnp.maximum(qt - 2, 0) if clamp else qt - 2
    in_window = tpos - (j2 * TT + kpos0) < WINDOW
    masks = (kpos0 <= tpos - t0,
             jnp.broadcast_to(qt >= 1, (TT, TT)) if clamp else None,
             ((qt >= 2) & in_window) if clamp else in_window)
    k_tiles = [kw_ref[0, pl.ds(pl.multiple_of(j * TT, TT), TT), :][:, 0:HEAD_DIM] for j in (qt, j1, j2)]
    v_tiles = [vw_ref[0, j] for j in (qt, j1, j2)]
    w_max, o_win = [], []

    def window_scores(r):
        q = qr_ref[0, r * HEAD_DIM:(r + 1) * HEAD_DIM, :]
        maxima = []
        for t, (k, mask) in enumerate(zip(k_tiles, masks)):
            s = _dot(k, q)
            s = s if mask is None else jnp.where(mask, s, MASK_BIAS)
            s_refs[r][3 * x + t] = s
            maxima.append(_slab_max(s))
        w_max.append(jnp.max(functools.reduce(jnp.maximum, maxima), axis=0, keepdims=True))

    def window_output(r):
        acc = sum(_dot(v, jnp.exp2(s_refs[r][3 * x + t] - w_max[r]).astype(BF16))[0:OROWS]
                  for t, v in enumerate(v_tiles))
        o_win.append(acc[0:HEAD_DIM] * (1.0 / acc[HEAD_DIM:HEAD_DIM + 1]))

    window_steps = ([functools.partial(window_scores, r) for r in range(NSA_REP)]
                    + [functools.partial(window_output, r) for r in range(NSA_REP)])
    n_round = min(N_SEL, n_sb)
    per_step = max(n_round // len(window_steps), 1)

    def over_blocks(pair_op, v):
        slabs = v.reshape(-1, 8, TT)
        v = functools.reduce(pair_op, [slabs[i] for i in range(slabs.shape[0])])
        for shift in (4, 2, 1):
            v = pair_op(v, pltpu.roll(v, shift, axis=0))
        return jnp.tile(v, (nrow // 8, 1))

    for i in range(n_round):
        top = over_blocks(jnp.maximum, score)
        first = over_blocks(jnp.minimum, jnp.where(score == top, blk_f, float(nrow)))
        chosen = blk_f == first
        sel_bias = jnp.where(chosen, 0.0, sel_bias)
        score = jnp.where(chosen, -3.0, score)
        if i % per_step == per_step - 1 and window_steps:
            window_steps.pop(0)()
        yield
    while window_steps:
        window_steps.pop(0)()
    return o_cmp, o_win, sel_bias.astype(BF16)


def _nsa_attn_body(nq, n_sb, qnl_ref, qnh_ref, qrl_ref, qrh_ref, kc_ref, vct_ref, ovt_ref, ks_ref, vs_ref,
                   kw_ref, vw_ref, gl_ref, gh_ref, olo_ref, ohi_ref, q_ref, *scratch):
    qi = pl.program_id(2)
    s_refs, c_refs = scratch[:NSA_REP], scratch[NSA_REP:]
    side_in = ((qi, qnl_ref, qrl_ref), (nq - 1 - qi, qnh_ref, qrh_ref))
    gens = [_nsa_side(n_sb, qt, x == 0 or nq < 6, x, qn_ref, qr_ref, kc_ref, vct_ref, ovt_ref, kw_ref, vw_ref,
                      s_refs) for x, (qt, qn_ref, qr_ref) in enumerate(side_in)]
    sides = [None, None]
    while None in sides:
        for x, gen in enumerate(gens):
            if sides[x] is None:
                try:
                    next(gen)
                except StopIteration as done:
                    sides[x] = done.value
    for x, (_, _, qr_ref) in enumerate(side_in):
        for r in range(NSA_REP):
            q_ref[x, r, 0:HEAD_DIM, :] = qr_ref[0, r * HEAD_DIM:(r + 1) * HEAD_DIM, :]
            q_ref[x, r, HEAD_DIM:AUG, :] = sides[x][2]

    k_tile = lambda kv: ks_ref[0, pl.ds(pl.multiple_of(kv * TT, TT), TT), :]
    v_tile = lambda kv: vs_ref[0, kv]
    o_sel = _interleave([_paired_sweep(nq, qi, lambda x, r=r: q_ref[x, r], k_tile, v_tile,
                                       lambda lo, kv: None, s_refs[r], c_refs[r]) for r in range(NSA_REP)])

    for x, (g_ref, o_ref) in enumerate(((gl_ref, olo_ref), (gh_ref, ohi_ref))):
        o_cmp, o_win, _ = sides[x]
        g = g_ref[0, 0]
        rows = []
        for r in range(NSA_REP):
            gc = g[r * N_BRANCH + 0:r * N_BRANCH + 1]
            gs = g[r * N_BRANCH + 1:r * N_BRANCH + 2]
            gw = g[r * N_BRANCH + 2:r * N_BRANCH + 3]
            rows.append(gc * o_cmp[r] + gs * o_sel[r][x] + gw * o_win[r])
        o_ref[0] = jnp.concatenate(rows, axis=0).T.astype(BF16)


def _nsa_attn(batch, seq_len, qn, qr, kc, vct, ovt, ksa, vs, kwa, vw, gt):
    nq = seq_len // TT
    nc = kc.shape[1]
    n_sb = seq_len // SEL_LEN
    g_rows = NSA_REP * N_BRANCH
    half = nq // 2
    ksa3 = ksa.reshape(batch, seq_len, NSA_GROUPS * AUG)
    kwa3 = kwa.reshape(batch, seq_len, NSA_GROUPS * AUG)
    vs5 = vs.reshape(batch, nq, NSA_GROUPS, VROWS, TT)
    vw5 = vw.reshape(batch, nq, NSA_GROUPS, VROWS, TT)
    gt4 = gt.reshape(batch * nq, NSA_GROUPS, g_rows, TT)
    lo_tile = lambda b, q: b * nq + q
    hi_tile = lambda b, q: b * nq + nq - 1 - q
    q_lo = pl.BlockSpec((1, NSA_REP * HEAD_DIM, TT), lambda b, g, q: (lo_tile(b, q), g, 0))
    q_hi = pl.BlockSpec((1, NSA_REP * HEAD_DIM, TT), lambda b, g, q: (hi_tile(b, q), g, 0))
    kspec = pl.BlockSpec((1, seq_len, AUG), lambda b, g, q: (b, 0, g))
    vspec = pl.BlockSpec((1, nq, None, VROWS, TT), lambda b, g, q: (b, 0, g, 0, 0))
    out = jax.ShapeDtypeStruct((batch, seq_len // 2, HQ), BF16)
    return pl.pallas_call(
        functools.partial(_nsa_attn_body, nq, n_sb),
        grid=(batch, NSA_GROUPS, half),
        in_specs=[
            q_lo, q_hi, q_lo, q_hi,
            pl.BlockSpec((1, nc, HEAD_DIM), lambda b, g, q: (b * NSA_GROUPS + g, 0, 0)),
            pl.BlockSpec((1, HEAD_DIM, nc), lambda b, g, q: (b * NSA_GROUPS + g, 0, 0)),
            pl.BlockSpec(ovt.shape, lambda b, g, q: (0, 0)),
            kspec, vspec, kspec, vspec,
            pl.BlockSpec((1, 1, g_rows, TT), lambda b, g, q: (lo_tile(b, q), g, 0, 0)),
            pl.BlockSpec((1, 1, g_rows, TT), lambda b, g, q: (hi_tile(b, q), g, 0, 0)),
        ],
        out_specs=[pl.BlockSpec((1, TT, NSA_REP * HEAD_DIM), lambda b, g, q: (b, q, g)),
                   pl.BlockSpec((1, TT, NSA_REP * HEAD_DIM), lambda b, g, q: (b, half - 1 - q, g))],
        out_shape=[out, out],
        scratch_shapes=([pltpu.VMEM((2, NSA_REP, AUG, TT), BF16)]
                        + [pltpu.VMEM((nq + 1, TT, TT), F32)] * NSA_REP
                        + [pltpu.VMEM((half, OROWS, TT), F32)] * NSA_REP),
        compiler_params=_cparams(("parallel", "parallel", "arbitrary")),
        name="nsa_attn",
    )(qn, qn, qr, qr, kc, vct, ovt, ksa3, vs5, kwa3, vw5, gt4, gt4)


def _overlap_t(seq_len, nc):
    n_cmp = (seq_len - CMP_LEN) // CMP_STRIDE + 1
    n_sb = seq_len // SEL_LEN
    cmp_start = jnp.arange(n_cmp) * CMP_STRIDE
    sel_start = jnp.arange(n_sb) * SEL_LEN
    ov = jnp.clip(jnp.minimum(cmp_start[:, None] + CMP_LEN, sel_start[None, :] + SEL_LEN)
                  - jnp.maximum(cmp_start[:, None], sel_start[None, :]), 0, None).astype(F32) / CMP_LEN
    out = jnp.zeros((SEL_LEN, nc), F32).at[:n_sb, :n_cmp].set(ov.T)
    return out.astype(BF16)


def _nsa_attention(x, batch, seq_len, w_in, b_gate, pe, k_w1, k_b1, k_w2, v_w1, v_b1, v_w2, tables):
    wt, wn, bg = _nsa_inproj_weights(w_in, b_gate)
    qn, qr, vs, vw, gt, ksa, kwa, kc, vc = _nsa_inproj(x, seq_len, wt, wn, bg, *tables)

    nc = seq_len // CMP_STRIDE
    hw = CMP_STRIDE * HEAD_DIM

    def half_blocks(t):
        t = t.reshape(batch, seq_len, NSA_GROUPS, HEAD_DIM).transpose(0, 2, 1, 3)
        return t.reshape(batch * NSA_GROUPS, nc, hw)

    kcmp, vcmp_t = _compress(
        half_blocks(kc), half_blocks(vc), pe.reshape(2, hw),
        k_w1.reshape(2, hw, PHI_HIDDEN).astype(BF16), k_b1.reshape(1, -1), k_w2.astype(BF16),
        v_w1.reshape(2, hw, PHI_HIDDEN).transpose(0, 2, 1).astype(BF16), v_b1.reshape(-1, 1),
        v_w2.T.astype(BF16))
    return _nsa_attn(batch, seq_len, qn, qr, kcmp, vcmp_t, _overlap_t(seq_len, nc), ksa, vs, kwa, vw, gt)


def _nsa_layer(x, batch, seq_len, w_in, b_gate, pe, k_w1, k_b1, k_w2, v_w1, v_b1, v_w2, w_o,
               ln_g, ln_b, tables):
    a_lo, a_hi = _nsa_attention(x, batch, seq_len, w_in, b_gate, pe, k_w1, k_b1, k_w2, v_w1, v_b1, v_w2,
                                tables)
    return _proj_ln(a_lo, a_hi, w_o.astype(BF16), x, ln_g.reshape(1, -1), ln_b.reshape(1, -1))


_FOX_T_ROWS = 2 * HQ + N_HEADS
_FOX_N_COLS = N_HEADS * AUG + 128
_N_PIECE = 3


def _fox_inproj_body(seq_tiles, x_ref, wt_ref, wn_ref, bft_ref, bfn_ref, place_ref, ones_ref, route_ref,
                     qa_ref, ka_ref, vt_ref, off_ref, run_ref):
    i = pl.program_id(0)
    xb = x_ref[...].astype(BF16)
    t = _dot_nt(wt_ref[...], xb)
    n = _dot(xb, wn_ref[...])
    vt_ref[0, :, 0:HEAD_DIM, :] = t[HQ:2 * HQ].reshape(N_HEADS, HEAD_DIM, TT).astype(BF16)
    vt_ref[0, :, HEAD_DIM:VROWS, :] = jnp.ones((N_HEADS, VROWS - HEAD_DIM, TT), BF16)
    lf_t = _log_sigmoid(t[2 * HQ:] + bft_ref[...]) * LOG2E
    kw = N_HEADS * AUG
    lf_n = _log_sigmoid(n[:, kw:kw + _N_PIECE * N_HEADS] + bfn_ref[...]) * LOG2E

    r_i = lax.broadcasted_iota(jnp.int32, (TT, TT), 0)
    c_i = lax.broadcasted_iota(jnp.int32, (TT, TT), 1)
    upper = ((r_i > 0) & (r_i <= c_i)).astype(BF16)
    lower = ((c_i > 0) & (c_i <= r_i)).astype(BF16)
    a_t = sum(_dot(p, upper) for p in _split3(lf_t))
    a_n = sum(_dot(lower, p) for p in _split3(lf_n))

    q3 = (t[0:HQ] * QSCALE).reshape(N_HEADS, HEAD_DIM, TT)
    qa_ref[0, :, 0:HEAD_DIM, :] = q3.astype(BF16)
    stacked = jnp.concatenate(list(_split3(a_t)) + [jnp.ones((N_HEADS, TT), BF16)], axis=0)
    for h in range(N_HEADS):
        qa_ref[0, h, HEAD_DIM:AUG, :] = _dot(route_ref[h], stacked).astype(BF16)

    b1, b2, b3 = _split3(-a_n)
    grp = lax.broadcasted_iota(jnp.int32, (TT, _N_PIECE * N_HEADS), 1) // N_HEADS
    bsel = jnp.where(grp == 0, b1, jnp.where(grp == 1, b2, b3))
    ka_ref[...] = (n[:, 0:kw] + _dot(bsel, place_ref[...]) + ones_ref[...]).astype(BF16)

    @pl.when(i % seq_tiles == 0)
    def _():
        run_ref[...] = jnp.zeros_like(run_ref)

    first = lf_t[:, 0:1]
    off_ref[0] = jnp.broadcast_to(run_ref[:, 0:1] + first, (N_HEADS, TT))
    run_ref[...] = run_ref[...] + (a_t[:, TT - 1:TT] + first)


def _fox_inproj(x, seq_len, wt, wn, bft, bfn, place, ones, route):
    n = x.shape[0]
    nt = n // TT
    kw = N_HEADS * AUG
    full = lambda r, c: pl.BlockSpec((r, c), lambda i: (0, 0))
    return pl.pallas_call(
        functools.partial(_fox_inproj_body, seq_len // TT),
        grid=(nt,),
        in_specs=[
            pl.BlockSpec((TT, D_MODEL), lambda i: (i, 0)),
            full(_FOX_T_ROWS, D_MODEL), full(D_MODEL, _FOX_N_COLS),
            full(N_HEADS, 1), full(1, _N_PIECE * N_HEADS),
            full(_N_PIECE * N_HEADS, kw), full(1, kw),
            pl.BlockSpec((N_HEADS, AUG - HEAD_DIM, AUG - HEAD_DIM), lambda i: (0, 0, 0)),
        ],
        out_specs=[
            pl.BlockSpec((1, N_HEADS, AUG, TT), lambda i: (i, 0, 0, 0)),
            pl.BlockSpec((TT, kw), lambda i: (i, 0)),
            pl.BlockSpec((1, N_HEADS, VROWS, TT), lambda i: (i, 0, 0, 0)),
            pl.BlockSpec((1, N_HEADS, TT), lambda i: (i, 0, 0)),
        ],
        out_shape=[
            jax.ShapeDtypeStruct((nt, N_HEADS, AUG, TT), BF16),
            jax.ShapeDtypeStruct((n, kw), BF16),
            jax.ShapeDtypeStruct((nt, N_HEADS, VROWS, TT), BF16),
            jax.ShapeDtypeStruct((nt, N_HEADS, TT), F32),
        ],
        scratch_shapes=[pltpu.VMEM((N_HEADS, 128), F32)],
        compiler_params=_cparams(("arbitrary",)),
        name="fox_inproj",
    )(x, wt, wn, bft, bfn, place, ones, route)


def _fox_inproj_weights(w_in, b_f):
    wq, wk, wv, wf = jnp.split(w_in, [HQ, 2 * HQ, 3 * HQ], axis=1)
    wt = jnp.concatenate([wq, wv, wf], axis=1).T.astype(BF16)
    pad = jnp.zeros((D_MODEL, _FOX_N_COLS - N_HEADS * AUG - _N_PIECE * N_HEADS), F32)
    wn = jnp.concatenate([_pad_heads(wk, N_HEADS)] + [wf] * _N_PIECE + [pad], axis=1).astype(BF16)
    bft = b_f.reshape(-1, 1)
    bfn = jnp.tile(b_f, _N_PIECE).reshape(1, -1)
    rows = jnp.arange(_N_PIECE * N_HEADS)
    cols = (rows % N_HEADS) * AUG + HEAD_DIM + _N_PIECE + rows // N_HEADS
    place = jnp.zeros((_N_PIECE * N_HEADS, N_HEADS * AUG), F32).at[rows, cols].set(1.0).astype(BF16)
    lane = jnp.arange(N_HEADS * AUG) % AUG
    ones = ((lane >= HEAD_DIM) & (lane < HEAD_DIM + _N_PIECE)).astype(F32).reshape(1, -1)
    hh = jnp.arange(N_HEADS)
    route = jnp.zeros((N_HEADS, AUG - HEAD_DIM, AUG - HEAD_DIM), F32)
    for k in range(_N_PIECE):
        route = route.at[hh, k, k * N_HEADS + hh].set(1.0)
        route = route.at[hh, _N_PIECE + k, _N_PIECE * N_HEADS].set(1.0)
    return wt, wn, bft, bfn, place, ones, route.astype(BF16)


def _fox_attn_body(nq, qlo_ref, qhi_ref, ka_ref, vt_ref, off_ref, olo_ref, ohi_ref, q_ref, *scratch):
    hg = pl.program_id(1)
    qi = pl.program_id(2)
    s_refs, c_refs = scratch[:HEAD_BLOCK], scratch[HEAD_BLOCK:]
    q_ref[0] = qlo_ref[0]
    q_ref[1] = qhi_ref[0]
    sweeps = []
    for r in range(HEAD_BLOCK):
        h = hg * HEAD_BLOCK + r

        def off_row(t, h=h):
            return off_ref[0, pl.ds(t, 1), pl.ds(h, 1), :].reshape(1, TT)

        off_lo, off_hi = off_row(qi), off_row(nq - 1 - qi)

        def bias_row(lo, kv, off_lo=off_lo, off_hi=off_hi, off_row=off_row):
            base = off_hi if lo is False else jnp.where(lo, off_lo, off_hi)
            return base - off_row(kv)

        def k_tile(kv, r=r):
            return ka_ref[0, pl.ds(pl.multiple_of(kv * TT, TT), TT), r * AUG:(r + 1) * AUG]

        def v_tile(kv, r=r):
            return vt_ref[0, kv, r]

        sweeps.append(_paired_sweep(nq, qi, lambda x, r=r: q_ref[x, r], k_tile, v_tile, bias_row,
                                    s_refs[r], c_refs[r]))
    outs = _interleave(sweeps)
    for x, o_ref in enumerate((olo_ref, ohi_ref)):
        o_ref[0] = jnp.concatenate([o[x] for o in outs], axis=0).T.astype(BF16)


def _fox_attn(batch, seq_len, qa, ka, vt, off):
    nq = seq_len // TT
    half = nq // 2
    hb = HEAD_BLOCK
    ka3 = ka.reshape(batch, seq_len, N_HEADS * AUG)
    vt5 = vt.reshape(batch, nq, N_HEADS, VROWS, TT)
    off4 = off.reshape(batch, nq, N_HEADS, TT)
    out = jax.ShapeDtypeStruct((batch, seq_len // 2, HQ), BF16)
    return pl.pallas_call(
        functools.partial(_fox_attn_body, nq),
        grid=(batch, N_HEADS // hb, half),
        in_specs=[
            pl.BlockSpec((1, hb, AUG, TT), lambda b, h, q: (b * nq + q, h, 0, 0)),
            pl.BlockSpec((1, hb, AUG, TT), lambda b, h, q: (b * nq + nq - 1 - q, h, 0, 0)),
            pl.BlockSpec((1, seq_len, hb * AUG), lambda b, h, q: (b, 0, h)),
            pl.BlockSpec((1, nq, hb, VROWS, TT), lambda b, h, q: (b, 0, h, 0, 0)),
            pl.BlockSpec((1, nq, N_HEADS, TT), lambda b, h, q: (b, 0, 0, 0)),
        ],
        out_specs=[pl.BlockSpec((1, TT, hb * HEAD_DIM), lambda b, h, q: (b, q, h)),
                   pl.BlockSpec((1, TT, hb * HEAD_DIM), lambda b, h, q: (b, half - 1 - q, h))],
        out_shape=[out, out],
        scratch_shapes=([pltpu.VMEM((2, hb, AUG, TT), BF16)]
                        + [pltpu.VMEM((nq + 1, TT, TT), F32)] * hb
                        + [pltpu.VMEM((half, OROWS, TT), F32)] * hb),
        compiler_params=_cparams(("parallel", "parallel", "arbitrary")),
        name="fox_attn",
    )(qa, qa, ka3, vt5, off4)


def _fox_layer(x, batch, seq_len, w_in, b_f, w_o, ln_g, ln_b):
    qa, ka, vt, off = _fox_inproj(x, seq_len, *_fox_inproj_weights(w_in, b_f))
    a_lo, a_hi = _fox_attn(batch, seq_len, qa, ka, vt, off)
    return _proj_ln(a_lo, a_hi, w_o.astype(BF16), x, ln_g.reshape(1, -1), ln_b.reshape(1, -1))


def kernel(x, nsa_w_in, nsa_b_gate, nsa_pe, nsa_phik_w1, nsa_phik_b1, nsa_phik_w2, nsa_phiv_w1,
           nsa_phiv_b1, nsa_phiv_w2, nsa_w_o, fox_w_in, fox_b_f, fox_w_o, ffn_w_up, ffn_conv_w,
           ffn_conv_b, ffn_w_down, ln1_g, ln1_b, ln2_g, ln2_b):
    batch, seq_len, d = x.shape
    assert d == D_MODEL and seq_len % (2 * PROJ_TM) == 0 and seq_len % FFN_TM == 0 and seq_len % (2 * TT) == 0
    assert seq_len // SEL_LEN <= AUG - HEAD_DIM
    tables = _rope_tables(seq_len)
    h = x.reshape(batch * seq_len, d)
    for i in range(DEPTH):
        j = i // 2
        if i % 2 == 0:
            h = _nsa_layer(h, batch, seq_len, nsa_w_in[j], nsa_b_gate[j], nsa_pe[j], nsa_phik_w1[j],
                           nsa_phik_b1[j], nsa_phik_w2[j], nsa_phiv_w1[j], nsa_phiv_b1[j],
                           nsa_phiv_w2[j], nsa_w_o[j], ln1_g[i], ln1_b[i], tables)
        else:
            h = _fox_layer(h, batch, seq_len, fox_w_in[j], fox_b_f[j], fox_w_o[j], ln1_g[i], ln1_b[i])
        h = _ffn(h, seq_len, *_ffn_weights(ffn_w_up[i], ffn_conv_w[i], ffn_conv_b[i], ffn_w_down[i]),
                 ln2_g[i].reshape(1, -1), ln2_b[i].reshape(1, -1))
    return h.reshape(batch, seq_len, d)
```

```python
import functools
import math

import jax
import jax.numpy as jnp
from jax import lax
from jax.experimental import pallas as pl
from jax.experimental.pallas import tpu as pltpu

F32 = jnp.float32
BF16 = jnp.bfloat16

D_MODEL = 1024
DEPTH = 4
HEAD_DIM = 64
N_HEADS = 16
HQ = N_HEADS * HEAD_DIM
NSA_GROUPS = 4
NSA_REP = 4
NSA_KV = NSA_GROUPS * HEAD_DIM
CMP_LEN = 32
CMP_STRIDE = 16
SEL_LEN = 64
N_SEL = 16
WINDOW = 512
PHI_HIDDEN = 256
N_BRANCH = 3
FORCE_SCORE = 1e4
D_FF = 2816
ROPE_THETA = 10000.0
ALPHA = (2 * DEPTH) ** 0.25
LN_EPS = 1e-5
SCALE = HEAD_DIM ** -0.5
LOG2E = math.log2(math.e)
QSCALE = SCALE * LOG2E
VROWS = 80
OROWS = 72

TT = 256
AUG = 128
MASK_BIAS = -1e30
FF_CHUNK = 256
FFN_TM = 256
PROJ_TM = 512
HEAD_BLOCK = 4
SWEEP_WIDTH = 1
HALO = 16
VMEM_LIMIT = 56 * 1024 * 1024


def _cparams(sem):
    return pltpu.CompilerParams(dimension_semantics=sem, vmem_limit_bytes=VMEM_LIMIT)


def _layer_norm(z, g, b):
    mu = jnp.mean(z, axis=-1, keepdims=True)
    zc = z - mu
    var = jnp.mean(zc * zc, axis=-1, keepdims=True)
    return zc * lax.rsqrt(var + LN_EPS) * g + b


def _gelu_tanh(x):
    c = math.sqrt(2.0 / math.pi)
    return x * (0.5 * (1.0 + jnp.tanh(c * (x + 0.044715 * (x * x * x)))))


def _log_sigmoid(z):
    return -(jnp.maximum(-z, 0.0) + jnp.log1p(jnp.exp(-jnp.abs(z))))


def _dot(a, b):
    return jnp.dot(a, b, preferred_element_type=F32)


def _dot_nt(a, b):
    return lax.dot_general(a, b, (((1,), (1,)), ((), ())), preferred_element_type=F32)


def _split3(x):
    p1 = x.astype(BF16)
    r1 = x - p1.astype(F32)
    p2 = r1.astype(BF16)
    p3 = (r1 - p2.astype(F32)).astype(BF16)
    return p1, p2, p3


def _proj_ln_body(half, lo_ref, hi_ref, w_ref, x_ref, g_ref, b_ref, o_ref):
    in_lo = (pl.program_id(0) % (2 * half)) < half
    a = jnp.where(in_lo, lo_ref[0], hi_ref[0])
    tm = a.shape[0]
    rows = [slice(r, r + tm // 4) for r in range(0, tm, tm // 4)]
    ys = [_dot(a[r], w_ref[...]) for r in rows]
    for r, y in zip(rows, ys):
        o_ref[r, :] = _layer_norm(ALPHA * x_ref[r, :] + y, g_ref[...], b_ref[...])


def _proj_ln(a_lo, a_hi, w, x, g, b):
    _, s_half, k = a_lo.shape
    n, d = x.shape
    tm = PROJ_TM
    half = s_half // tm
    return pl.pallas_call(
        functools.partial(_proj_ln_body, half),
        grid=(n // tm,),
        in_specs=[
            pl.BlockSpec((1, tm, k), lambda i: (i // (2 * half), jnp.minimum(i % (2 * half), half - 1), 0)),
            pl.BlockSpec((1, tm, k), lambda i: (i // (2 * half), jnp.maximum(i % (2 * half) - half, 0), 0)),
            pl.BlockSpec((k, d), lambda i: (0, 0)),
            pl.BlockSpec((tm, d), lambda i: (i, 0)),
            pl.BlockSpec((1, d), lambda i: (0, 0)),
            pl.BlockSpec((1, d), lambda i: (0, 0)),
        ],
        out_specs=pl.BlockSpec((tm, d), lambda i: (i, 0)),
        out_shape=jax.ShapeDtypeStruct((n, d), F32),
        compiler_params=_cparams(("parallel",)),
        name="proj_ln",
    )(a_lo, a_hi, w, x, g, b)


GELU_C0 = math.sqrt(2.0 / math.pi)
GELU_C1 = GELU_C0 * 0.044715


def _ffn_body(seq_tiles, x_ref, xp_ref, wa_ref, wb_ref, cwa_ref, cwb_ref, wd_ref, g_ref, b_ref, o_ref,
              perm_ref):
    i = pl.program_id(0)
    tm, d = x_ref.shape
    nj = tm // 8
    pitch = nj + 1
    n_lane = d // 128
    for s in range(8):
        for cb in range(n_lane):
            perm_ref[cb, pl.ds(s * pitch, nj), :] = x_ref[s * nj:(s + 1) * nj, cb * 128:(cb + 1) * 128]
    x = jnp.concatenate(
        [jnp.concatenate([perm_ref[cb, pl.ds(j, 8, stride=pitch), :] for cb in range(n_lane)], axis=1)
         for j in range(nj)], axis=0)
    starts_seq = (i % seq_tiles) == 0
    halo = jnp.where(starts_seq, 0.0, xp_ref[...]).astype(BF16)
    xcat = jnp.concatenate([halo, x.astype(BF16)], axis=0)
    n_chunks = wa_ref.shape[0]
    first_sublane = lax.broadcasted_iota(jnp.int32, (8, 1), 0) == 0

    def up(c):
        return _dot(xcat, wa_ref[c]), _dot(xcat, wb_ref[c])

    def conv(h, cw):
        h3 = h[HALO:].reshape(nj, 8, h.shape[-1])

        def wrap(slab, halo_row):
            return jnp.where(first_sublane, halo_row, pltpu.roll(slab, 1, axis=0))[None]

        prev1 = wrap(h3[nj - 1], h[HALO - 1:HALO])
        prev2 = wrap(h3[nj - 2], h[HALO - 2:HALO - 1])
        s1 = jnp.concatenate([prev1, h3[:-1]], axis=0)
        s2 = jnp.concatenate([prev2, prev1, h3[:-2]], axis=0)
        return (cw[0:1] * s2 + cw[1:2] * s1 + cw[2:3] * h3 + cw[3:4]).reshape(tm, h.shape[-1])

    y = None
    nxt = up(0)
    for c in range(n_chunks):
        cur, nxt = nxt, (up(c + 1) if c + 1 < n_chunks else None)
        ha = conv(cur[0], cwa_ref[c])
        act = ha + ha * jnp.tanh(ha * (GELU_C0 + GELU_C1 * (ha * ha)))
        hb = conv(cur[1], cwb_ref[c])
        gated = (act * hb).astype(BF16)
        part = _dot(gated, wd_ref[c])
        y = part if y is None else y + part
    out = _layer_norm(ALPHA * x + y, g_ref[...], b_ref[...])
    for j in range(nj):
        for cb in range(n_lane):
            perm_ref[cb, pl.ds(j, 8, stride=pitch), :] = out[j * 8:(j + 1) * 8, cb * 128:(cb + 1) * 128]
    for s in range(8):
        for cb in range(n_lane):
            o_ref[s * nj:(s + 1) * nj, cb * 128:(cb + 1) * 128] = perm_ref[cb, pl.ds(s * pitch, nj), :]


def _ffn(x, seq_len, wa, wb, cwa, cwb, wd, g, b):
    n, d = x.shape
    tm = FFN_TM
    nc = wa.shape[0]
    cf = wa.shape[2]
    hblk = tm // HALO
    return pl.pallas_call(
        functools.partial(_ffn_body, seq_len // tm),
        grid=(n // tm,),
        in_specs=[
            pl.BlockSpec((tm, d), lambda i: (i, 0)),
            pl.BlockSpec((HALO, d), lambda i: (jnp.maximum(i * hblk - 1, 0), 0)),
            pl.BlockSpec((nc, d, cf), lambda i: (0, 0, 0)),
            pl.BlockSpec((nc, d, cf), lambda i: (0, 0, 0)),
            pl.BlockSpec((nc, 4, cf), lambda i: (0, 0, 0)),
            pl.BlockSpec((nc, 4, cf), lambda i: (0, 0, 0)),
            pl.BlockSpec((nc, cf, d), lambda i: (0, 0, 0)),
            pl.BlockSpec((1, d), lambda i: (0, 0)),
            pl.BlockSpec((1, d), lambda i: (0, 0)),
        ],
        out_specs=pl.BlockSpec((tm, d), lambda i: (i, 0)),
        out_shape=jax.ShapeDtypeStruct((n, d), F32),
        scratch_shapes=[pltpu.VMEM((d // 128, tm + 8, 128), F32)],
        compiler_params=_cparams(("parallel",)),
        name="conv_ffn",
    )(x, x, wa, wb, cwa, cwb, wd, g, b)


def _ffn_weights(w_up, conv_w, conv_b, w_down):
    nc = D_FF // FF_CHUNK

    def up(w):
        return w.reshape(D_MODEL, nc, FF_CHUNK).transpose(1, 0, 2).astype(BF16)

    def taps(cw, cb):
        t = jnp.concatenate([cw, cb[None]], axis=0)
        return t.reshape(4, nc, FF_CHUNK).transpose(1, 0, 2)

    wa, wb = up(w_up[:, :D_FF]), up(w_up[:, D_FF:])
    cwa = taps(conv_w[:, :D_FF], conv_b[:D_FF])
    cwb = 0.5 * taps(conv_w[:, D_FF:], conv_b[D_FF:])
    wd = w_down.reshape(nc, FF_CHUNK, D_MODEL).astype(BF16)
    return wa, wb, cwa, cwb, wd


_NSA_T_ROWS = HQ + 2 * NSA_KV + N_BRANCH * N_HEADS
_NSA_N_COLS = 2 * NSA_KV + 2 * NSA_GROUPS * AUG


def _nsa_inproj_body(seq_tiles, x_ref, wt_ref, wn_ref, bg_ref, cost_ref, sint_ref, cosn_ref, sinn_ref,
                     qn_ref, qr_ref, vs_ref, vw_ref, gt_ref, ksa_ref, kwa_ref, kc_ref, vc_ref):
    i = pl.program_id(0)
    xb = x_ref[...].astype(BF16)
    t = _dot_nt(wt_ref[...], xb)
    n = _dot(xb, wn_ref[...])
    q = t[0:HQ] * QSCALE
    qn_ref[0] = q.astype(BF16)
    q3 = q.reshape(N_HEADS, HEAD_DIM, TT)
    half = HEAD_DIM // 2
    rot = jnp.concatenate([-q3[:, half:], q3[:, :half]], axis=1)
    qr = q3 * cost_ref[0][None] + rot * sint_ref[0][None]
    qr_ref[0] = qr.reshape(HQ, TT).astype(BF16)
    ones = jnp.ones((NSA_GROUPS, VROWS - HEAD_DIM, TT), BF16)
    for v_ref, lo in ((vs_ref, HQ), (vw_ref, HQ + NSA_KV)):
        v_ref[0, :, 0:HEAD_DIM, :] = t[lo:lo + NSA_KV].reshape(NSA_GROUPS, HEAD_DIM, TT).astype(BF16)
        v_ref[0, :, HEAD_DIM:VROWS, :] = ones
    gt_ref[0] = jax.nn.sigmoid(t[HQ + 2 * NSA_KV:] + bg_ref[...])

    kc_ref[...] = n[:, 0:NSA_KV]
    vc_ref[...] = n[:, NSA_KV:2 * NSA_KV]
    width = NSA_GROUPS * AUG
    lane = lax.broadcasted_iota(jnp.int32, (TT, width), 1) % AUG
    cosn, sinn = cosn_ref[...], sinn_ref[...]

    def rope_nat(k):
        rh = jnp.where(lane < half, -pltpu.roll(k, width - half, axis=1), pltpu.roll(k, half, axis=1))
        return k * cosn + rh * sinn

    ks = rope_nat(n[:, 2 * NSA_KV:2 * NSA_KV + width])
    kw = rope_nat(n[:, 2 * NSA_KV + width:])
    row = lax.broadcasted_iota(jnp.int32, (TT, width), 0)
    blk = ((i % seq_tiles) * TT + row) // SEL_LEN
    onehot = (lane - HEAD_DIM == blk).astype(F32)
    ksa_ref[...] = (ks + onehot).astype(BF16)
    kwa_ref[...] = kw.astype(BF16)


def _nsa_inproj(x, seq_len, wt, wn, bg, cost, sint, cosn, sinn):
    n = x.shape[0]
    nt = n // TT
    seq_tiles = seq_len // TT
    width = NSA_GROUPS * AUG
    full = lambda r, c: pl.BlockSpec((r, c), lambda i: (0, 0))
    tile3 = lambda r: pl.BlockSpec((1, r, TT), lambda i: (i, 0, 0))
    nat = lambda c: pl.BlockSpec((TT, c), lambda i: (i, 0))
    vtile = pl.BlockSpec((1, NSA_GROUPS, VROWS, TT), lambda i: (i, 0, 0, 0))
    return pl.pallas_call(
        functools.partial(_nsa_inproj_body, seq_tiles),
        grid=(nt,),
        in_specs=[
            nat(D_MODEL),
            full(_NSA_T_ROWS, D_MODEL),
            full(D_MODEL, _NSA_N_COLS),
            full(N_BRANCH * N_HEADS, 1),
            pl.BlockSpec((1, HEAD_DIM, TT), lambda i: (i % seq_tiles, 0, 0)),
            pl.BlockSpec((1, HEAD_DIM, TT), lambda i: (i % seq_tiles, 0, 0)),
            pl.BlockSpec((TT, width), lambda i: (i % seq_tiles, 0)),
            pl.BlockSpec((TT, width), lambda i: (i % seq_tiles, 0)),
        ],
        out_specs=[tile3(HQ), tile3(HQ), vtile, vtile, tile3(N_BRANCH * N_HEADS),
                   nat(width), nat(width), nat(NSA_KV), nat(NSA_KV)],
        out_shape=[
            jax.ShapeDtypeStruct((nt, HQ, TT), BF16),
            jax.ShapeDtypeStruct((nt, HQ, TT), BF16),
            jax.ShapeDtypeStruct((nt, NSA_GROUPS, VROWS, TT), BF16),
            jax.ShapeDtypeStruct((nt, NSA_GROUPS, VROWS, TT), BF16),
            jax.ShapeDtypeStruct((nt, N_BRANCH * N_HEADS, TT), F32),
            jax.ShapeDtypeStruct((n, width), BF16),
            jax.ShapeDtypeStruct((n, width), BF16),
            jax.ShapeDtypeStruct((n, NSA_KV), F32),
            jax.ShapeDtypeStruct((n, NSA_KV), F32),
        ],
        compiler_params=_cparams(("parallel",)),
        name="nsa_inproj",
    )(x, wt, wn, bg, cost, sint, cosn, sinn)


def _pad_heads(w, n_heads):
    w3 = w.reshape(w.shape[0], n_heads, HEAD_DIM)
    return jnp.concatenate([w3, jnp.zeros_like(w3)], axis=-1).reshape(w.shape[0], n_heads * AUG)


def _nsa_inproj_weights(w_in, b_gate):
    cuts = [HQ + i * NSA_KV for i in range(7)]
    wq, wkc, wvc, wks, wvs, wkw, wvw, wg = jnp.split(w_in, cuts, axis=1)
    wt = jnp.concatenate([wq, wvs, wvw, wg], axis=1).T.astype(BF16)
    wn = jnp.concatenate([wkc, wvc, _pad_heads(wks, NSA_GROUPS), _pad_heads(wkw, NSA_GROUPS)],
                         axis=1).astype(BF16)
    return wt, wn, b_gate.reshape(-1, 1)


def _rope_tables(seq_len):
    inv = ROPE_THETA ** (-jnp.arange(0, HEAD_DIM, 2, dtype=F32) / HEAD_DIM)
    ang = jnp.arange(seq_len, dtype=F32)[:, None] * inv[None, :]
    ang = jnp.concatenate([ang, ang], axis=-1)
    cos, sin = jnp.cos(ang), jnp.sin(ang)
    seq_tiles = seq_len // TT

    def transposed(t):
        return t.reshape(seq_tiles, TT, HEAD_DIM).transpose(0, 2, 1)

    def natural(t):
        return jnp.tile(jnp.concatenate([t, jnp.zeros_like(t)], axis=1), (1, NSA_GROUPS))

    return transposed(cos), transposed(sin), natural(cos), natural(sin)


def _compress_body(hk_ref, hv_ref, pe_ref, w1k_ref, b1k_ref, w2k_ref, w1vt_ref, b1v_ref, w2vt_ref,
                   kc_ref, vct_ref):
    nc = hk_ref.shape[1]
    pe_top, pe_bot = pe_ref[0:1], pe_ref[1:2]
    hk = hk_ref[0]
    top = _dot((hk + pe_top).astype(BF16), w1k_ref[0])
    bot = _dot((hk + pe_bot).astype(BF16), w1k_ref[1])
    hid = _gelu_tanh(top + pltpu.roll(bot, nc - 1, axis=0) + b1k_ref[...])
    kc_ref[0] = _dot(hid.astype(BF16), w2k_ref[...]).astype(BF16)

    hv = hv_ref[0]
    top_t = _dot_nt(w1vt_ref[0], (hv + pe_top).astype(BF16))
    bot_t = _dot_nt(w1vt_ref[1], (hv + pe_bot).astype(BF16))
    hid_t = _gelu_tanh(top_t + pltpu.roll(bot_t, nc - 1, axis=1) + b1v_ref[...])
    vct_ref[0] = _dot(w2vt_ref[...], hid_t.astype(BF16)).astype(BF16)


def _compress(hk, hv, pe2, w1k, b1k, w2k, w1vt, b1v, w2vt):
    bg, nc, hw = hk.shape
    cst = lambda shape: pl.BlockSpec(shape, lambda i: (0,) * len(shape))
    return pl.pallas_call(
        _compress_body,
        grid=(bg,),
        in_specs=[
            pl.BlockSpec((1, nc, hw), lambda i: (i, 0, 0)),
            pl.BlockSpec((1, nc, hw), lambda i: (i, 0, 0)),
            cst((2, hw)),
            cst((2, hw, PHI_HIDDEN)), cst((1, PHI_HIDDEN)), cst((PHI_HIDDEN, HEAD_DIM)),
            cst((2, PHI_HIDDEN, hw)), cst((PHI_HIDDEN, 1)), cst((HEAD_DIM, PHI_HIDDEN)),
        ],
        out_specs=[pl.BlockSpec((1, nc, HEAD_DIM), lambda i: (i, 0, 0)),
                   pl.BlockSpec((1, HEAD_DIM, nc), lambda i: (i, 0, 0))],
        out_shape=[jax.ShapeDtypeStruct((bg, nc, HEAD_DIM), BF16),
                   jax.ShapeDtypeStruct((bg, HEAD_DIM, nc), BF16)],
        compiler_params=_cparams(("parallel",)),
        name="nsa_compress",
    )(hk, hv, pe2, w1k, b1k, w2k, w1vt, b1v, w2vt)


def _slab_max(s):
    return jnp.max(s.reshape(-1, 8, s.shape[-1]), axis=0)


def _slab_min(s):
    return jnp.min(s.reshape(-1, 8, s.shape[-1]), axis=0)


def _slab_sum(p):
    return jnp.sum(p.reshape(-1, 8, p.shape[-1]), axis=0)


def _paired_sweep(nq, qi, q_of, k_tile, v_tile, bias_row, s_ref, c_ref):
    tpos = lax.broadcasted_iota(jnp.int32, (1, TT), 1)
    kpos = lax.broadcasted_iota(jnp.int32, (TT, 1), 0)
    causal = kpos <= tpos
    half = nq // 2
    slots = []
    for s in range(nq + 1):
        if s < half:
            lo = s <= qi
            slots.append((lo, jnp.where(lo, s, s - qi - 1), jnp.where(lo, 0, 1)))
        else:
            slots.append((False, s - qi - 1, 1))

    slot_max, bias = [], []
    for s, (lo, kv, x) in enumerate(slots):
        sc = _dot(k_tile(kv), q_of(x))
        if s == nq:
            sc = jnp.where(causal, sc, MASK_BIAS)
        s_ref[s] = sc
        b = bias_row(lo, kv)
        bias.append(b)
        slot_max.append(_slab_max(sc) if b is None else _slab_max(sc) + b)
        yield 'A'
    diag = jnp.where(causal, s_ref[qi], MASK_BIAS)
    s_ref[qi] = diag
    m_lo = _slab_max(diag)
    m_hi = slot_max[nq]
    for s in range(nq):
        if s < half:
            m_lo = jnp.maximum(m_lo, jnp.where(s < qi, slot_max[s], MASK_BIAS))
            m_hi = jnp.maximum(m_hi, jnp.where(s > qi, slot_max[s], MASK_BIAS))
        else:
            m_hi = jnp.maximum(m_hi, slot_max[s])
    m_lo = jnp.max(m_lo, axis=0, keepdims=True)
    m_hi = jnp.max(m_hi, axis=0, keepdims=True)
    yield 'M'

    acc_lo = jnp.zeros((OROWS, TT), F32)
    acc_hi = jnp.zeros((OROWS, TT), F32)
    for s, (lo, kv, x) in enumerate(slots):
        m_row = m_hi if lo is False else jnp.where(lo, m_lo, m_hi)
        if bias[s] is not None:
            m_row = m_row - bias[s]
        p = jnp.exp2(s_ref[s] - m_row)
        pv = _dot(v_tile(kv), p.astype(BF16))[0:OROWS]
        if lo is False:
            acc_hi = acc_hi + pv
        else:
            c_ref[s] = pv
        yield 'B'
    for s in range(half):
        lo, c = slots[s][0], c_ref[s]
        acc_lo = acc_lo + jnp.where(lo, c, 0.0)
        acc_hi = acc_hi + jnp.where(lo, 0.0, c)
    o_lo = acc_lo[0:HEAD_DIM] * (1.0 / acc_lo[HEAD_DIM:HEAD_DIM + 1])
    o_hi = acc_hi[0:HEAD_DIM] * (1.0 / acc_hi[HEAD_DIM:HEAD_DIM + 1])
    return o_lo, o_hi


def _interleave(sweeps, width=SWEEP_WIDTH):
    outs = [None] * len(sweeps)
    groups = [list(range(i, min(i + width, len(sweeps)))) for i in range(0, len(sweeps), width)]

    def step(group, until):
        done = True
        for i in group:
            if state[i] == until:
                continue
            try:
                state[i] = next(sweeps[i])
            except StopIteration as fin:
                outs[i] = fin.value
                state[i] = 'END'
            done = done and state[i] == until
        return done

    state = [None] * len(sweeps)
    while not step(groups[0], 'M'):
        pass
    for g, group in enumerate(groups):
        nxt = groups[g + 1] if g + 1 < len(groups) else []
        cur_done, nxt_done = False, not nxt
        while not (cur_done and nxt_done):
            if not nxt_done:
                nxt_done = step(nxt, 'M')
            if not cur_done:
                cur_done = step(group, 'END')
    return outs


def _nsa_side(n_sb, qt, clamp, x, qn_ref, qr_ref, kc_ref, vct_ref, ovt_ref, kw_ref, vw_ref, s_refs):
    t0 = qt * TT
    tpos = t0 + lax.broadcasted_iota(jnp.int32, (1, TT), 1)
    kpos0 = lax.broadcasted_iota(jnp.int32, (TT, 1), 0)
    nc = kc_ref.shape[1]

    kc = kc_ref[0]
    cmp_end = lax.broadcasted_iota(jnp.int32, (nc, 1), 0) * CMP_STRIDE + (CMP_LEN - 1)
    cmask = cmp_end <= tpos
    c_slots = [s_refs[r].at[6 + x, 0:nc] for r in range(NSA_REP)]
    maxima, inv_sums = [], []
    for r in range(NSA_REP):
        s = jnp.where(cmask, _dot(kc, qn_ref[0, r * HEAD_DIM:(r + 1) * HEAD_DIM, :]), -jnp.inf)
        c_slots[r][...] = s
        m = jnp.max(_slab_max(s), axis=0, keepdims=True)
        maxima.append(jnp.where(m == -jnp.inf, 0.0, m))
        yield
    for r in range(NSA_REP):
        e = jnp.exp2(c_slots[r][...] - maxima[r])
        c_slots[r][...] = e
        d = jnp.sum(_slab_sum(e), axis=0, keepdims=True)
        inv_sums.append(1.0 / jnp.where(d > 0, d, 1.0))
        yield
    psum = jnp.zeros((nc, TT), F32)
    o_cmp = []
    for r in range(NSA_REP):
        p = c_slots[r][...] * inv_sums[r]
        psum = psum + p
        o_cmp.append(_dot(vct_ref[0], p.astype(BF16)))
        yield
    p_hi = psum.astype(BF16)
    p_lo = (psum - p_hi.astype(F32)).astype(BF16)
    imp = _dot(ovt_ref[...], p_hi) + _dot(ovt_ref[...], p_lo)

    nrow = ovt_ref.shape[0]
    blk = lax.broadcasted_iota(jnp.int32, (nrow, 1), 0)
    cur = tpos // SEL_LEN
    forced = (blk == 0) | (blk == cur) | (blk == cur - 1)
    valid = blk * SEL_LEN <= tpos
    score = jnp.where(forced, FORCE_SCORE, jnp.where(valid, imp, -1.0))
    score = jnp.where(blk < n_sb, score, -2.0)
    blk_f = blk.astype(F32)
    sel_bias = jnp.full((nrow, TT), MASK_BIAS, F32)

    j1 = jnp.maximum(qt - 1, 0) if clamp else qt - 1
    j2 = jnp.maximum(qt - 2, 0) if clamp else qt - 2
    in_window = tpos - (j2 * TT + kpos0) < WINDOW
    masks = (kpos0 <= tpos - t0,
             jnp.broadcast_to(qt >= 1, (TT, TT)) if clamp else None,
             ((qt >= 2) & in_window) if clamp else in_window)
    k_tiles = [kw_ref[0, pl.ds(pl.multiple_of(j * TT, TT), TT), :][:, 0:HEAD_DIM] for j in (qt, j1, j2)]
    v_tiles = [vw_ref[0, j] for j in (qt, j1, j2)]
    w_max, o_win = [], []

    def window_scores(r):
        q = qr_ref[0, r * HEAD_DIM:(r + 1) * HEAD_DIM, :]
        maxima = []
        for t, (k, mask) in enumerate(zip(k_tiles, masks)):
            s = _dot(k, q)
            s = s if mask is None else jnp.where(mask, s, MASK_BIAS)
            s_refs[r][3 * x + t] = s
            maxima.append(_slab_max(s))
        w_max.append(jnp.max(functools.reduce(jnp.maximum, maxima), axis=0, keepdims=True))

    def window_output(r):
        acc = sum(_dot(v, jnp.exp2(s_refs[r][3 * x + t] - w_max[r]).astype(BF16))[0:OROWS]
                  for t, v in enumerate(v_tiles))
        o_win.append(acc[0:HEAD_DIM] * (1.0 / acc[HEAD_DIM:HEAD_DIM + 1]))

    window_steps = ([functools.partial(window_scores, r) for r in range(NSA_REP)]
                    + [functools.partial(window_output, r) for r in range(NSA_REP)])
    n_round = min(N_SEL, n_sb)
    per_step = max(n_round // len(window_steps), 1)

    def over_blocks(pair_op, v):
        slabs = v.reshape(-1, 8, TT)
        v = functools.reduce(pair_op, [slabs[i] for i in range(slabs.shape[0])])
        for shift in (4, 2, 1):
            v = pair_op(v, pltpu.roll(v, shift, axis=0))
        return jnp.tile(v, (nrow // 8, 1))

    for i in range(n_round):
        top = over_blocks(jnp.maximum, score)
        first = over_blocks(jnp.minimum, jnp.where(score == top, blk_f, float(nrow)))
        chosen = blk_f == first
        sel_bias = jnp.where(chosen, 0.0, sel_bias)
        score = jnp.where(chosen, -3.0, score)
        if i % per_step == per_step - 1 and window_steps:
            window_steps.pop(0)()
        yield
    while window_steps:
        window_steps.pop(0)()
    return o_cmp, o_win, sel_bias.astype(BF16)


def _nsa_attn_body(nq, n_sb, qnl_ref, qnh_ref, qrl_ref, qrh_ref, kc_ref, vct_ref, ovt_ref, ks_ref, vs_ref,
                   kw_ref, vw_ref, gl_ref, gh_ref, olo_ref, ohi_ref, q_ref, *scratch):
    qi = pl.program_id(2)
    s_refs, c_refs = scratch[:NSA_REP], scratch[NSA_REP:]
    side_in = ((qi, qnl_ref, qrl_ref), (nq - 1 - qi, qnh_ref, qrh_ref))
    gens = [_nsa_side(n_sb, qt, x == 0 or nq < 6, x, qn_ref, qr_ref, kc_ref, vct_ref, ovt_ref, kw_ref, vw_ref,
                      s_refs) for x, (qt, qn_ref, qr_ref) in enumerate(side_in)]
    sides = [None, None]
    while None in sides:
        for x, gen in enumerate(gens):
            if sides[x] is None:
                try:
                    next(gen)
                except StopIteration as done:
                    sides[x] = done.value
    for x, (_, _, qr_ref) in enumerate(side_in):
        for r in range(NSA_REP):
            q_ref[x, r, 0:HEAD_DIM, :] = qr_ref[0, r * HEAD_DIM:(r + 1) * HEAD_DIM, :]
            q_ref[x, r, HEAD_DIM:AUG, :] = sides[x][2]

    k_tile = lambda kv: ks_ref[0, pl.ds(pl.multiple_of(kv * TT, TT), TT), :]
    v_tile = lambda kv: vs_ref[0, kv]
    o_sel = _interleave([_paired_sweep(nq, qi, lambda x, r=r: q_ref[x, r], k_tile, v_tile,
                                       lambda lo, kv: None, s_refs[r], c_refs[r]) for r in range(NSA_REP)])

    for x, (g_ref, o_ref) in enumerate(((gl_ref, olo_ref), (gh_ref, ohi_ref))):
        o_cmp, o_win, _ = sides[x]
        g = g_ref[0, 0]
        rows = []
        for r in range(NSA_REP):
            gc = g[r * N_BRANCH + 0:r * N_BRANCH + 1]
            gs = g[r * N_BRANCH + 1:r * N_BRANCH + 2]
            gw = g[r * N_BRANCH + 2:r * N_BRANCH + 3]
            rows.append(gc * o_cmp[r] + gs * o_sel[r][x] + gw * o_win[r])
        o_ref[0] = jnp.concatenate(rows, axis=0).T.astype(BF16)


def _nsa_attn(batch, seq_len, qn, qr, kc, vct, ovt, ksa, vs, kwa, vw, gt):
    nq = seq_len // TT
    nc = kc.shape[1]
    n_sb = seq_len // SEL_LEN
    g_rows = NSA_REP * N_BRANCH
    half = nq // 2
    ksa3 = ksa.reshape(batch, seq_len, NSA_GROUPS * AUG)
    kwa3 = kwa.reshape(batch, seq_len, NSA_GROUPS * AUG)
    vs5 = vs.reshape(batch, nq, NSA_GROUPS, VROWS, TT)
    vw5 = vw.reshape(batch, nq, NSA_GROUPS, VROWS, TT)
    gt4 = gt.reshape(batch * nq, NSA_GROUPS, g_rows, TT)
    lo_tile = lambda b, q: b * nq + q
    hi_tile = lambda b, q: b * nq + nq - 1 - q
    q_lo = pl.BlockSpec((1, NSA_REP * HEAD_DIM, TT), lambda b, g, q: (lo_tile(b, q), g, 0))
    q_hi = pl.BlockSpec((1, NSA_REP * HEAD_DIM, TT), lambda b, g, q: (hi_tile(b, q), g, 0))
    kspec = pl.BlockSpec((1, seq_len, AUG), lambda b, g, q: (b, 0, g))
    vspec = pl.BlockSpec((1, nq, None, VROWS, TT), lambda b, g, q: (b, 0, g, 0, 0))
    out = jax.ShapeDtypeStruct((batch, seq_len // 2, HQ), BF16)
    return pl.pallas_call(
        functools.partial(_nsa_attn_body, nq, n_sb),
        grid=(batch, NSA_GROUPS, half),
        in_specs=[
            q_lo, q_hi, q_lo, q_hi,
            pl.BlockSpec((1, nc, HEAD_DIM), lambda b, g, q: (b * NSA_GROUPS + g, 0, 0)),
            pl.BlockSpec((1, HEAD_DIM, nc), lambda b, g, q: (b * NSA_GROUPS + g, 0, 0)),
            pl.BlockSpec(ovt.shape, lambda b, g, q: (0, 0)),
            kspec, vspec, kspec, vspec,
            pl.BlockSpec((1, 1, g_rows, TT), lambda b, g, q: (lo_tile(b, q), g, 0, 0)),
            pl.BlockSpec((1, 1, g_rows, TT), lambda b, g, q: (hi_tile(b, q), g, 0, 0)),
        ],
        out_specs=[pl.BlockSpec((1, TT, NSA_REP * HEAD_DIM), lambda b, g, q: (b, q, g)),
                   pl.BlockSpec((1, TT, NSA_REP * HEAD_DIM), lambda b, g, q: (b, half - 1 - q, g))],
        out_shape=[out, out],
        scratch_shapes=([pltpu.VMEM((2, NSA_REP, AUG, TT), BF16)]
                        + [pltpu.VMEM((nq + 1, TT, TT), F32)] * NSA_REP
                        + [pltpu.VMEM((half, OROWS, TT), F32)] * NSA_REP),
        compiler_params=_cparams(("parallel", "parallel", "arbitrary")),
        name="nsa_attn",
    )(qn, qn, qr, qr, kc, vct, ovt, ksa3, vs5, kwa3, vw5, gt4, gt4)


def _overlap_t(seq_len, nc):
    n_cmp = (seq_len - CMP_LEN) // CMP_STRIDE + 1
    n_sb = seq_len // SEL_LEN
    cmp_start = jnp.arange(n_cmp) * CMP_STRIDE
    sel_start = jnp.arange(n_sb) * SEL_LEN
    ov = jnp.clip(jnp.minimum(cmp_start[:, None] + CMP_LEN, sel_start[None, :] + SEL_LEN)
                  - jnp.maximum(cmp_start[:, None], sel_start[None, :]), 0, None).astype(F32) / CMP_LEN
    out = jnp.zeros((SEL_LEN, nc), F32).at[:n_sb, :n_cmp].set(ov.T)
    return out.astype(BF16)


def _nsa_attention(x, batch, seq_len, w_in, b_gate, pe, k_w1, k_b1, k_w2, v_w1, v_b1, v_w2, tables):
    wt, wn, bg = _nsa_inproj_weights(w_in, b_gate)
    qn, qr, vs, vw, gt, ksa, kwa, kc, vc = _nsa_inproj(x, seq_len, wt, wn, bg, *tables)

    nc = seq_len // CMP_STRIDE
    hw = CMP_STRIDE * HEAD_DIM

    def half_blocks(t):
        t = t.reshape(batch, seq_len, NSA_GROUPS, HEAD_DIM).transpose(0, 2, 1, 3)
        return t.reshape(batch * NSA_GROUPS, nc, hw)

    kcmp, vcmp_t = _compress(
        half_blocks(kc), half_blocks(vc), pe.reshape(2, hw),
        k_w1.reshape(2, hw, PHI_HIDDEN).astype(BF16), k_b1.reshape(1, -1), k_w2.astype(BF16),
        v_w1.reshape(2, hw, PHI_HIDDEN).transpose(0, 2, 1).astype(BF16), v_b1.reshape(-1, 1),
        v_w2.T.astype(BF16))
    return _nsa_attn(batch, seq_len, qn, qr, kcmp, vcmp_t, _overlap_t(seq_len, nc), ksa, vs, kwa, vw, gt)


def _nsa_layer(x, batch, seq_len, w_in, b_gate, pe, k_w1, k_b1, k_w2, v_w1, v_b1, v_w2, w_o,
               ln_g, ln_b, tables):
    a_lo, a_hi = _nsa_attention(x, batch, seq_len, w_in, b_gate, pe, k_w1, k_b1, k_w2, v_w1, v_b1, v_w2,
                                tables)
    return _proj_ln(a_lo, a_hi, w_o.astype(BF16), x, ln_g.reshape(1, -1), ln_b.reshape(1, -1))


_FOX_T_ROWS = 2 * HQ + N_HEADS
_FOX_N_COLS = N_HEADS * AUG + 128
_N_PIECE = 3


def _fox_inproj_body(seq_tiles, x_ref, wt_ref, wn_ref, bft_ref, bfn_ref, place_ref, ones_ref, route_ref,
                     qa_ref, ka_ref, vt_ref, off_ref, run_ref):
    i = pl.program_id(0)
    xb = x_ref[...].astype(BF16)
    t = _dot_nt(wt_ref[...], xb)
    n = _dot(xb, wn_ref[...])
    vt_ref[0, :, 0:HEAD_DIM, :] = t[HQ:2 * HQ].reshape(N_HEADS, HEAD_DIM, TT).astype(BF16)
    vt_ref[0, :, HEAD_DIM:VROWS, :] = jnp.ones((N_HEADS, VROWS - HEAD_DIM, TT), BF16)
    lf_t = _log_sigmoid(t[2 * HQ:] + bft_ref[...]) * LOG2E
    kw = N_HEADS * AUG
    lf_n = _log_sigmoid(n[:, kw:kw + _N_PIECE * N_HEADS] + bfn_ref[...]) * LOG2E

    r_i = lax.broadcasted_iota(jnp.int32, (TT, TT), 0)
    c_i = lax.broadcasted_iota(jnp.int32, (TT, TT), 1)
    upper = ((r_i > 0) & (r_i <= c_i)).astype(BF16)
    lower = ((c_i > 0) & (c_i <= r_i)).astype(BF16)
    a_t = sum(_dot(p, upper) for p in _split3(lf_t))
    a_n = sum(_dot(lower, p) for p in _split3(lf_n))

    q3 = (t[0:HQ] * QSCALE).reshape(N_HEADS, HEAD_DIM, TT)
    qa_ref[0, :, 0:HEAD_DIM, :] = q3.astype(BF16)
    stacked = jnp.concatenate(list(_split3(a_t)) + [jnp.ones((N_HEADS, TT), BF16)], axis=0)
    for h in range(N_HEADS):
        qa_ref[0, h, HEAD_DIM:AUG, :] = _dot(route_ref[h], stacked).astype(BF16)

    b1, b2, b3 = _split3(-a_n)
    grp = lax.broadcasted_iota(jnp.int32, (TT, _N_PIECE * N_HEADS), 1) // N_HEADS
    bsel = jnp.where(grp == 0, b1, jnp.where(grp == 1, b2, b3))
    ka_ref[...] = (n[:, 0:kw] + _dot(bsel, place_ref[...]) + ones_ref[...]).astype(BF16)

    @pl.when(i % seq_tiles == 0)
    def _():
        run_ref[...] = jnp.zeros_like(run_ref)

    first = lf_t[:, 0:1]
    off_ref[0] = jnp.broadcast_to(run_ref[:, 0:1] + first, (N_HEADS, TT))
    run_ref[...] = run_ref[...] + (a_t[:, TT - 1:TT] + first)


def _fox_inproj(x, seq_len, wt, wn, bft, bfn, place, ones, route):
    n = x.shape[0]
    nt = n // TT
    kw = N_HEADS * AUG
    full = lambda r, c: pl.BlockSpec((r, c), lambda i: (0, 0))
    return pl.pallas_call(
        functools.partial(_fox_inproj_body, seq_len // TT),
        grid=(nt,),
        in_specs=[
            pl.BlockSpec((TT, D_MODEL), lambda i: (i, 0)),
            full(_FOX_T_ROWS, D_MODEL), full(D_MODEL, _FOX_N_COLS),
            full(N_HEADS, 1), full(1, _N_PIECE * N_HEADS),
            full(_N_PIECE * N_HEADS, kw), full(1, kw),
            pl.BlockSpec((N_HEADS, AUG - HEAD_DIM, AUG - HEAD_DIM), lambda i: (0, 0, 0)),
        ],
        out_specs=[
            pl.BlockSpec((1, N_HEADS, AUG, TT), lambda i: (i, 0, 0, 0)),
            pl.BlockSpec((TT, kw), lambda i: (i, 0)),
            pl.BlockSpec((1, N_HEADS, VROWS, TT), lambda i: (i, 0, 0, 0)),
            pl.BlockSpec((1, N_HEADS, TT), lambda i: (i, 0, 0)),
        ],
        out_shape=[
            jax.ShapeDtypeStruct((nt, N_HEADS, AUG, TT), BF16),
            jax.ShapeDtypeStruct((n, kw), BF16),
            jax.ShapeDtypeStruct((nt, N_HEADS, VROWS, TT), BF16),
            jax.ShapeDtypeStruct((nt, N_HEADS, TT), F32),
        ],
        scratch_shapes=[pltpu.VMEM((N_HEADS, 128), F32)],
        compiler_params=_cparams(("arbitrary",)),
        name="fox_inproj",
    )(x, wt, wn, bft, bfn, place, ones, route)


def _fox_inproj_weights(w_in, b_f):
    wq, wk, wv, wf = jnp.split(w_in, [HQ, 2 * HQ, 3 * HQ], axis=1)
    wt = jnp.concatenate([wq, wv, wf], axis=1).T.astype(BF16)
    pad = jnp.zeros((D_MODEL, _FOX_N_COLS - N_HEADS * AUG - _N_PIECE * N_HEADS), F32)
    wn = jnp.concatenate([_pad_heads(wk, N_HEADS)] + [wf] * _N_PIECE + [pad], axis=1).astype(BF16)
    bft = b_f.reshape(-1, 1)
    bfn = jnp.tile(b_f, _N_PIECE).reshape(1, -1)
    rows = jnp.arange(_N_PIECE * N_HEADS)
    cols = (rows % N_HEADS) * AUG + HEAD_DIM + _N_PIECE + rows // N_HEADS
    place = jnp.zeros((_N_PIECE * N_HEADS, N_HEADS * AUG), F32).at[rows, cols].set(1.0).astype(BF16)
    lane = jnp.arange(N_HEADS * AUG) % AUG
    ones = ((lane >= HEAD_DIM) & (lane < HEAD_DIM + _N_PIECE)).astype(F32).reshape(1, -1)
    hh = jnp.arange(N_HEADS)
    route = jnp.zeros((N_HEADS, AUG - HEAD_DIM, AUG - HEAD_DIM), F32)
    for k in range(_N_PIECE):
        route = route.at[hh, k, k * N_HEADS + hh].set(1.0)
        route = route.at[hh, _N_PIECE + k, _N_PIECE * N_HEADS].set(1.0)
    return wt, wn, bft, bfn, place, ones, route.astype(BF16)


def _fox_attn_body(nq, qlo_ref, qhi_ref, ka_ref, vt_ref, off_ref, olo_ref, ohi_ref, q_ref, *scratch):
    hg = pl.program_id(1)
    qi = pl.program_id(2)
    s_refs, c_refs = scratch[:HEAD_BLOCK], scratch[HEAD_BLOCK:]
    q_ref[0] = qlo_ref[0]
    q_ref[1] = qhi_ref[0]
    sweeps = []
    for r in range(HEAD_BLOCK):
        h = hg * HEAD_BLOCK + r

        def off_row(t, h=h):
            return off_ref[0, pl.ds(t, 1), pl.ds(h, 1), :].reshape(1, TT)

        off_lo, off_hi = off_row(qi), off_row(nq - 1 - qi)

        def bias_row(lo, kv, off_lo=off_lo, off_hi=off_hi, off_row=off_row):
            base = off_hi if lo is False else jnp.where(lo, off_lo, off_hi)
            return base - off_row(kv)

        def k_tile(kv, r=r):
            return ka_ref[0, pl.ds(pl.multiple_of(kv * TT, TT), TT), r * AUG:(r + 1) * AUG]

        def v_tile(kv, r=r):
            return vt_ref[0, kv, r]

        sweeps.append(_paired_sweep(nq, qi, lambda x, r=r: q_ref[x, r], k_tile, v_tile, bias_row,
                                    s_refs[r], c_refs[r]))
    outs = _interleave(sweeps)
    for x, o_ref in enumerate((olo_ref, ohi_ref)):
        o_ref[0] = jnp.concatenate([o[x] for o in outs], axis=0).T.astype(BF16)


def _fox_attn(batch, seq_len, qa, ka, vt, off):
    nq = seq_len // TT
    half = nq // 2
    hb = HEAD_BLOCK
    ka3 = ka.reshape(batch, seq_len, N_HEADS * AUG)
    vt5 = vt.reshape(batch, nq, N_HEADS, VROWS, TT)
    off4 = off.reshape(batch, nq, N_HEADS, TT)
    out = jax.ShapeDtypeStruct((batch, seq_len // 2, HQ), BF16)
    return pl.pallas_call(
        functools.partial(_fox_attn_body, nq),
        grid=(batch, N_HEADS // hb, half),
        in_specs=[
            pl.BlockSpec((1, hb, AUG, TT), lambda b, h, q: (b * nq + q, h, 0, 0)),
            pl.BlockSpec((1, hb, AUG, TT), lambda b, h, q: (b * nq + nq - 1 - q, h, 0, 0)),
            pl.BlockSpec((1, seq_len, hb * AUG), lambda b, h, q: (b, 0, h)),
            pl.BlockSpec((1, nq, hb, VROWS, TT), lambda b, h, q: (b, 0, h, 0, 0)),
            pl.BlockSpec((1, nq, N_HEADS, TT), lambda b, h, q: (b, 0, 0, 0)),
        ],
        out_specs=[pl.BlockSpec((1, TT, hb * HEAD_DIM), lambda b, h, q: (b, q, h)),
                   pl.BlockSpec((1, TT, hb * HEAD_DIM), lambda b, h, q: (b, half - 1 - q, h))],
        out_shape=[out, out],
        scratch_shapes=([pltpu.VMEM((2, hb, AUG, TT), BF16)]
                        + [pltpu.VMEM((nq + 1, TT, TT), F32)] * hb
                        + [pltpu.VMEM((half, OROWS, TT), F32)] * hb),
        compiler_params=_cparams(("parallel", "parallel", "arbitrary")),
        name="fox_attn",
    )(qa, qa, ka3, vt5, off4)


def _fox_layer(x, batch, seq_len, w_in, b_f, w_o, ln_g, ln_b):
    qa, ka, vt, off = _fox_inproj(x, seq_len, *_fox_inproj_weights(w_in, b_f))
    a_lo, a_hi = _fox_attn(batch, seq_len, qa, ka, vt, off)
    return _proj_ln(a_lo, a_hi, w_o.astype(BF16), x, ln_g.reshape(1, -1), ln_b.reshape(1, -1))


def kernel(x, nsa_w_in, nsa_b_gate, nsa_pe, nsa_phik_w1, nsa_phik_b1, nsa_phik_w2, nsa_phiv_w1,
           nsa_phiv_b1, nsa_phiv_w2, nsa_w_o, fox_w_in, fox_b_f, fox_w_o, ffn_w_up, ffn_conv_w,
           ffn_conv_b, ffn_w_down, ln1_g, ln1_b, ln2_g, ln2_b):
    batch, seq_len, d = x.shape
    assert d == D_MODEL and seq_len % (2 * PROJ_TM) == 0 and seq_len % FFN_TM == 0 and seq_len % (2 * TT) == 0
    assert seq_len // SEL_LEN <= AUG - HEAD_DIM
    tables = _rope_tables(seq_len)
    h = x.reshape(batch * seq_len, d)
    for i in range(DEPTH):
        j = i // 2
        if i % 2 == 0:
            h = _nsa_layer(h, batch, seq_len, nsa_w_in[j], nsa_b_gate[j], nsa_pe[j], nsa_phik_w1[j],
                           nsa_phik_b1[j], nsa_phik_w2[j], nsa_phiv_w1[j], nsa_phiv_b1[j],
                           nsa_phiv_w2[j], nsa_w_o[j], ln1_g[i], ln1_b[i], tables)
        else:
            h = _fox_layer(h, batch, seq_len, fox_w_in[j], fox_b_f[j], fox_w_o[j], ln1_g[i], ln1_b[i])
        h = _ffn(h, seq_len, *_ffn_weights(ffn_w_up[i], ffn_conv_w[i], ffn_conv_b[i], ffn_w_down[i]),
                 ln2_g[i].reshape(1, -1), ln2_b[i].reshape(1, -1))
    return h.reshape(batch, seq_len, d)
```

```python
import functools
import math

import jax
import jax.numpy as jnp
from jax import lax
from jax.experimental import pallas as pl
from jax.experimental.pallas import tpu as pltpu

F32 = jnp.float32
BF16 = jnp.bfloat16

D_MODEL = 1024
DEPTH = 4
HEAD_DIM = 64
N_HEADS = 16
HQ = N_HEADS * HEAD_DIM
NSA_GROUPS = 4
NSA_REP = 4
NSA_KV = NSA_GROUPS * HEAD_DIM
CMP_LEN = 32
CMP_STRIDE = 16
SEL_LEN = 64
N_SEL = 16
WINDOW = 512
PHI_HIDDEN = 256
N_BRANCH = 3
D_FF = 2816
ROPE_THETA = 10000.0
ALPHA = (2 * DEPTH) ** 0.25
LN_EPS = 1e-5
SCALE = HEAD_DIM ** -0.5
LOG2E = math.log2(math.e)
QSCALE = SCALE * LOG2E
VROWS = 80
OROWS = 72

TT = 256
AUG = 128
MASK_BIAS = -1e30
FF_CHUNK = 256
FFN_TM = 256
PROJ_TM = 512
HEAD_BLOCK = 4
SWEEP_WIDTH = 1
HALO = 16
VMEM_LIMIT = 56 * 1024 * 1024


def _cparams(sem):
    return pltpu.CompilerParams(dimension_semantics=sem, vmem_limit_bytes=VMEM_LIMIT)


def _layer_norm(z, g, b):
    mu = jnp.mean(z, axis=-1, keepdims=True)
    zc = z - mu
    var = jnp.mean(zc * zc, axis=-1, keepdims=True)
    return zc * lax.rsqrt(var + LN_EPS) * g + b


def _gelu_tanh(x):
    c = math.sqrt(2.0 / math.pi)
    return x * (0.5 * (1.0 + jnp.tanh(c * (x + 0.044715 * (x * x * x)))))


def _log_sigmoid(z):
    return -(jnp.maximum(-z, 0.0) + jnp.log1p(jnp.exp(-jnp.abs(z))))


def _dot(a, b):
    return jnp.dot(a, b, preferred_element_type=F32)


def _dot_nt(a, b):
    return lax.dot_general(a, b, (((1,), (1,)), ((), ())), preferred_element_type=F32)


def _split3(x):
    p1 = x.astype(BF16)
    r1 = x - p1.astype(F32)
    p2 = r1.astype(BF16)
    p3 = (r1 - p2.astype(F32)).astype(BF16)
    return p1, p2, p3


def _proj_ln_body(half, lo_ref, hi_ref, w_ref, x_ref, g_ref, b_ref, o_ref):
    in_lo = (pl.program_id(0) % (2 * half)) < half
    a = jnp.where(in_lo, lo_ref[0], hi_ref[0])
    tm = a.shape[0]
    rows = [slice(r, r + tm // 4) for r in range(0, tm, tm // 4)]
    ys = [_dot(a[r], w_ref[...]) for r in rows]
    for r, y in zip(rows, ys):
        o_ref[r, :] = _layer_norm(ALPHA * x_ref[r, :] + y, g_ref[...], b_ref[...])


def _proj_ln(a_lo, a_hi, w, x, g, b):
    _, s_half, k = a_lo.shape
    n, d = x.shape
    tm = PROJ_TM
    half = s_half // tm
    return pl.pallas_call(
        functools.partial(_proj_ln_body, half),
        grid=(n // tm,),
        in_specs=[
            pl.BlockSpec((1, tm, k), lambda i: (i // (2 * half), jnp.minimum(i % (2 * half), half - 1), 0)),
            pl.BlockSpec((1, tm, k), lambda i: (i // (2 * half), jnp.maximum(i % (2 * half) - half, 0), 0)),
            pl.BlockSpec((k, d), lambda i: (0, 0)),
            pl.BlockSpec((tm, d), lambda i: (i, 0)),
            pl.BlockSpec((1, d), lambda i: (0, 0)),
            pl.BlockSpec((1, d), lambda i: (0, 0)),
        ],
        out_specs=pl.BlockSpec((tm, d), lambda i: (i, 0)),
        out_shape=jax.ShapeDtypeStruct((n, d), F32),
        compiler_params=_cparams(("parallel",)),
        name="proj_ln",
    )(a_lo, a_hi, w, x, g, b)


GELU_C0 = math.sqrt(2.0 / math.pi)
GELU_C1 = GELU_C0 * 0.044715


def _ffn_body(seq_tiles, x_ref, xp_ref, wa_ref, wb_ref, cwa_ref, cwb_ref, wd_ref, g_ref, b_ref, o_ref,
              perm_ref):
    i = pl.program_id(0)
    tm, d = x_ref.shape
    nj = tm // 8
    pitch = nj + 1
    n_lane = d // 128
    for s in range(8):
        for cb in range(n_lane):
            perm_ref[cb, pl.ds(s * pitch, nj), :] = x_ref[s * nj:(s + 1) * nj, cb * 128:(cb + 1) * 128]
    x = jnp.concatenate(
        [jnp.concatenate([perm_ref[cb, pl.ds(j, 8, stride=pitch), :] for cb in range(n_lane)], axis=1)
         for j in range(nj)], axis=0)
    starts_seq = (i % seq_tiles) == 0
    halo = jnp.where(starts_seq, 0.0, xp_ref[...]).astype(BF16)
    xcat = jnp.concatenate([halo, x.astype(BF16)], axis=0)
    n_chunks = wa_ref.shape[0]
    first_sublane = lax.broadcasted_iota(jnp.int32, (8, 1), 0) == 0

    def up(c):
        return _dot(xcat, wa_ref[c]), _dot(xcat, wb_ref[c])

    def conv(h, cw):
        h3 = h[HALO:].reshape(nj, 8, h.shape[-1])

        def wrap(slab, halo_row):
            return jnp.where(first_sublane, halo_row, pltpu.roll(slab, 1, axis=0))[None]

        prev1 = wrap(h3[nj - 1], h[HALO - 1:HALO])
        prev2 = wrap(h3[nj - 2], h[HALO - 2:HALO - 1])
        s1 = jnp.concatenate([prev1, h3[:-1]], axis=0)
        s2 = jnp.concatenate([prev2, prev1, h3[:-2]], axis=0)
        return (cw[0:1] * s2 + cw[1:2] * s1 + cw[2:3] * h3 + cw[3:4]).reshape(tm, h.shape[-1])

    y = None
    nxt = up(0)
    for c in range(n_chunks):
        cur, nxt = nxt, (up(c + 1) if c + 1 < n_chunks else None)
        ha = conv(cur[0], cwa_ref[c])
        act = ha + ha * jnp.tanh(ha * (GELU_C0 + GELU_C1 * (ha * ha)))
        hb = conv(cur[1], cwb_ref[c])
        gated = (act * hb).astype(BF16)
        part = _dot(gated, wd_ref[c])
        y = part if y is None else y + part
    out = _layer_norm(ALPHA * x + y, g_ref[...], b_ref[...])
    for j in range(nj):
        for cb in range(n_lane):
            perm_ref[cb, pl.ds(j, 8, stride=pitch), :] = out[j * 8:(j + 1) * 8, cb * 128:(cb + 1) * 128]
    for s in range(8):
        for cb in range(n_lane):
            o_ref[s * nj:(s + 1) * nj, cb * 128:(cb + 1) * 128] = perm_ref[cb, pl.ds(s * pitch, nj), :]


def _ffn(x, seq_len, wa, wb, cwa, cwb, wd, g, b):
    n, d = x.shape
    tm = FFN_TM
    nc = wa.shape[0]
    cf = wa.shape[2]
    hblk = tm // HALO
    return pl.pallas_call(
        functools.partial(_ffn_body, seq_len // tm),
        grid=(n // tm,),
        in_specs=[
            pl.BlockSpec((tm, d), lambda i: (i, 0)),
            pl.BlockSpec((HALO, d), lambda i: (jnp.maximum(i * hblk - 1, 0), 0)),
            pl.BlockSpec((nc, d, cf), lambda i: (0, 0, 0)),
            pl.BlockSpec((nc, d, cf), lambda i: (0, 0, 0)),
            pl.BlockSpec((nc, 4, cf), lambda i: (0, 0, 0)),
            pl.BlockSpec((nc, 4, cf), lambda i: (0, 0, 0)),
            pl.BlockSpec((nc, cf, d), lambda i: (0, 0, 0)),
            pl.BlockSpec((1, d), lambda i: (0, 0)),
            pl.BlockSpec((1, d), lambda i: (0, 0)),
        ],
        out_specs=pl.BlockSpec((tm, d), lambda i: (i, 0)),
        out_shape=jax.ShapeDtypeStruct((n, d), F32),
        scratch_shapes=[pltpu.VMEM((d // 128, tm + 8, 128), F32)],
        compiler_params=_cparams(("parallel",)),
        name="conv_ffn",
    )(x, x, wa, wb, cwa, cwb, wd, g, b)


def _ffn_weights(w_up, conv_w, conv_b, w_down):
    nc = D_FF // FF_CHUNK

    def up(w):
        return w.reshape(D_MODEL, nc, FF_CHUNK).transpose(1, 0, 2).astype(BF16)

    def taps(cw, cb):
        t = jnp.concatenate([cw, cb[None]], axis=0)
        return t.reshape(4, nc, FF_CHUNK).transpose(1, 0, 2)

    wa, wb = up(w_up[:, :D_FF]), up(w_up[:, D_FF:])
    cwa = taps(conv_w[:, :D_FF], conv_b[:D_FF])
    cwb = 0.5 * taps(conv_w[:, D_FF:], conv_b[D_FF:])
    wd = w_down.reshape(nc, FF_CHUNK, D_MODEL).astype(BF16)
    return wa, wb, cwa, cwb, wd


_NSA_T_ROWS = HQ + 2 * NSA_KV + N_BRANCH * N_HEADS
_NSA_N_COLS = 2 * NSA_KV + 2 * NSA_GROUPS * AUG


def _nsa_inproj_body(seq_tiles, x_ref, wt_ref, wn_ref, bg_ref, cost_ref, sint_ref, cosn_ref, sinn_ref,
                     qn_ref, qr_ref, vs_ref, vw_ref, gt_ref, ksa_ref, kwa_ref, kc_ref, vc_ref):
    i = pl.program_id(0)
    xb = x_ref[...].astype(BF16)
    t = _dot_nt(wt_ref[...], xb)
    n = _dot(xb, wn_ref[...])
    q = t[0:HQ] * QSCALE
    qn_ref[0] = q.astype(BF16)
    q3 = q.reshape(N_HEADS, HEAD_DIM, TT)
    half = HEAD_DIM // 2
    rot = jnp.concatenate([-q3[:, half:], q3[:, :half]], axis=1)
    qr = q3 * cost_ref[0][None] + rot * sint_ref[0][None]
    qr_ref[0] = qr.reshape(HQ, TT).astype(BF16)
    ones = jnp.ones((NSA_GROUPS, VROWS - HEAD_DIM, TT), BF16)
    for v_ref, lo in ((vs_ref, HQ), (vw_ref, HQ + NSA_KV)):
        v_ref[0, :, 0:HEAD_DIM, :] = t[lo:lo + NSA_KV].reshape(NSA_GROUPS, HEAD_DIM, TT).astype(BF16)
        v_ref[0, :, HEAD_DIM:VROWS, :] = ones
    gt_ref[0] = jax.nn.sigmoid(t[HQ + 2 * NSA_KV:] + bg_ref[...])

    kc_ref[...] = n[:, 0:NSA_KV]
    vc_ref[...] = n[:, NSA_KV:2 * NSA_KV]
    width = NSA_GROUPS * AUG
    lane = lax.broadcasted_iota(jnp.int32, (TT, width), 1) % AUG
    cosn, sinn = cosn_ref[...], sinn_ref[...]

    def rope_nat(k):
        rh = jnp.where(lane < half, -pltpu.roll(k, width - half, axis=1), pltpu.roll(k, half, axis=1))
        return k * cosn + rh * sinn

    ks = rope_nat(n[:, 2 * NSA_KV:2 * NSA_KV + width])
    kw = rope_nat(n[:, 2 * NSA_KV + width:])
    row = lax.broadcasted_iota(jnp.int32, (TT, width), 0)
    blk = ((i % seq_tiles) * TT + row) // SEL_LEN
    onehot = (lane - HEAD_DIM == blk).astype(F32)
    ksa_ref[...] = (ks + onehot).astype(BF16)
    kwa_ref[...] = kw.astype(BF16)


def _nsa_inproj(x, seq_len, wt, wn, bg, cost, sint, cosn, sinn):
    n = x.shape[0]
    nt = n // TT
    seq_tiles = seq_len // TT
    width = NSA_GROUPS * AUG
    full = lambda r, c: pl.BlockSpec((r, c), lambda i: (0, 0))
    tile3 = lambda r: pl.BlockSpec((1, r, TT), lambda i: (i, 0, 0))
    nat = lambda c: pl.BlockSpec((TT, c), lambda i: (i, 0))
    vtile = pl.BlockSpec((1, NSA_GROUPS, VROWS, TT), lambda i: (i, 0, 0, 0))
    return pl.pallas_call(
        functools.partial(_nsa_inproj_body, seq_tiles),
        grid=(nt,),
        in_specs=[
            nat(D_MODEL),
            full(_NSA_T_ROWS, D_MODEL),
            full(D_MODEL, _NSA_N_COLS),
            full(N_BRANCH * N_HEADS, 1),
            pl.BlockSpec((1, HEAD_DIM, TT), lambda i: (i % seq_tiles, 0, 0)),
            pl.BlockSpec((1, HEAD_DIM, TT), lambda i: (i % seq_tiles, 0, 0)),
            pl.BlockSpec((TT, width), lambda i: (i % seq_tiles, 0)),
            pl.BlockSpec((TT, width), lambda i: (i % seq_tiles, 0)),
        ],
        out_specs=[tile3(HQ), tile3(HQ), vtile, vtile, tile3(N_BRANCH * N_HEADS),
                   nat(width), nat(width), nat(NSA_KV), nat(NSA_KV)],
        out_shape=[
            jax.ShapeDtypeStruct((nt, HQ, TT), BF16),
            jax.ShapeDtypeStruct((nt, HQ, TT), BF16),
            jax.ShapeDtypeStruct((nt, NSA_GROUPS, VROWS, TT), BF16),
            jax.ShapeDtypeStruct((nt, NSA_GROUPS, VROWS, TT), BF16),
            jax.ShapeDtypeStruct((nt, N_BRANCH * N_HEADS, TT), F32),
            jax.ShapeDtypeStruct((n, width), BF16),
            jax.ShapeDtypeStruct((n, width), BF16),
            jax.ShapeDtypeStruct((n, NSA_KV), F32),
            jax.ShapeDtypeStruct((n, NSA_KV), F32),
        ],
        compiler_params=_cparams(("parallel",)),
        name="nsa_inproj",
    )(x, wt, wn, bg, cost, sint, cosn, sinn)


def _pad_heads(w, n_heads):
    w3 = w.reshape(w.shape[0], n_heads, HEAD_DIM)
    return jnp.concatenate([w3, jnp.zeros_like(w3)], axis=-1).reshape(w.shape[0], n_heads * AUG)


def _nsa_inproj_weights(w_in, b_gate):
    cuts = [HQ + i * NSA_KV for i in range(7)]
    wq, wkc, wvc, wks, wvs, wkw, wvw, wg = jnp.split(w_in, cuts, axis=1)
    wt = jnp.concatenate([wq, wvs, wvw, wg], axis=1).T.astype(BF16)
    wn = jnp.concatenate([wkc, wvc, _pad_heads(wks, NSA_GROUPS), _pad_heads(wkw, NSA_GROUPS)],
                         axis=1).astype(BF16)
    return wt, wn, b_gate.reshape(-1, 1)


def _rope_tables(seq_len):
    inv = ROPE_THETA ** (-jnp.arange(0, HEAD_DIM, 2, dtype=F32) / HEAD_DIM)
    ang = jnp.arange(seq_len, dtype=F32)[:, None] * inv[None, :]
    ang = jnp.concatenate([ang, ang], axis=-1)
    cos, sin = jnp.cos(ang), jnp.sin(ang)
    seq_tiles = seq_len // TT

    def transposed(t):
        return t.reshape(seq_tiles, TT, HEAD_DIM).transpose(0, 2, 1)

    def natural(t):
        return jnp.tile(jnp.concatenate([t, jnp.zeros_like(t)], axis=1), (1, NSA_GROUPS))

    return transposed(cos), transposed(sin), natural(cos), natural(sin)


def _compress_body(hk_ref, hv_ref, pe_ref, w1k_ref, b1k_ref, w2k_ref, w1vt_ref, b1v_ref, w2vt_ref,
                   kc_ref, vct_ref):
    nc = hk_ref.shape[1]
    pe_top, pe_bot = pe_ref[0:1], pe_ref[1:2]
    hk = hk_ref[0]
    top = _dot((hk + pe_top).astype(BF16), w1k_ref[0])
    bot = _dot((hk + pe_bot).astype(BF16), w1k_ref[1])
    hid = _gelu_tanh(top + pltpu.roll(bot, nc - 1, axis=0) + b1k_ref[...])
    kc_ref[0] = _dot(hid.astype(BF16), w2k_ref[...]).astype(BF16)

    hv = hv_ref[0]
    top_t = _dot_nt(w1vt_ref[0], (hv + pe_top).astype(BF16))
    bot_t = _dot_nt(w1vt_ref[1], (hv + pe_bot).astype(BF16))
    hid_t = _gelu_tanh(top_t + pltpu.roll(bot_t, nc - 1, axis=1) + b1v_ref[...])
    vct_ref[0] = _dot(w2vt_ref[...], hid_t.astype(BF16)).astype(BF16)


def _compress(hk, hv, pe2, w1k, b1k, w2k, w1vt, b1v, w2vt):
    bg, nc, hw = hk.shape
    cst = lambda shape: pl.BlockSpec(shape, lambda i: (0,) * len(shape))
    return pl.pallas_call(
        _compress_body,
        grid=(bg,),
        in_specs=[
            pl.BlockSpec((1, nc, hw), lambda i: (i, 0, 0)),
            pl.BlockSpec((1, nc, hw), lambda i: (i, 0, 0)),
            cst((2, hw)),
            cst((2, hw, PHI_HIDDEN)), cst((1, PHI_HIDDEN)), cst((PHI_HIDDEN, HEAD_DIM)),
            cst((2, PHI_HIDDEN, hw)), cst((PHI_HIDDEN, 1)), cst((HEAD_DIM, PHI_HIDDEN)),
        ],
        out_specs=[pl.BlockSpec((1, nc, HEAD_DIM), lambda i: (i, 0, 0)),
                   pl.BlockSpec((1, HEAD_DIM, nc), lambda i: (i, 0, 0))],
        out_shape=[jax.ShapeDtypeStruct((bg, nc, HEAD_DIM), BF16),
                   jax.ShapeDtypeStruct((bg, HEAD_DIM, nc), BF16)],
        compiler_params=_cparams(("parallel",)),
        name="nsa_compress",
    )(hk, hv, pe2, w1k, b1k, w2k, w1vt, b1v, w2vt)


def _slab_max(s):
    return jnp.max(s.reshape(-1, 8, s.shape[-1]), axis=0)


def _slab_min(s):
    return jnp.min(s.reshape(-1, 8, s.shape[-1]), axis=0)


def _slab_sum(p):
    return jnp.sum(p.reshape(-1, 8, p.shape[-1]), axis=0)


def _paired_sweep(nq, qi, q_of, k_tile, v_tile, bias_row, s_ref, c_ref):
    tpos = lax.broadcasted_iota(jnp.int32, (1, TT), 1)
    kpos = lax.broadcasted_iota(jnp.int32, (TT, 1), 0)
    causal = kpos <= tpos
    half = nq // 2
    slots = []
    for s in range(nq + 1):
        if s < half:
            lo = s <= qi
            slots.append((lo, jnp.where(lo, s, s - qi - 1), jnp.where(lo, 0, 1)))
        else:
            slots.append((False, s - qi - 1, 1))

    slot_max, bias = [], []
    for s, (lo, kv, x) in enumerate(slots):
        sc = _dot(k_tile(kv), q_of(x))
        if s == nq:
            sc = jnp.where(causal, sc, MASK_BIAS)
        s_ref[s] = sc
        b = bias_row(lo, kv)
        bias.append(b)
        slot_max.append(_slab_max(sc) if b is None else _slab_max(sc) + b)
        yield 'A'
    diag = jnp.where(causal, s_ref[qi], MASK_BIAS)
    s_ref[qi] = diag
    m_lo = _slab_max(diag)
    m_hi = slot_max[nq]
    for s in range(nq):
        if s < half:
            m_lo = jnp.maximum(m_lo, jnp.where(s < qi, slot_max[s], MASK_BIAS))
            m_hi = jnp.maximum(m_hi, jnp.where(s > qi, slot_max[s], MASK_BIAS))
        else:
            m_hi = jnp.maximum(m_hi, slot_max[s])
    m_lo = jnp.max(m_lo, axis=0, keepdims=True)
    m_hi = jnp.max(m_hi, axis=0, keepdims=True)
    yield 'M'

    acc_lo = jnp.zeros((OROWS, TT), F32)
    acc_hi = jnp.zeros((OROWS, TT), F32)
    for s, (lo, kv, x) in enumerate(slots):
        m_row = m_hi if lo is False else jnp.where(lo, m_lo, m_hi)
        if bias[s] is not None:
            m_row = m_row - bias[s]
        p = jnp.exp2(s_ref[s] - m_row)
        pv = _dot(v_tile(kv), p.astype(BF16))[0:OROWS]
        if lo is False:
            acc_hi = acc_hi + pv
        else:
            c_ref[s] = pv
        yield 'B'
    for s in range(half):
        lo, c = slots[s][0], c_ref[s]
        acc_lo = acc_lo + jnp.where(lo, c, 0.0)
        acc_hi = acc_hi + jnp.where(lo, 0.0, c)
    o_lo = acc_lo[0:HEAD_DIM] * (1.0 / acc_lo[HEAD_DIM:HEAD_DIM + 1])
    o_hi = acc_hi[0:HEAD_DIM] * (1.0 / acc_hi[HEAD_DIM:HEAD_DIM + 1])
    return o_lo, o_hi


def _interleave(sweeps, width=SWEEP_WIDTH):
    outs = [None] * len(sweeps)
    groups = [list(range(i, min(i + width, len(sweeps)))) for i in range(0, len(sweeps), width)]

    def step(group, until):
        done = True
        for i in group:
            if state[i] == until:
                continue
            try:
                state[i] = next(sweeps[i])
            except StopIteration as fin:
                outs[i] = fin.value
                state[i] = 'END'
            done = done and state[i] == until
        return done

    state = [None] * len(sweeps)
    while not step(groups[0], 'M'):
        pass
    for g, group in enumerate(groups):
        nxt = groups[g + 1] if g + 1 < len(groups) else []
        cur_done, nxt_done = False, not nxt
        while not (cur_done and nxt_done):
            if not nxt_done:
                nxt_done = step(nxt, 'M')
            if not cur_done:
                cur_done = step(group, 'END')
    return outs


def _nsa_side(n_sb, qt, clamp, x, qn_ref, qr_ref, kc_ref, vct_ref, ovt_ref, kw_ref, vw_ref, s_refs):
    t0 = qt * TT
    tpos = t0 + lax.broadcasted_iota(jnp.int32, (1, TT), 1)
    kpos0 = lax.broadcasted_iota(jnp.int32, (TT, 1), 0)
    nc = kc_ref.shape[1]

    kc = kc_ref[0]
    cmp_end = lax.broadcasted_iota(jnp.int32, (nc, 1), 0) * CMP_STRIDE + (CMP_LEN - 1)
    cmask = cmp_end <= tpos
    c_slots = [s_refs[r].at[6 + x, 0:nc] for r in range(NSA_REP)]
    maxima, inv_sums = [], []
    for r in range(NSA_REP):
        s = jnp.where(cmask, _dot(kc, qn_ref[0, r * HEAD_DIM:(r + 1) * HEAD_DIM, :]), -jnp.inf)
        c_slots[r][...] = s
        m = jnp.max(_slab_max(s), axis=0, keepdims=True)
        maxima.append(jnp.where(m == -jnp.inf, 0.0, m))
        yield
    for r in range(NSA_REP):
        e = jnp.exp2(c_slots[r][...] - maxima[r])
        c_slots[r][...] = e
        d = jnp.sum(_slab_sum(e), axis=0, keepdims=True)
        inv_sums.append(1.0 / jnp.where(d > 0, d, 1.0))
        yield
    psum = jnp.zeros((nc, TT), F32)
    o_cmp = []
    for r in range(NSA_REP):
        p = c_slots[r][...] * inv_sums[r]
        psum = psum + p
        o_cmp.append(_dot(vct_ref[0], p.astype(BF16)))
        yield
    p_hi = psum.astype(BF16)
    p_lo = (psum - p_hi.astype(F32)).astype(BF16)
    imp = _dot(ovt_ref[...], p_hi) + _dot(ovt_ref[...], p_lo)

    nrow = ovt_ref.shape[0]
    blk = lax.broadcasted_iota(jnp.int32, (nrow, 1), 0)
    cur = tpos // SEL_LEN
    forced = (blk == 0) | (blk == cur) | (blk == cur - 1)
    valid = blk * SEL_LEN <= tpos
    taken = -3.0
    score = jnp.where(forced, taken, jnp.where(valid, imp, -1.0))
    score = jnp.where(blk < n_sb, score, -2.0)
    blk_f = blk.astype(F32)

    j1 = jnp.maximum(qt - 1, 0) if clamp else qt - 1
    j2 = jnp.maximum(qt - 2, 0) if clamp else qt - 2
    in_window = tpos - (j2 * TT + kpos0) < WINDOW
    masks = (kpos0 <= tpos - t0,
             jnp.broadcast_to(qt >= 1, (TT, TT)) if clamp else None,
             ((qt >= 2) & in_window) if clamp else in_window)
    k_tiles = [kw_ref[0, pl.ds(pl.multiple_of(j * TT, TT), TT), :][:, 0:HEAD_DIM] for j in (qt, j1, j2)]
    v_tiles = [vw_ref[0, j] for j in (qt, j1, j2)]
    w_max, o_win = [], []

    def window_scores(r):
        q = qr_ref[0, r * HEAD_DIM:(r + 1) * HEAD_DIM, :]
        maxima = []
        for t, (k, mask) in enumerate(zip(k_tiles, masks)):
            s = _dot(k, q)
            s = s if mask is None else jnp.where(mask, s, MASK_BIAS)
            s_refs[r][3 * x + t] = s
            maxima.append(_slab_max(s))
        w_max.append(jnp.max(functools.reduce(jnp.maximum, maxima), axis=0, keepdims=True))

    def window_output(r):
        acc = sum(_dot(v, jnp.exp2(s_refs[r][3 * x + t] - w_max[r]).astype(BF16))[0:OROWS]
                  for t, v in enumerate(v_tiles))
        o_win.append(acc[0:HEAD_DIM] * (1.0 / acc[HEAD_DIM:HEAD_DIM + 1]))

    window_steps = ([functools.partial(window_scores, r) for r in range(NSA_REP)]
                    + [functools.partial(window_output, r) for r in range(NSA_REP)])
    n_round = max(min(N_SEL, n_sb) - 3, 0)
    per_step = max(n_round // len(window_steps), 1)

    def over_blocks(pair_op, v):
        slabs = v.reshape(-1, 8, TT)
        v = functools.reduce(pair_op, [slabs[i] for i in range(slabs.shape[0])])
        for shift in (4, 2, 1):
            v = pair_op(v, pltpu.roll(v, shift, axis=0))
        return jnp.tile(v, (nrow // 8, 1))

    for i in range(n_round):
        top = over_blocks(jnp.maximum, score)
        first = over_blocks(jnp.minimum, jnp.where(score == top, blk_f, float(nrow)))
        score = jnp.where(blk_f == first, taken, score)
        if i % per_step == per_step - 1 and window_steps:
            window_steps.pop(0)()
        yield
    while window_steps:
        window_steps.pop(0)()
    sel_bias = jnp.where(score == taken, 0.0, MASK_BIAS).astype(BF16)
    return o_cmp, o_win, sel_bias


def _nsa_attn_body(nq, n_sb, qnl_ref, qnh_ref, qrl_ref, qrh_ref, kc_ref, vct_ref, ovt_ref, ks_ref, vs_ref,
                   kw_ref, vw_ref, gl_ref, gh_ref, olo_ref, ohi_ref, q_ref, *scratch):
    qi = pl.program_id(2)
    s_refs, c_refs = scratch[:NSA_REP], scratch[NSA_REP:]
    side_in = ((qi, qnl_ref, qrl_ref), (nq - 1 - qi, qnh_ref, qrh_ref))
    gens = [_nsa_side(n_sb, qt, x == 0 or nq < 6, x, qn_ref, qr_ref, kc_ref, vct_ref, ovt_ref, kw_ref, vw_ref,
                      s_refs) for x, (qt, qn_ref, qr_ref) in enumerate(side_in)]
    sides = [None, None]
    while None in sides:
        for x, gen in enumerate(gens):
            if sides[x] is None:
                try:
                    next(gen)
                except StopIteration as done:
                    sides[x] = done.value
    for x, (_, _, qr_ref) in enumerate(side_in):
        for r in range(NSA_REP):
            q_ref[x, r, 0:HEAD_DIM, :] = qr_ref[0, r * HEAD_DIM:(r + 1) * HEAD_DIM, :]
            q_ref[x, r, HEAD_DIM:AUG, :] = sides[x][2]

    k_tile = lambda kv: ks_ref[0, pl.ds(pl.multiple_of(kv * TT, TT), TT), :]
    v_tile = lambda kv: vs_ref[0, kv]
    o_sel = _interleave([_paired_sweep(nq, qi, lambda x, r=r: q_ref[x, r], k_tile, v_tile,
                                       lambda lo, kv: None, s_refs[r], c_refs[r]) for r in range(NSA_REP)])

    for x, (g_ref, o_ref) in enumerate(((gl_ref, olo_ref), (gh_ref, ohi_ref))):
        o_cmp, o_win, _ = sides[x]
        g = g_ref[0, 0]
        rows = []
        for r in range(NSA_REP):
            gc = g[r * N_BRANCH + 0:r * N_BRANCH + 1]
            gs = g[r * N_BRANCH + 1:r * N_BRANCH + 2]
            gw = g[r * N_BRANCH + 2:r * N_BRANCH + 3]
            rows.append(gc * o_cmp[r] + gs * o_sel[r][x] + gw * o_win[r])
        o_ref[0] = jnp.concatenate(rows, axis=0).T.astype(BF16)


def _nsa_attn(batch, seq_len, qn, qr, kc, vct, ovt, ksa, vs, kwa, vw, gt):
    nq = seq_len // TT
    nc = kc.shape[1]
    n_sb = seq_len // SEL_LEN
    g_rows = NSA_REP * N_BRANCH
    half = nq // 2
    ksa3 = ksa.reshape(batch, seq_len, NSA_GROUPS * AUG)
    kwa3 = kwa.reshape(batch, seq_len, NSA_GROUPS * AUG)
    vs5 = vs.reshape(batch, nq, NSA_GROUPS, VROWS, TT)
    vw5 = vw.reshape(batch, nq, NSA_GROUPS, VROWS, TT)
    gt4 = gt.reshape(batch * nq, NSA_GROUPS, g_rows, TT)
    lo_tile = lambda b, q: b * nq + q
    hi_tile = lambda b, q: b * nq + nq - 1 - q
    q_lo = pl.BlockSpec((1, NSA_REP * HEAD_DIM, TT), lambda b, g, q: (lo_tile(b, q), g, 0))
    q_hi = pl.BlockSpec((1, NSA_REP * HEAD_DIM, TT), lambda b, g, q: (hi_tile(b, q), g, 0))
    kspec = pl.BlockSpec((1, seq_len, AUG), lambda b, g, q: (b, 0, g))
    vspec = pl.BlockSpec((1, nq, None, VROWS, TT), lambda b, g, q: (b, 0, g, 0, 0))
    out = jax.ShapeDtypeStruct((batch, seq_len // 2, HQ), BF16)
    return pl.pallas_call(
        functools.partial(_nsa_attn_body, nq, n_sb),
        grid=(batch, NSA_GROUPS, half),
        in_specs=[
            q_lo, q_hi, q_lo, q_hi,
            pl.BlockSpec((1, nc, HEAD_DIM), lambda b, g, q: (b * NSA_GROUPS + g, 0, 0)),
            pl.BlockSpec((1, HEAD_DIM, nc), lambda b, g, q: (b * NSA_GROUPS + g, 0, 0)),
            pl.BlockSpec(ovt.shape, lambda b, g, q: (0, 0)),
            kspec, vspec, kspec, vspec,
            pl.BlockSpec((1, 1, g_rows, TT), lambda b, g, q: (lo_tile(b, q), g, 0, 0)),
            pl.BlockSpec((1, 1, g_rows, TT), lambda b, g, q: (hi_tile(b, q), g, 0, 0)),
        ],
        out_specs=[pl.BlockSpec((1, TT, NSA_REP * HEAD_DIM), lambda b, g, q: (b, q, g)),
                   pl.BlockSpec((1, TT, NSA_REP * HEAD_DIM), lambda b, g, q: (b, half - 1 - q, g))],
        out_shape=[out, out],
        scratch_shapes=([pltpu.VMEM((2, NSA_REP, AUG, TT), BF16)]
                        + [pltpu.VMEM((nq + 1, TT, TT), F32)] * NSA_REP
                        + [pltpu.VMEM((half, OROWS, TT), F32)] * NSA_REP),
        compiler_params=_cparams(("parallel", "parallel", "arbitrary")),
        name="nsa_attn",
    )(qn, qn, qr, qr, kc, vct, ovt, ksa3, vs5, kwa3, vw5, gt4, gt4)


def _overlap_t(seq_len, nc):
    n_cmp = (seq_len - CMP_LEN) // CMP_STRIDE + 1
    n_sb = seq_len // SEL_LEN
    cmp_start = jnp.arange(n_cmp) * CMP_STRIDE
    sel_start = jnp.arange(n_sb) * SEL_LEN
    ov = jnp.clip(jnp.minimum(cmp_start[:, None] + CMP_LEN, sel_start[None, :] + SEL_LEN)
                  - jnp.maximum(cmp_start[:, None], sel_start[None, :]), 0, None).astype(F32) / CMP_LEN
    out = jnp.zeros((SEL_LEN, nc), F32).at[:n_sb, :n_cmp].set(ov.T)
    return out.astype(BF16)


def _nsa_attention(x, batch, seq_len, w_in, b_gate, pe, k_w1, k_b1, k_w2, v_w1, v_b1, v_w2, tables):
    wt, wn, bg = _nsa_inproj_weights(w_in, b_gate)
    qn, qr, vs, vw, gt, ksa, kwa, kc, vc = _nsa_inproj(x, seq_len, wt, wn, bg, *tables)

    nc = seq_len // CMP_STRIDE
    hw = CMP_STRIDE * HEAD_DIM

    def half_blocks(t):
        t = t.reshape(batch, seq_len, NSA_GROUPS, HEAD_DIM).transpose(0, 2, 1, 3)
        return t.reshape(batch * NSA_GROUPS, nc, hw)

    kcmp, vcmp_t = _compress(
        half_blocks(kc), half_blocks(vc), pe.reshape(2, hw),
        k_w1.reshape(2, hw, PHI_HIDDEN).astype(BF16), k_b1.reshape(1, -1), k_w2.astype(BF16),
        v_w1.reshape(2, hw, PHI_HIDDEN).transpose(0, 2, 1).astype(BF16), v_b1.reshape(-1, 1),
        v_w2.T.astype(BF16))
    return _nsa_attn(batch, seq_len, qn, qr, kcmp, vcmp_t, _overlap_t(seq_len, nc), ksa, vs, kwa, vw, gt)


def _nsa_layer(x, batch, seq_len, w_in, b_gate, pe, k_w1, k_b1, k_w2, v_w1, v_b1, v_w2, w_o,
               ln_g, ln_b, tables):
    a_lo, a_hi = _nsa_attention(x, batch, seq_len, w_in, b_gate, pe, k_w1, k_b1, k_w2, v_w1, v_b1, v_w2,
                                tables)
    return _proj_ln(a_lo, a_hi, w_o.astype(BF16), x, ln_g.reshape(1, -1), ln_b.reshape(1, -1))


_FOX_T_ROWS = 2 * HQ + N_HEADS
_FOX_N_COLS = N_HEADS * AUG + 128
_N_PIECE = 3


def _fox_inproj_body(seq_tiles, x_ref, wt_ref, wn_ref, bft_ref, bfn_ref, place_ref, ones_ref, route_ref,
                     qa_ref, ka_ref, vt_ref, off_ref, run_ref):
    i = pl.program_id(0)
    xb = x_ref[...].astype(BF16)
    t = _dot_nt(wt_ref[...], xb)
    n = _dot(xb, wn_ref[...])
    vt_ref[0, :, 0:HEAD_DIM, :] = t[HQ:2 * HQ].reshape(N_HEADS, HEAD_DIM, TT).astype(BF16)
    vt_ref[0, :, HEAD_DIM:VROWS, :] = jnp.ones((N_HEADS, VROWS - HEAD_DIM, TT), BF16)
    lf_t = _log_sigmoid(t[2 * HQ:] + bft_ref[...]) * LOG2E
    kw = N_HEADS * AUG
    lf_n = _log_sigmoid(n[:, kw:kw + _N_PIECE * N_HEADS] + bfn_ref[...]) * LOG2E

    r_i = lax.broadcasted_iota(jnp.int32, (TT, TT), 0)
    c_i = lax.broadcasted_iota(jnp.int32, (TT, TT), 1)
    upper = ((r_i > 0) & (r_i <= c_i)).astype(BF16)
    lower = ((c_i > 0) & (c_i <= r_i)).astype(BF16)
    a_t = sum(_dot(p, upper) for p in _split3(lf_t))
    a_n = sum(_dot(lower, p) for p in _split3(lf_n))

    q3 = (t[0:HQ] * QSCALE).reshape(N_HEADS, HEAD_DIM, TT)
    qa_ref[0, :, 0:HEAD_DIM, :] = q3.astype(BF16)
    stacked = jnp.concatenate(list(_split3(a_t)) + [jnp.ones((N_HEADS, TT), BF16)], axis=0)
    for h in range(N_HEADS):
        qa_ref[0, h, HEAD_DIM:AUG, :] = _dot(route_ref[h], stacked).astype(BF16)

    b1, b2, b3 = _split3(-a_n)
    grp = lax.broadcasted_iota(jnp.int32, (TT, _N_PIECE * N_HEADS), 1) // N_HEADS
    bsel = jnp.where(grp == 0, b1, jnp.where(grp == 1, b2, b3))
    ka_ref[...] = (n[:, 0:kw] + _dot(bsel, place_ref[...]) + ones_ref[...]).astype(BF16)

    @pl.when(i % seq_tiles == 0)
    def _():
        run_ref[...] = jnp.zeros_like(run_ref)

    first = lf_t[:, 0:1]
    off_ref[0] = jnp.broadcast_to(run_ref[:, 0:1] + first, (N_HEADS, TT))
    run_ref[...] = run_ref[...] + (a_t[:, TT - 1:TT] + first)


def _fox_inproj(x, seq_len, wt, wn, bft, bfn, place, ones, route):
    n = x.shape[0]
    nt = n // TT
    kw = N_HEADS * AUG
    full = lambda r, c: pl.BlockSpec((r, c), lambda i: (0, 0))
    return pl.pallas_call(
        functools.partial(_fox_inproj_body, seq_len // TT),
        grid=(nt,),
        in_specs=[
            pl.BlockSpec((TT, D_MODEL), lambda i: (i, 0)),
            full(_FOX_T_ROWS, D_MODEL), full(D_MODEL, _FOX_N_COLS),
            full(N_HEADS, 1), full(1, _N_PIECE * N_HEADS),
            full(_N_PIECE * N_HEADS, kw), full(1, kw),
            pl.BlockSpec((N_HEADS, AUG - HEAD_DIM, AUG - HEAD_DIM), lambda i: (0, 0, 0)),
        ],
        out_specs=[
            pl.BlockSpec((1, N_HEADS, AUG, TT), lambda i: (i, 0, 0, 0)),
            pl.BlockSpec((TT, kw), lambda i: (i, 0)),
            pl.BlockSpec((1, N_HEADS, VROWS, TT), lambda i: (i, 0, 0, 0)),
            pl.BlockSpec((1, N_HEADS, TT), lambda i: (i, 0, 0)),
        ],
        out_shape=[
            jax.ShapeDtypeStruct((nt, N_HEADS, AUG, TT), BF16),
            jax.ShapeDtypeStruct((n, kw), BF16),
            jax.ShapeDtypeStruct((nt, N_HEADS, VROWS, TT), BF16),
            jax.ShapeDtypeStruct((nt, N_HEADS, TT), F32),
        ],
        scratch_shapes=[pltpu.VMEM((N_HEADS, 128), F32)],
        compiler_params=_cparams(("arbitrary",)),
        name="fox_inproj",
    )(x, wt, wn, bft, bfn, place, ones, route)


def _fox_inproj_weights(w_in, b_f):
    wq, wk, wv, wf = jnp.split(w_in, [HQ, 2 * HQ, 3 * HQ], axis=1)
    wt = jnp.concatenate([wq, wv, wf], axis=1).T.astype(BF16)
    pad = jnp.zeros((D_MODEL, _FOX_N_COLS - N_HEADS * AUG - _N_PIECE * N_HEADS), F32)
    wn = jnp.concatenate([_pad_heads(wk, N_HEADS)] + [wf] * _N_PIECE + [pad], axis=1).astype(BF16)
    bft = b_f.reshape(-1, 1)
    bfn = jnp.tile(b_f, _N_PIECE).reshape(1, -1)
    rows = jnp.arange(_N_PIECE * N_HEADS)
    cols = (rows % N_HEADS) * AUG + HEAD_DIM + _N_PIECE + rows // N_HEADS
    place = jnp.zeros((_N_PIECE * N_HEADS, N_HEADS * AUG), F32).at[rows, cols].set(1.0).astype(BF16)
    lane = jnp.arange(N_HEADS * AUG) % AUG
    ones = ((lane >= HEAD_DIM) & (lane < HEAD_DIM + _N_PIECE)).astype(F32).reshape(1, -1)
    hh = jnp.arange(N_HEADS)
    route = jnp.zeros((N_HEADS, AUG - HEAD_DIM, AUG - HEAD_DIM), F32)
    for k in range(_N_PIECE):
        route = route.at[hh, k, k * N_HEADS + hh].set(1.0)
        route = route.at[hh, _N_PIECE + k, _N_PIECE * N_HEADS].set(1.0)
    return wt, wn, bft, bfn, place, ones, route.astype(BF16)


def _fox_attn_body(nq, qlo_ref, qhi_ref, ka_ref, vt_ref, off_ref, olo_ref, ohi_ref, q_ref, *scratch):
    hg = pl.program_id(1)
    qi = pl.program_id(2)
    s_refs, c_refs = scratch[:HEAD_BLOCK], scratch[HEAD_BLOCK:]
    q_ref[0] = qlo_ref[0]
    q_ref[1] = qhi_ref[0]
    sweeps = []
    for r in range(HEAD_BLOCK):
        h = hg * HEAD_BLOCK + r

        def off_row(t, h=h):
            return off_ref[0, pl.ds(t, 1), pl.ds(h, 1), :].reshape(1, TT)

        off_lo, off_hi = off_row(qi), off_row(nq - 1 - qi)

        def bias_row(lo, kv, off_lo=off_lo, off_hi=off_hi, off_row=off_row):
            base = off_hi if lo is False else jnp.where(lo, off_lo, off_hi)
            return base - off_row(kv)

        def k_tile(kv, r=r):
            return ka_ref[0, pl.ds(pl.multiple_of(kv * TT, TT), TT), r * AUG:(r + 1) * AUG]

        def v_tile(kv, r=r):
            return vt_ref[0, kv, r]

        sweeps.append(_paired_sweep(nq, qi, lambda x, r=r: q_ref[x, r], k_tile, v_tile, bias_row,
                                    s_refs[r], c_refs[r]))
    outs = _interleave(sweeps)
    for x, o_ref in enumerate((olo_ref, ohi_ref)):
        o_ref[0] = jnp.concatenate([o[x] for o in outs], axis=0).T.astype(BF16)


def _fox_attn(batch, seq_len, qa, ka, vt, off):
    nq = seq_len // TT
    half = nq // 2
    hb = HEAD_BLOCK
    ka3 = ka.reshape(batch, seq_len, N_HEADS * AUG)
    vt5 = vt.reshape(batch, nq, N_HEADS, VROWS, TT)
    off4 = off.reshape(batch, nq, N_HEADS, TT)
    out = jax.ShapeDtypeStruct((batch, seq_len // 2, HQ), BF16)
    return pl.pallas_call(
        functools.partial(_fox_attn_body, nq),
        grid=(batch, N_HEADS // hb, half),
        in_specs=[
            pl.BlockSpec((1, hb, AUG, TT), lambda b, h, q: (b * nq + q, h, 0, 0)),
            pl.BlockSpec((1, hb, AUG, TT), lambda b, h, q: (b * nq + nq - 1 - q, h, 0, 0)),
            pl.BlockSpec((1, seq_len, hb * AUG), lambda b, h, q: (b, 0, h)),
            pl.BlockSpec((1, nq, hb, VROWS, TT), lambda b, h, q: (b, 0, h, 0, 0)),
            pl.BlockSpec((1, nq, N_HEADS, TT), lambda b, h, q: (b, 0, 0, 0)),
        ],
        out_specs=[pl.BlockSpec((1, TT, hb * HEAD_DIM), lambda b, h, q: (b, q, h)),
                   pl.BlockSpec((1, TT, hb * HEAD_DIM), lambda b, h, q: (b, half - 1 - q, h))],
        out_shape=[out, out],
        scratch_shapes=([pltpu.VMEM((2, hb, AUG, TT), BF16)]
                        + [pltpu.VMEM((nq + 1, TT, TT), F32)] * hb
                        + [pltpu.VMEM((half, OROWS, TT), F32)] * hb),
        compiler_params=_cparams(("parallel", "parallel", "arbitrary")),
        name="fox_attn",
    )(qa, qa, ka3, vt5, off4)


def _fox_layer(x, batch, seq_len, w_in, b_f, w_o, ln_g, ln_b):
    qa, ka, vt, off = _fox_inproj(x, seq_len, *_fox_inproj_weights(w_in, b_f))
    a_lo, a_hi = _fox_attn(batch, seq_len, qa, ka, vt, off)
    return _proj_ln(a_lo, a_hi, w_o.astype(BF16), x, ln_g.reshape(1, -1), ln_b.reshape(1, -1))


def kernel(x, nsa_w_in, nsa_b_gate, nsa_pe, nsa_phik_w1, nsa_phik_b1, nsa_phik_w2, nsa_phiv_w1,
           nsa_phiv_b1, nsa_phiv_w2, nsa_w_o, fox_w_in, fox_b_f, fox_w_o, ffn_w_up, ffn_conv_w,
           ffn_conv_b, ffn_w_down, ln1_g, ln1_b, ln2_g, ln2_b):
    batch, seq_len, d = x.shape
    assert d == D_MODEL and seq_len % (2 * PROJ_TM) == 0 and seq_len % FFN_TM == 0 and seq_len % (2 * TT) == 0
    assert seq_len // SEL_LEN <= AUG - HEAD_DIM
    tables = _rope_tables(seq_len)
    h = x.reshape(batch * seq_len, d)
    for i in range(DEPTH):
        j = i // 2
        if i % 2 == 0:
            h = _nsa_layer(h, batch, seq_len, nsa_w_in[j], nsa_b_gate[j], nsa_pe[j], nsa_phik_w1[j],
                           nsa_phik_b1[j], nsa_phik_w2[j], nsa_phiv_w1[j], nsa_phiv_b1[j],
                           nsa_phiv_w2[j], nsa_w_o[j], ln1_g[i], ln1_b[i], tables)
        else:
            h = _fox_layer(h, batch, seq_len, fox_w_in[j], fox_b_f[j], fox_w_o[j], ln1_g[i], ln1_b[i])
        h = _ffn(h, seq_len, *_ffn_weights(ffn_w_up[i], ffn_conv_w[i], ffn_conv_b[i], ffn_w_down[i]),
                 ln2_g[i].reshape(1, -1), ln2_b[i].reshape(1, -1))
    return h.reshape(batch, seq_len, d)
```

```python
import functools
import math

import jax
import jax.numpy as jnp
from jax import lax
from jax.experimental import pallas as pl
from jax.experimental.pallas import tpu as pltpu

F32 = jnp.float32
BF16 = jnp.bfloat16

D_MODEL = 1024
DEPTH = 4
HEAD_DIM = 64
N_HEADS = 16
HQ = N_HEADS * HEAD_DIM
NSA_GROUPS = 4
NSA_REP = 4
NSA_KV = NSA_GROUPS * HEAD_DIM
CMP_LEN = 32
CMP_STRIDE = 16
SEL_LEN = 64
N_SEL = 16
WINDOW = 512
PHI_HIDDEN = 256
N_BRANCH = 3
D_FF = 2816
ROPE_THETA = 10000.0
ALPHA = (2 * DEPTH) ** 0.25
LN_EPS = 1e-5
SCALE = HEAD_DIM ** -0.5
LOG2E = math.log2(math.e)
QSCALE = SCALE * LOG2E
VROWS = 80
OROWS = 72

TT = 256
AUG = 128
MASK_BIAS = -1e30
FF_CHUNK = 256
FFN_TM = 256
PROJ_TM = 512
HEAD_BLOCK = 4
SWEEP_WIDTH = 1
HALO = 16
VMEM_LIMIT = 56 * 1024 * 1024


def _cparams(sem):
    return pltpu.CompilerParams(dimension_semantics=sem, vmem_limit_bytes=VMEM_LIMIT)


def _layer_norm(z, g, b):
    mu = jnp.mean(z, axis=-1, keepdims=True)
    zc = z - mu
    var = jnp.mean(zc * zc, axis=-1, keepdims=True)
    return zc * lax.rsqrt(var + LN_EPS) * g + b


def _gelu_tanh(x):
    c = math.sqrt(2.0 / math.pi)
    return x * (0.5 * (1.0 + jnp.tanh(c * (x + 0.044715 * (x * x * x)))))


def _log_sigmoid(z):
    return -(jnp.maximum(-z, 0.0) + jnp.log1p(jnp.exp(-jnp.abs(z))))


def _dot(a, b):
    return jnp.dot(a, b, preferred_element_type=F32)


def _dot_nt(a, b):
    return lax.dot_general(a, b, (((1,), (1,)), ((), ())), preferred_element_type=F32)


def _split3(x):
    p1 = x.astype(BF16)
    r1 = x - p1.astype(F32)
    p2 = r1.astype(BF16)
    p3 = (r1 - p2.astype(F32)).astype(BF16)
    return p1, p2, p3


def _proj_ln_body(half, lo_ref, hi_ref, w_ref, x_ref, g_ref, b_ref, o_ref):
    in_lo = (pl.program_id(0) % (2 * half)) < half
    a = jnp.where(in_lo, lo_ref[0], hi_ref[0])
    tm = a.shape[0]
    rows = [slice(r, r + tm // 4) for r in range(0, tm, tm // 4)]
    ys = [_dot(a[r], w_ref[...]) for r in rows]
    for r, y in zip(rows, ys):
        o_ref[r, :] = _layer_norm(ALPHA * x_ref[r, :] + y, g_ref[...], b_ref[...])


def _proj_ln(a_lo, a_hi, w, x, g, b):
    _, s_half, k = a_lo.shape
    n, d = x.shape
    tm = PROJ_TM
    half = s_half // tm
    return pl.pallas_call(
        functools.partial(_proj_ln_body, half),
        grid=(n // tm,),
        in_specs=[
            pl.BlockSpec((1, tm, k), lambda i: (i // (2 * half), jnp.minimum(i % (2 * half), half - 1), 0)),
            pl.BlockSpec((1, tm, k), lambda i: (i // (2 * half), jnp.maximum(i % (2 * half) - half, 0), 0)),
            pl.BlockSpec((k, d), lambda i: (0, 0)),
            pl.BlockSpec((tm, d), lambda i: (i, 0)),
            pl.BlockSpec((1, d), lambda i: (0, 0)),
            pl.BlockSpec((1, d), lambda i: (0, 0)),
        ],
        out_specs=pl.BlockSpec((tm, d), lambda i: (i, 0)),
        out_shape=jax.ShapeDtypeStruct((n, d), F32),
        compiler_params=_cparams(("parallel",)),
        name="proj_ln",
    )(a_lo, a_hi, w, x, g, b)


GELU_C0 = math.sqrt(2.0 / math.pi)
GELU_C1 = GELU_C0 * 0.044715


def _ffn_body(seq_tiles, x_ref, xp_ref, wa_ref, wb_ref, cwa_ref, cwb_ref, wd_ref, g_ref, b_ref, o_ref,
              perm_ref):
    i = pl.program_id(0)
    tm, d = x_ref.shape
    nj = tm // 8
    pitch = nj + 1
    n_lane = d // 128
    for s in range(8):
        for cb in range(n_lane):
            perm_ref[cb, pl.ds(s * pitch, nj), :] = x_ref[s * nj:(s + 1) * nj, cb * 128:(cb + 1) * 128]
    x = jnp.concatenate(
        [jnp.concatenate([perm_ref[cb, pl.ds(j, 8, stride=pitch), :] for cb in range(n_lane)], axis=1)
         for j in range(nj)], axis=0)
    starts_seq = (i % seq_tiles) == 0
    halo = jnp.where(starts_seq, 0.0, xp_ref[...]).astype(BF16)
    xcat = jnp.concatenate([halo, x.astype(BF16)], axis=0)
    n_chunks = wa_ref.shape[0]
    first_sublane = lax.broadcasted_iota(jnp.int32, (8, 1), 0) == 0

    def up(c):
        return _dot(xcat, wa_ref[c]), _dot(xcat, wb_ref[c])

    def conv(h, cw):
        h3 = h[HALO:].reshape(nj, 8, h.shape[-1])

        def wrap(slab, halo_row):
            return jnp.where(first_sublane, halo_row, pltpu.roll(slab, 1, axis=0))[None]

        prev1 = wrap(h3[nj - 1], h[HALO - 1:HALO])
        prev2 = wrap(h3[nj - 2], h[HALO - 2:HALO - 1])
        s1 = jnp.concatenate([prev1, h3[:-1]], axis=0)
        s2 = jnp.concatenate([prev2, prev1, h3[:-2]], axis=0)
        return (cw[0:1] * s2 + cw[1:2] * s1 + cw[2:3] * h3 + cw[3:4]).reshape(tm, h.shape[-1])

    y = None
    nxt = up(0)
    for c in range(n_chunks):
        cur, nxt = nxt, (up(c + 1) if c + 1 < n_chunks else None)
        ha = conv(cur[0], cwa_ref[c])
        act = ha + ha * jnp.tanh(ha * (GELU_C0 + GELU_C1 * (ha * ha)))
        hb = conv(cur[1], cwb_ref[c])
        gated = (act * hb).astype(BF16)
        part = _dot(gated, wd_ref[c])
        y = part if y is None else y + part
    out = _layer_norm(ALPHA * x + y, g_ref[...], b_ref[...])
    for j in range(nj):
        for cb in range(n_lane):
            perm_ref[cb, pl.ds(j, 8, stride=pitch), :] = out[j * 8:(j + 1) * 8, cb * 128:(cb + 1) * 128]
    for s in range(8):
        for cb in range(n_lane):
            o_ref[s * nj:(s + 1) * nj, cb * 128:(cb + 1) * 128] = perm_ref[cb, pl.ds(s * pitch, nj), :]


def _ffn(x, seq_len, wa, wb, cwa, cwb, wd, g, b):
    n, d = x.shape
    tm = FFN_TM
    nc = wa.shape[0]
    cf = wa.shape[2]
    hblk = tm // HALO
    return pl.pallas_call(
        functools.partial(_ffn_body, seq_len // tm),
        grid=(n // tm,),
        in_specs=[
            pl.BlockSpec((tm, d), lambda i: (i, 0)),
            pl.BlockSpec((HALO, d), lambda i: (jnp.maximum(i * hblk - 1, 0), 0)),
            pl.BlockSpec((nc, d, cf), lambda i: (0, 0, 0)),
            pl.BlockSpec((nc, d, cf), lambda i: (0, 0, 0)),
            pl.BlockSpec((nc, 4, cf), lambda i: (0, 0, 0)),
            pl.BlockSpec((nc, 4, cf), lambda i: (0, 0, 0)),
            pl.BlockSpec((nc, cf, d), lambda i: (0, 0, 0)),
            pl.BlockSpec((1, d), lambda i: (0, 0)),
            pl.BlockSpec((1, d), lambda i: (0, 0)),
        ],
        out_specs=pl.BlockSpec((tm, d), lambda i: (i, 0)),
        out_shape=jax.ShapeDtypeStruct((n, d), F32),
        scratch_shapes=[pltpu.VMEM((d // 128, tm + 8, 128), F32)],
        compiler_params=_cparams(("parallel",)),
        name="conv_ffn",
    )(x, x, wa, wb, cwa, cwb, wd, g, b)


def _ffn_weights(w_up, conv_w, conv_b, w_down):
    nc = D_FF // FF_CHUNK

    def up(w):
        return w.reshape(D_MODEL, nc, FF_CHUNK).transpose(1, 0, 2).astype(BF16)

    def taps(cw, cb):
        t = jnp.concatenate([cw, cb[None]], axis=0)
        return t.reshape(4, nc, FF_CHUNK).transpose(1, 0, 2)

    wa, wb = up(w_up[:, :D_FF]), up(w_up[:, D_FF:])
    cwa = taps(conv_w[:, :D_FF], conv_b[:D_FF])
    cwb = 0.5 * taps(conv_w[:, D_FF:], conv_b[D_FF:])
    wd = w_down.reshape(nc, FF_CHUNK, D_MODEL).astype(BF16)
    return wa, wb, cwa, cwb, wd


_NSA_T_ROWS = HQ + 2 * NSA_KV + N_BRANCH * N_HEADS
_NSA_N_COLS = 2 * NSA_KV + 2 * NSA_GROUPS * AUG


def _nsa_inproj_body(seq_tiles, x_ref, wt_ref, wn_ref, bg_ref, cost_ref, sint_ref, cosn_ref, sinn_ref,
                     qn_ref, qr_ref, vs_ref, vw_ref, gt_ref, ksa_ref, kwa_ref, kc_ref, vc_ref):
    i = pl.program_id(0)
    xb = x_ref[...].astype(BF16)
    t = _dot_nt(wt_ref[...], xb)
    n = _dot(xb, wn_ref[...])
    q = t[0:HQ] * QSCALE
    qn_ref[0] = q.astype(BF16)
    q3 = q.reshape(N_HEADS, HEAD_DIM, TT)
    half = HEAD_DIM // 2
    rot = jnp.concatenate([-q3[:, half:], q3[:, :half]], axis=1)
    qr = q3 * cost_ref[0][None] + rot * sint_ref[0][None]
    qr_ref[0] = qr.reshape(HQ, TT).astype(BF16)
    ones = jnp.ones((NSA_GROUPS, VROWS - HEAD_DIM, TT), BF16)
    for v_ref, lo in ((vs_ref, HQ), (vw_ref, HQ + NSA_KV)):
        v_ref[0, :, 0:HEAD_DIM, :] = t[lo:lo + NSA_KV].reshape(NSA_GROUPS, HEAD_DIM, TT).astype(BF16)
        v_ref[0, :, HEAD_DIM:VROWS, :] = ones
    gt_ref[0] = jax.nn.sigmoid(t[HQ + 2 * NSA_KV:] + bg_ref[...])

    kc_ref[...] = n[:, 0:NSA_KV]
    vc_ref[...] = n[:, NSA_KV:2 * NSA_KV]
    width = NSA_GROUPS * AUG
    lane = lax.broadcasted_iota(jnp.int32, (TT, width), 1) % AUG
    cosn, sinn = cosn_ref[...], sinn_ref[...]

    def rope_nat(k):
        rh = jnp.where(lane < half, -pltpu.roll(k, width - half, axis=1), pltpu.roll(k, half, axis=1))
        return k * cosn + rh * sinn

    ks = rope_nat(n[:, 2 * NSA_KV:2 * NSA_KV + width])
    kw = rope_nat(n[:, 2 * NSA_KV + width:])
    row = lax.broadcasted_iota(jnp.int32, (TT, width), 0)
    blk = ((i % seq_tiles) * TT + row) // SEL_LEN
    onehot = (lane - HEAD_DIM == blk).astype(F32)
    ksa_ref[...] = (ks + onehot).astype(BF16)
    kwa_ref[...] = kw.astype(BF16)


def _nsa_inproj(x, seq_len, wt, wn, bg, cost, sint, cosn, sinn):
    n = x.shape[0]
    nt = n // TT
    seq_tiles = seq_len // TT
    width = NSA_GROUPS * AUG
    full = lambda r, c: pl.BlockSpec((r, c), lambda i: (0, 0))
    tile3 = lambda r: pl.BlockSpec((1, r, TT), lambda i: (i, 0, 0))
    nat = lambda c: pl.BlockSpec((TT, c), lambda i: (i, 0))
    vtile = pl.BlockSpec((1, NSA_GROUPS, VROWS, TT), lambda i: (i, 0, 0, 0))
    return pl.pallas_call(
        functools.partial(_nsa_inproj_body, seq_tiles),
        grid=(nt,),
        in_specs=[
            nat(D_MODEL),
            full(_NSA_T_ROWS, D_MODEL),
            full(D_MODEL, _NSA_N_COLS),
            full(N_BRANCH * N_HEADS, 1),
            pl.BlockSpec((1, HEAD_DIM, TT), lambda i: (i % seq_tiles, 0, 0)),
            pl.BlockSpec((1, HEAD_DIM, TT), lambda i: (i % seq_tiles, 0, 0)),
            pl.BlockSpec((TT, width), lambda i: (i % seq_tiles, 0)),
            pl.BlockSpec((TT, width), lambda i: (i % seq_tiles, 0)),
        ],
        out_specs=[tile3(HQ), tile3(HQ), vtile, vtile, tile3(N_BRANCH * N_HEADS),
                   nat(width), nat(width), nat(NSA_KV), nat(NSA_KV)],
        out_shape=[
            jax.ShapeDtypeStruct((nt, HQ, TT), BF16),
            jax.ShapeDtypeStruct((nt, HQ, TT), BF16),
            jax.ShapeDtypeStruct((nt, NSA_GROUPS, VROWS, TT), BF16),
            jax.ShapeDtypeStruct((nt, NSA_GROUPS, VROWS, TT), BF16),
            jax.ShapeDtypeStruct((nt, N_BRANCH * N_HEADS, TT), F32),
            jax.ShapeDtypeStruct((n, width), BF16),
            jax.ShapeDtypeStruct((n, width), BF16),
            jax.ShapeDtypeStruct((n, NSA_KV), F32),
            jax.ShapeDtypeStruct((n, NSA_KV), F32),
        ],
        compiler_params=_cparams(("parallel",)),
        name="nsa_inproj",
    )(x, wt, wn, bg, cost, sint, cosn, sinn)


def _pad_heads(w, n_heads):
    w3 = w.reshape(w.shape[0], n_heads, HEAD_DIM)
    return jnp.concatenate([w3, jnp.zeros_like(w3)], axis=-1).reshape(w.shape[0], n_heads * AUG)


def _nsa_inproj_weights(w_in, b_gate):
    cuts = [HQ + i * NSA_KV for i in range(7)]
    wq, wkc, wvc, wks, wvs, wkw, wvw, wg = jnp.split(w_in, cuts, axis=1)
    wt = jnp.concatenate([wq, wvs, wvw, wg], axis=1).T.astype(BF16)
    wn = jnp.concatenate([wkc, wvc, _pad_heads(wks, NSA_GROUPS), _pad_heads(wkw, NSA_GROUPS)],
                         axis=1).astype(BF16)
    return wt, wn, b_gate.reshape(-1, 1)


def _rope_tables(seq_len):
    inv = ROPE_THETA ** (-jnp.arange(0, HEAD_DIM, 2, dtype=F32) / HEAD_DIM)
    ang = jnp.arange(seq_len, dtype=F32)[:, None] * inv[None, :]
    ang = jnp.concatenate([ang, ang], axis=-1)
    cos, sin = jnp.cos(ang), jnp.sin(ang)
    seq_tiles = seq_len // TT

    def transposed(t):
        return t.reshape(seq_tiles, TT, HEAD_DIM).transpose(0, 2, 1)

    def natural(t):
        return jnp.tile(jnp.concatenate([t, jnp.zeros_like(t)], axis=1), (1, NSA_GROUPS))

    return transposed(cos), transposed(sin), natural(cos), natural(sin)


def _compress_body(hk_ref, hv_ref, pe_ref, w1k_ref, b1k_ref, w2k_ref, w1vt_ref, b1v_ref, w2vt_ref,
                   kc_ref, vct_ref):
    nc = hk_ref.shape[1]
    pe_top, pe_bot = pe_ref[0:1], pe_ref[1:2]
    hk = hk_ref[0]
    top = _dot((hk + pe_top).astype(BF16), w1k_ref[0])
    bot = _dot((hk + pe_bot).astype(BF16), w1k_ref[1])
    hid = _gelu_tanh(top + pltpu.roll(bot, nc - 1, axis=0) + b1k_ref[...])
    kc_ref[0] = _dot(hid.astype(BF16), w2k_ref[...]).astype(BF16)

    hv = hv_ref[0]
    top_t = _dot_nt(w1vt_ref[0], (hv + pe_top).astype(BF16))
    bot_t = _dot_nt(w1vt_ref[1], (hv + pe_bot).astype(BF16))
    hid_t = _gelu_tanh(top_t + pltpu.roll(bot_t, nc - 1, axis=1) + b1v_ref[...])
    vct_ref[0] = _dot(w2vt_ref[...], hid_t.astype(BF16)).astype(BF16)


def _compress(hk, hv, pe2, w1k, b1k, w2k, w1vt, b1v, w2vt):
    bg, nc, hw = hk.shape
    cst = lambda shape: pl.BlockSpec(shape, lambda i: (0,) * len(shape))
    return pl.pallas_call(
        _compress_body,
        grid=(bg,),
        in_specs=[
            pl.BlockSpec((1, nc, hw), lambda i: (i, 0, 0)),
            pl.BlockSpec((1, nc, hw), lambda i: (i, 0, 0)),
            cst((2, hw)),
            cst((2, hw, PHI_HIDDEN)), cst((1, PHI_HIDDEN)), cst((PHI_HIDDEN, HEAD_DIM)),
            cst((2, PHI_HIDDEN, hw)), cst((PHI_HIDDEN, 1)), cst((HEAD_DIM, PHI_HIDDEN)),
        ],
        out_specs=[pl.BlockSpec((1, nc, HEAD_DIM), lambda i: (i, 0, 0)),
                   pl.BlockSpec((1, HEAD_DIM, nc), lambda i: (i, 0, 0))],
        out_shape=[jax.ShapeDtypeStruct((bg, nc, HEAD_DIM), BF16),
                   jax.ShapeDtypeStruct((bg, HEAD_DIM, nc), BF16)],
        compiler_params=_cparams(("parallel",)),
        name="nsa_compress",
    )(hk, hv, pe2, w1k, b1k, w2k, w1vt, b1v, w2vt)


def _slab_max(s):
    return jnp.max(s.reshape(-1, 8, s.shape[-1]), axis=0)


def _slab_sum(p):
    return jnp.sum(p.reshape(-1, 8, p.shape[-1]), axis=0)


def _paired_sweep(nq, qi, q_of, k_tile, v_tile, bias_row, s_ref, c_ref):
    tpos = lax.broadcasted_iota(jnp.int32, (1, TT), 1)
    kpos = lax.broadcasted_iota(jnp.int32, (TT, 1), 0)
    causal = kpos <= tpos
    half = nq // 2
    slots = []
    for s in range(nq + 1):
        if s < half:
            lo = s <= qi
            slots.append((lo, jnp.where(lo, s, s - qi - 1), jnp.where(lo, 0, 1)))
        else:
            slots.append((False, s - qi - 1, 1))

    slot_max, bias = [], []
    for s, (lo, kv, x) in enumerate(slots):
        sc = _dot(k_tile(kv), q_of(x))
        if s == nq:
            sc = jnp.where(causal, sc, MASK_BIAS)
        s_ref[s] = sc
        b = bias_row(lo, kv)
        bias.append(b)
        slot_max.append(_slab_max(sc) if b is None else _slab_max(sc) + b)
        yield 'A'
    diag = jnp.where(causal, s_ref[qi], MASK_BIAS)
    s_ref[qi] = diag
    m_lo = _slab_max(diag)
    m_hi = slot_max[nq]
    for s in range(nq):
        if s < half:
            m_lo = jnp.maximum(m_lo, jnp.where(s < qi, slot_max[s], MASK_BIAS))
            m_hi = jnp.maximum(m_hi, jnp.where(s > qi, slot_max[s], MASK_BIAS))
        else:
            m_hi = jnp.maximum(m_hi, slot_max[s])
    m_lo = jnp.max(m_lo, axis=0, keepdims=True)
    m_hi = jnp.max(m_hi, axis=0, keepdims=True)
    yield 'M'

    acc_lo = jnp.zeros((OROWS, TT), F32)
    acc_hi = jnp.zeros((OROWS, TT), F32)
    for s, (lo, kv, x) in enumerate(slots):
        m_row = m_hi if lo is False else jnp.where(lo, m_lo, m_hi)
        if bias[s] is not None:
            m_row = m_row - bias[s]
        p = jnp.exp2(s_ref[s] - m_row)
        pv = _dot(v_tile(kv), p.astype(BF16))[0:OROWS]
        if lo is False:
            acc_hi = acc_hi + pv
        else:
            c_ref[s] = pv
        yield 'B'
    for s in range(half):
        lo, c = slots[s][0], c_ref[s]
        acc_lo = acc_lo + jnp.where(lo, c, 0.0)
        acc_hi = acc_hi + jnp.where(lo, 0.0, c)
    o_lo = acc_lo[0:HEAD_DIM] * (1.0 / acc_lo[HEAD_DIM:HEAD_DIM + 1])
    o_hi = acc_hi[0:HEAD_DIM] * (1.0 / acc_hi[HEAD_DIM:HEAD_DIM + 1])
    return o_lo, o_hi


def _interleave(sweeps, width=SWEEP_WIDTH):
    outs = [None] * len(sweeps)
    groups = [list(range(i, min(i + width, len(sweeps)))) for i in range(0, len(sweeps), width)]

    def step(group, until):
        done = True
        for i in group:
            if state[i] == until:
                continue
            try:
                state[i] = next(sweeps[i])
            except StopIteration as fin:
                outs[i] = fin.value
                state[i] = 'END'
            done = done and state[i] == until
        return done

    state = [None] * len(sweeps)
    while not step(groups[0], 'M'):
        pass
    for g, group in enumerate(groups):
        nxt = groups[g + 1] if g + 1 < len(groups) else []
        cur_done, nxt_done = False, not nxt
        while not (cur_done and nxt_done):
            if not nxt_done:
                nxt_done = step(nxt, 'M')
            if not cur_done:
                cur_done = step(group, 'END')
    return outs


def _nsa_side(n_sb, qt, clamp, x, qn_ref, qr_ref, kc_ref, vct_ref, ovt_ref, kw_ref, vw_ref, s_refs):
    t0 = qt * TT
    tpos = t0 + lax.broadcasted_iota(jnp.int32, (1, TT), 1)
    kpos0 = lax.broadcasted_iota(jnp.int32, (TT, 1), 0)
    nc = kc_ref.shape[1]

    kc = kc_ref[0]
    cmp_end = lax.broadcasted_iota(jnp.int32, (nc, 1), 0) * CMP_STRIDE + (CMP_LEN - 1)
    cmask = cmp_end <= tpos
    c_slots = [s_refs[r].at[6 + x, 0:nc] for r in range(NSA_REP)]
    maxima, inv_sums = [], []
    for r in range(NSA_REP):
        s = jnp.where(cmask, _dot(kc, qn_ref[0, r * HEAD_DIM:(r + 1) * HEAD_DIM, :]), -jnp.inf)
        c_slots[r][...] = s
        m = jnp.max(_slab_max(s), axis=0, keepdims=True)
        maxima.append(jnp.where(m == -jnp.inf, 0.0, m))
        yield
    for r in range(NSA_REP):
        e = jnp.exp2(c_slots[r][...] - maxima[r])
        c_slots[r][...] = e
        d = jnp.sum(_slab_sum(e), axis=0, keepdims=True)
        inv_sums.append(1.0 / jnp.where(d > 0, d, 1.0))
        yield
    psum = jnp.zeros((nc, TT), F32)
    o_cmp = []
    for r in range(NSA_REP):
        p = c_slots[r][...] * inv_sums[r]
        psum = psum + p
        o_cmp.append(_dot(vct_ref[0], p.astype(BF16)))
        yield
    p_hi = psum.astype(BF16)
    p_lo = (psum - p_hi.astype(F32)).astype(BF16)
    imp = _dot(ovt_ref[...], p_hi) + _dot(ovt_ref[...], p_lo)

    nrow = ovt_ref.shape[0]
    blk = lax.broadcasted_iota(jnp.int32, (nrow, 1), 0)
    cur = tpos // SEL_LEN
    forced = (blk == 0) | (blk == cur) | (blk == cur - 1)
    valid = blk * SEL_LEN <= tpos
    taken = -3.0
    score = jnp.where(forced, taken, jnp.where(valid, imp, -1.0))
    score = jnp.where(blk < n_sb, score, -2.0)
    blk_f = blk.astype(F32)

    j1 = jnp.maximum(qt - 1, 0) if clamp else qt - 1
    j2 = jnp.maximum(qt - 2, 0) if clamp else qt - 2
    in_window = tpos - (j2 * TT + kpos0) < WINDOW
    masks = (kpos0 <= tpos - t0,
             jnp.broadcast_to(qt >= 1, (TT, TT)) if clamp else None,
             ((qt >= 2) & in_window) if clamp else in_window)
    k_tiles = [kw_ref[0, pl.ds(pl.multiple_of(j * TT, TT), TT), :][:, 0:HEAD_DIM] for j in (qt, j1, j2)]
    v_tiles = [vw_ref[0, j] for j in (qt, j1, j2)]
    w_max, o_win = [], []

    def window_scores(r):
        q = qr_ref[0, r * HEAD_DIM:(r + 1) * HEAD_DIM, :]
        maxima = []
        for t, (k, mask) in enumerate(zip(k_tiles, masks)):
            s = _dot(k, q)
            s = s if mask is None else jnp.where(mask, s, MASK_BIAS)
            s_refs[r][3 * x + t] = s
            maxima.append(_slab_max(s))
        w_max.append(jnp.max(functools.reduce(jnp.maximum, maxima), axis=0, keepdims=True))

    def window_output(r):
        acc = sum(_dot(v, jnp.exp2(s_refs[r][3 * x + t] - w_max[r]).astype(BF16))[0:OROWS]
                  for t, v in enumerate(v_tiles))
        o_win.append(acc[0:HEAD_DIM] * (1.0 / acc[HEAD_DIM:HEAD_DIM + 1]))

    window_steps = ([functools.partial(window_scores, r) for r in range(NSA_REP)]
                    + [functools.partial(window_output, r) for r in range(NSA_REP)])
    n_round = max(min(N_SEL, n_sb) - 3, 0)
    per_step = max(n_round // len(window_steps), 1)

    def over_blocks(pair_op, v):
        slabs = v.reshape(-1, 8, TT)
        v = functools.reduce(pair_op, [slabs[i] for i in range(slabs.shape[0])])
        for shift in (4, 2, 1):
            v = pair_op(v, pltpu.roll(v, shift, axis=0))
        return jnp.tile(v, (nrow // 8, 1))

    for i in range(n_round):
        top = over_blocks(jnp.maximum, score)
        first = over_blocks(jnp.minimum, jnp.where(score == top, blk_f, float(nrow)))
        score = jnp.where(blk_f == first, taken, score)
        if i % per_step == per_step - 1 and window_steps:
            window_steps.pop(0)()
        yield
    while window_steps:
        window_steps.pop(0)()
    sel_bias = jnp.where(score == taken, 0.0, MASK_BIAS).astype(BF16)
    return o_cmp, o_win, sel_bias


def _nsa_attn_body(nq, n_sb, qnl_ref, qnh_ref, qrl_ref, qrh_ref, kc_ref, vct_ref, ovt_ref, ks_ref, vs_ref,
                   kw_ref, vw_ref, gl_ref, gh_ref, olo_ref, ohi_ref, q_ref, *scratch):
    qi = pl.program_id(2)
    s_refs, c_refs = scratch[:NSA_REP], scratch[NSA_REP:]
    side_in = ((qi, qnl_ref, qrl_ref), (nq - 1 - qi, qnh_ref, qrh_ref))
    gens = [_nsa_side(n_sb, qt, x == 0 or nq < 6, x, qn_ref, qr_ref, kc_ref, vct_ref, ovt_ref, kw_ref, vw_ref,
                      s_refs) for x, (qt, qn_ref, qr_ref) in enumerate(side_in)]
    sides = [None, None]
    while None in sides:
        for x, gen in enumerate(gens):
            if sides[x] is None:
                try:
                    next(gen)
                except StopIteration as done:
                    sides[x] = done.value
    for x, (_, _, qr_ref) in enumerate(side_in):
        for r in range(NSA_REP):
            q_ref[x, r, 0:HEAD_DIM, :] = qr_ref[0, r * HEAD_DIM:(r + 1) * HEAD_DIM, :]
            q_ref[x, r, HEAD_DIM:AUG, :] = sides[x][2]

    k_tile = lambda kv: ks_ref[0, pl.ds(pl.multiple_of(kv * TT, TT), TT), :]
    v_tile = lambda kv: vs_ref[0, kv]
    o_sel = _interleave([_paired_sweep(nq, qi, lambda x, r=r: q_ref[x, r], k_tile, v_tile,
                                       lambda lo, kv: None, s_refs[r], c_refs[r]) for r in range(NSA_REP)])

    for x, (g_ref, o_ref) in enumerate(((gl_ref, olo_ref), (gh_ref, ohi_ref))):
        o_cmp, o_win, _ = sides[x]
        g = g_ref[0, 0]
        rows = []
        for r in range(NSA_REP):
            gc = g[r * N_BRANCH + 0:r * N_BRANCH + 1]
            gs = g[r * N_BRANCH + 1:r * N_BRANCH + 2]
            gw = g[r * N_BRANCH + 2:r * N_BRANCH + 3]
            rows.append(gc * o_cmp[r] + gs * o_sel[r][x] + gw * o_win[r])
        o_ref[0] = jnp.concatenate(rows, axis=0).T.astype(BF16)


def _nsa_attn(batch, seq_len, qn, qr, kc, vct, ovt, ksa, vs, kwa, vw, gt):
    nq = seq_len // TT
    nc = kc.shape[1]
    n_sb = seq_len // SEL_LEN
    g_rows = NSA_REP * N_BRANCH
    half = nq // 2
    ksa3 = ksa.reshape(batch, seq_len, NSA_GROUPS * AUG)
    kwa3 = kwa.reshape(batch, seq_len, NSA_GROUPS * AUG)
    vs5 = vs.reshape(batch, nq, NSA_GROUPS, VROWS, TT)
    vw5 = vw.reshape(batch, nq, NSA_GROUPS, VROWS, TT)
    gt4 = gt.reshape(batch * nq, NSA_GROUPS, g_rows, TT)
    lo_tile = lambda b, q: b * nq + q
    hi_tile = lambda b, q: b * nq + nq - 1 - q
    q_lo = pl.BlockSpec((1, NSA_REP * HEAD_DIM, TT), lambda b, g, q: (lo_tile(b, q), g, 0))
    q_hi = pl.BlockSpec((1, NSA_REP * HEAD_DIM, TT), lambda b, g, q: (hi_tile(b, q), g, 0))
    kspec = pl.BlockSpec((1, seq_len, AUG), lambda b, g, q: (b, 0, g))
    vspec = pl.BlockSpec((1, nq, None, VROWS, TT), lambda b, g, q: (b, 0, g, 0, 0))
    out = jax.ShapeDtypeStruct((batch, seq_len // 2, HQ), BF16)
    return pl.pallas_call(
        functools.partial(_nsa_attn_body, nq, n_sb),
        grid=(batch, NSA_GROUPS, half),
        in_specs=[
            q_lo, q_hi, q_lo, q_hi,
            pl.BlockSpec((1, nc, HEAD_DIM), lambda b, g, q: (b * NSA_GROUPS + g, 0, 0)),
            pl.BlockSpec((1, HEAD_DIM, nc), lambda b, g, q: (b * NSA_GROUPS + g, 0, 0)),
            pl.BlockSpec(ovt.shape, lambda b, g, q: (0, 0)),
            kspec, vspec, kspec, vspec,
            pl.BlockSpec((1, 1, g_rows, TT), lambda b, g, q: (lo_tile(b, q), g, 0, 0)),
            pl.BlockSpec((1, 1, g_rows, TT), lambda b, g, q: (hi_tile(b, q), g, 0, 0)),
        ],
        out_specs=[pl.BlockSpec((1, TT, NSA_REP * HEAD_DIM), lambda b, g, q: (b, q, g)),
                   pl.BlockSpec((1, TT, NSA_REP * HEAD_DIM), lambda b, g, q: (b, half - 1 - q, g))],
        out_shape=[out, out],
        scratch_shapes=([pltpu.VMEM((2, NSA_REP, AUG, TT), BF16)]
                        + [pltpu.VMEM((nq + 1, TT, TT), F32)] * NSA_REP
                        + [pltpu.VMEM((half, OROWS, TT), F32)] * NSA_REP),
        compiler_params=_cparams(("parallel", "parallel", "arbitrary")),
        name="nsa_attn",
    )(qn, qn, qr, qr, kc, vct, ovt, ksa3, vs5, kwa3, vw5, gt4, gt4)


def _overlap_t(seq_len, nc):
    n_cmp = (seq_len - CMP_LEN) // CMP_STRIDE + 1
    n_sb = seq_len // SEL_LEN
    cmp_start = jnp.arange(n_cmp) * CMP_STRIDE
    sel_start = jnp.arange(n_sb) * SEL_LEN
    ov = jnp.clip(jnp.minimum(cmp_start[:, None] + CMP_LEN, sel_start[None, :] + SEL_LEN)
                  - jnp.maximum(cmp_start[:, None], sel_start[None, :]), 0, None).astype(F32) / CMP_LEN
    out = jnp.zeros((SEL_LEN, nc), F32).at[:n_sb, :n_cmp].set(ov.T)
    return out.astype(BF16)


def _nsa_attention(x, batch, seq_len, w_in, b_gate, pe, k_w1, k_b1, k_w2, v_w1, v_b1, v_w2, tables):
    wt, wn, bg = _nsa_inproj_weights(w_in, b_gate)
    qn, qr, vs, vw, gt, ksa, kwa, kc, vc = _nsa_inproj(x, seq_len, wt, wn, bg, *tables)

    nc = seq_len // CMP_STRIDE
    hw = CMP_STRIDE * HEAD_DIM

    def half_blocks(t):
        t = t.reshape(batch, seq_len, NSA_GROUPS, HEAD_DIM).transpose(0, 2, 1, 3)
        return t.reshape(batch * NSA_GROUPS, nc, hw)

    kcmp, vcmp_t = _compress(
        half_blocks(kc), half_blocks(vc), pe.reshape(2, hw),
        k_w1.reshape(2, hw, PHI_HIDDEN).astype(BF16), k_b1.reshape(1, -1), k_w2.astype(BF16),
        v_w1.reshape(2, hw, PHI_HIDDEN).transpose(0, 2, 1).astype(BF16), v_b1.reshape(-1, 1),
        v_w2.T.astype(BF16))
    return _nsa_attn(batch, seq_len, qn, qr, kcmp, vcmp_t, _overlap_t(seq_len, nc), ksa, vs, kwa, vw, gt)


def _nsa_layer(x, batch, seq_len, w_in, b_gate, pe, k_w1, k_b1, k_w2, v_w1, v_b1, v_w2, w_o,
               ln_g, ln_b, tables):
    a_lo, a_hi = _nsa_attention(x, batch, seq_len, w_in, b_gate, pe, k_w1, k_b1, k_w2, v_w1, v_b1, v_w2,
                                tables)
    return _proj_ln(a_lo, a_hi, w_o.astype(BF16), x, ln_g.reshape(1, -1), ln_b.reshape(1, -1))


_FOX_T_ROWS = 2 * HQ + N_HEADS
_FOX_N_COLS = N_HEADS * AUG + 128
_N_PIECE = 3


def _fox_inproj_body(seq_tiles, x_ref, wt_ref, wn_ref, bft_ref, bfn_ref, place_ref, ones_ref, route_ref,
                     qa_ref, ka_ref, vt_ref, off_ref, run_ref):
    i = pl.program_id(0)
    xb = x_ref[...].astype(BF16)
    t = _dot_nt(wt_ref[...], xb)
    n = _dot(xb, wn_ref[...])
    vt_ref[0, :, 0:HEAD_DIM, :] = t[HQ:2 * HQ].reshape(N_HEADS, HEAD_DIM, TT).astype(BF16)
    vt_ref[0, :, HEAD_DIM:VROWS, :] = jnp.ones((N_HEADS, VROWS - HEAD_DIM, TT), BF16)
    lf_t = _log_sigmoid(t[2 * HQ:] + bft_ref[...]) * LOG2E
    kw = N_HEADS * AUG
    lf_n = _log_sigmoid(n[:, kw:kw + _N_PIECE * N_HEADS] + bfn_ref[...]) * LOG2E

    r_i = lax.broadcasted_iota(jnp.int32, (TT, TT), 0)
    c_i = lax.broadcasted_iota(jnp.int32, (TT, TT), 1)
    upper = ((r_i > 0) & (r_i <= c_i)).astype(BF16)
    lower = ((c_i > 0) & (c_i <= r_i)).astype(BF16)
    a_t = sum(_dot(p, upper) for p in _split3(lf_t))
    a_n = sum(_dot(lower, p) for p in _split3(lf_n))

    q3 = (t[0:HQ] * QSCALE).reshape(N_HEADS, HEAD_DIM, TT)
    qa_ref[0, :, 0:HEAD_DIM, :] = q3.astype(BF16)
    stacked = jnp.concatenate(list(_split3(a_t)) + [jnp.ones((N_HEADS, TT), BF16)], axis=0)
    for h in range(N_HEADS):
        qa_ref[0, h, HEAD_DIM:AUG, :] = _dot(route_ref[h], stacked).astype(BF16)

    b1, b2, b3 = _split3(-a_n)
    grp = lax.broadcasted_iota(jnp.int32, (TT, _N_PIECE * N_HEADS), 1) // N_HEADS
    bsel = jnp.where(grp == 0, b1, jnp.where(grp == 1, b2, b3))
    ka_ref[...] = (n[:, 0:kw] + _dot(bsel, place_ref[...]) + ones_ref[...]).astype(BF16)

    @pl.when(i % seq_tiles == 0)
    def _():
        run_ref[...] = jnp.zeros_like(run_ref)

    first = lf_t[:, 0:1]
    off_ref[0] = jnp.broadcast_to(run_ref[:, 0:1] + first, (N_HEADS, TT))
    run_ref[...] = run_ref[...] + (a_t[:, TT - 1:TT] + first)


def _fox_inproj(x, seq_len, wt, wn, bft, bfn, place, ones, route):
    n = x.shape[0]
    nt = n // TT
    kw = N_HEADS * AUG
    full = lambda r, c: pl.BlockSpec((r, c), lambda i: (0, 0))
    return pl.pallas_call(
        functools.partial(_fox_inproj_body, seq_len // TT),
        grid=(nt,),
        in_specs=[
            pl.BlockSpec((TT, D_MODEL), lambda i: (i, 0)),
            full(_FOX_T_ROWS, D_MODEL), full(D_MODEL, _FOX_N_COLS),
            full(N_HEADS, 1), full(1, _N_PIECE * N_HEADS),
            full(_N_PIECE * N_HEADS, kw), full(1, kw),
            pl.BlockSpec((N_HEADS, AUG - HEAD_DIM, AUG - HEAD_DIM), lambda i: (0, 0, 0)),
        ],
        out_specs=[
            pl.BlockSpec((1, N_HEADS, AUG, TT), lambda i: (i, 0, 0, 0)),
            pl.BlockSpec((TT, kw), lambda i: (i, 0)),
            pl.BlockSpec((1, N_HEADS, VROWS, TT), lambda i: (i, 0, 0, 0)),
            pl.BlockSpec((1, N_HEADS, TT), lambda i: (i, 0, 0)),
        ],
        out_shape=[
            jax.ShapeDtypeStruct((nt, N_HEADS, AUG, TT), BF16),
            jax.ShapeDtypeStruct((n, kw), BF16),
            jax.ShapeDtypeStruct((nt, N_HEADS, VROWS, TT), BF16),
            jax.ShapeDtypeStruct((nt, N_HEADS, TT), F32),
        ],
        scratch_shapes=[pltpu.VMEM((N_HEADS, 128), F32)],
        compiler_params=_cparams(("arbitrary",)),
        name="fox_inproj",
    )(x, wt, wn, bft, bfn, place, ones, route)


def _fox_inproj_weights(w_in, b_f):
    wq, wk, wv, wf = jnp.split(w_in, [HQ, 2 * HQ, 3 * HQ], axis=1)
    wt = jnp.concatenate([wq, wv, wf], axis=1).T.astype(BF16)
    pad = jnp.zeros((D_MODEL, _FOX_N_COLS - N_HEADS * AUG - _N_PIECE * N_HEADS), F32)
    wn = jnp.concatenate([_pad_heads(wk, N_HEADS)] + [wf] * _N_PIECE + [pad], axis=1).astype(BF16)
    bft = b_f.reshape(-1, 1)
    bfn = jnp.tile(b_f, _N_PIECE).reshape(1, -1)
    rows = jnp.arange(_N_PIECE * N_HEADS)
    cols = (rows % N_HEADS) * AUG + HEAD_DIM + _N_PIECE + rows // N_HEADS
    place = jnp.zeros((_N_PIECE * N_HEADS, N_HEADS * AUG), F32).at[rows, cols].set(1.0).astype(BF16)
    lane = jnp.arange(N_HEADS * AUG) % AUG
    ones = ((lane >= HEAD_DIM) & (lane < HEAD_DIM + _N_PIECE)).astype(F32).reshape(1, -1)
    hh = jnp.arange(N_HEADS)
    route = jnp.zeros((N_HEADS, AUG - HEAD_DIM, AUG - HEAD_DIM), F32)
    for k in range(_N_PIECE):
        route = route.at[hh, k, k * N_HEADS + hh].set(1.0)
        route = route.at[hh, _N_PIECE + k, _N_PIECE * N_HEADS].set(1.0)
    return wt, wn, bft, bfn, place, ones, route.astype(BF16)


def _fox_attn_body(nq, qlo_ref, qhi_ref, ka_ref, vt_ref, off_ref, olo_ref, ohi_ref, q_ref, *scratch):
    hg = pl.program_id(1)
    qi = pl.program_id(2)
    s_refs, c_refs = scratch[:HEAD_BLOCK], scratch[HEAD_BLOCK:]
    q_ref[0] = qlo_ref[0]
    q_ref[1] = qhi_ref[0]
    sweeps = []
    for r in range(HEAD_BLOCK):
        h = hg * HEAD_BLOCK + r

        def off_row(t, h=h):
            return off_ref[0, pl.ds(t, 1), pl.ds(h, 1), :].reshape(1, TT)

        off_lo, off_hi = off_row(qi), off_row(nq - 1 - qi)

        def bias_row(lo, kv, off_lo=off_lo, off_hi=off_hi, off_row=off_row):
            base = off_hi if lo is False else jnp.where(lo, off_lo, off_hi)
            return base - off_row(kv)

        def k_tile(kv, r=r):
            return ka_ref[0, pl.ds(pl.multiple_of(kv * TT, TT), TT), r * AUG:(r + 1) * AUG]

        def v_tile(kv, r=r):
            return vt_ref[0, kv, r]

        sweeps.append(_paired_sweep(nq, qi, lambda x, r=r: q_ref[x, r], k_tile, v_tile, bias_row,
                                    s_refs[r], c_refs[r]))
    outs = _interleave(sweeps)
    for x, o_ref in enumerate((olo_ref, ohi_ref)):
        o_ref[0] = jnp.concatenate([o[x] for o in outs], axis=0).T.astype(BF16)


def _fox_attn(batch, seq_len, qa, ka, vt, off):
    nq = seq_len // TT
    half = nq // 2
    hb = HEAD_BLOCK
    ka3 = ka.reshape(batch, seq_len, N_HEADS * AUG)
    vt5 = vt.reshape(batch, nq, N_HEADS, VROWS, TT)
    off4 = off.reshape(batch, nq, N_HEADS, TT)
    out = jax.ShapeDtypeStruct((batch, seq_len // 2, HQ), BF16)
    return pl.pallas_call(
        functools.partial(_fox_attn_body, nq),
        grid=(batch, N_HEADS // hb, half),
        in_specs=[
            pl.BlockSpec((1, hb, AUG, TT), lambda b, h, q: (b * nq + q, h, 0, 0)),
            pl.BlockSpec((1, hb, AUG, TT), lambda b, h, q: (b * nq + nq - 1 - q, h, 0, 0)),
            pl.BlockSpec((1, seq_len, hb * AUG), lambda b, h, q: (b, 0, h)),
            pl.BlockSpec((1, nq, hb, VROWS, TT), lambda b, h, q: (b, 0, h, 0, 0)),
            pl.BlockSpec((1, nq, N_HEADS, TT), lambda b, h, q: (b, 0, 0, 0)),
        ],
        out_specs=[pl.BlockSpec((1, TT, hb * HEAD_DIM), lambda b, h, q: (b, q, h)),
                   pl.BlockSpec((1, TT, hb * HEAD_DIM), lambda b, h, q: (b, half - 1 - q, h))],
        out_shape=[out, out],
        scratch_shapes=([pltpu.VMEM((2, hb, AUG, TT), BF16)]
                        + [pltpu.VMEM((nq + 1, TT, TT), F32)] * hb
                        + [pltpu.VMEM((half, OROWS, TT), F32)] * hb),
        compiler_params=_cparams(("parallel", "parallel", "arbitrary")),
        name="fox_attn",
    )(qa, qa, ka3, vt5, off4)


def _fox_layer(x, batch, seq_len, w_in, b_f, w_o, ln_g, ln_b):
    qa, ka, vt, off = _fox_inproj(x, seq_len, *_fox_inproj_weights(w_in, b_f))
    a_lo, a_hi = _fox_attn(batch, seq_len, qa, ka, vt, off)
    return _proj_ln(a_lo, a_hi, w_o.astype(BF16), x, ln_g.reshape(1, -1), ln_b.reshape(1, -1))


def kernel(x, nsa_w_in, nsa_b_gate, nsa_pe, nsa_phik_w1, nsa_phik_b1, nsa_phik_w2, nsa_phiv_w1,
           nsa_phiv_b1, nsa_phiv_w2, nsa_w_o, fox_w_in, fox_b_f, fox_w_o, ffn_w_up, ffn_conv_w,
           ffn_conv_b, ffn_w_down, ln1_g, ln1_b, ln2_g, ln2_b):
    batch, seq_len, d = x.shape
    assert d == D_MODEL and seq_len % (2 * PROJ_TM) == 0 and seq_len % FFN_TM == 0 and seq_len % (2 * TT) == 0
    assert seq_len // SEL_LEN <= AUG - HEAD_DIM
    tables = _rope_tables(seq_len)
    h = x.reshape(batch * seq_len, d)
    for i in range(DEPTH):
        j = i // 2
        if i % 2 == 0:
            h = _nsa_layer(h, batch, seq_len, nsa_w_in[j], nsa_b_gate[j], nsa_pe[j], nsa_phik_w1[j],
                           nsa_phik_b1[j], nsa_phik_w2[j], nsa_phiv_w1[j], nsa_phiv_b1[j],
                           nsa_phiv_w2[j], nsa_w_o[j], ln1_g[i], ln1_b[i], tables)
        else:
            h = _fox_layer(h, batch, seq_len, fox_w_in[j], fox_b_f[j], fox_w_o[j], ln1_g[i], ln1_b[i])
        h = _ffn(h, seq_len, *_ffn_weights(ffn_w_up[i], ffn_conv_w[i], ffn_conv_b[i], ffn_w_down[i]),
                 ln2_g[i].reshape(1, -1), ln2_b[i].reshape(1, -1))
    return h.reshape(batch, seq_len, d)
```

```python
import functools
import math

import jax
import jax.numpy as jnp
from jax import lax
from jax.experimental import pallas as pl
from jax.experimental.pallas import tpu as pltpu

F32 = jnp.float32
BF16 = jnp.bfloat16

D_MODEL = 1024
DEPTH = 4
HEAD_DIM = 64
N_HEADS = 16
HQ = N_HEADS * HEAD_DIM
NSA_GROUPS = 4
NSA_REP = 4
NSA_KV = NSA_GROUPS * HEAD_DIM
CMP_LEN = 32
CMP_STRIDE = 16
SEL_LEN = 64
N_SEL = 16
WINDOW = 512
PHI_HIDDEN = 256
N_BRANCH = 3
D_FF = 2816
ROPE_THETA = 10000.0
ALPHA = (2 * DEPTH) ** 0.25
LN_EPS = 1e-5
SCALE = HEAD_DIM ** -0.5
LOG2E = math.log2(math.e)
QSCALE = SCALE * LOG2E
VROWS = 80
OROWS = 72

TT = 256
AUG = 128
MASK_BIAS = -1e30
FF_CHUNK = 256
FFN_TM = 256
PROJ_TM = 512
HEAD_BLOCK = 8
N_SCORE_BUF = 3
SWEEP_WIDTH = 1
HALO = 16
VMEM_LIMIT = 56 * 1024 * 1024


def _cparams(sem):
    return pltpu.CompilerParams(dimension_semantics=sem, vmem_limit_bytes=VMEM_LIMIT)


def _layer_norm(z, g, b):
    mu = jnp.mean(z, axis=-1, keepdims=True)
    zc = z - mu
    var = jnp.mean(zc * zc, axis=-1, keepdims=True)
    return zc * lax.rsqrt(var + LN_EPS) * g + b


def _gelu_tanh(x):
    c = math.sqrt(2.0 / math.pi)
    return x * (0.5 * (1.0 + jnp.tanh(c * (x + 0.044715 * (x * x * x)))))


def _log_sigmoid(z):
    return -(jnp.maximum(-z, 0.0) + jnp.log1p(jnp.exp(-jnp.abs(z))))


def _dot(a, b):
    return jnp.dot(a, b, preferred_element_type=F32)


def _dot_nt(a, b):
    return lax.dot_general(a, b, (((1,), (1,)), ((), ())), preferred_element_type=F32)


def _split3(x):
    p1 = x.astype(BF16)
    r1 = x - p1.astype(F32)
    p2 = r1.astype(BF16)
    p3 = (r1 - p2.astype(F32)).astype(BF16)
    return p1, p2, p3


def _proj_ln_body(half, lo_ref, hi_ref, w_ref, x_ref, g_ref, b_ref, o_ref):
    in_lo = (pl.program_id(0) % (2 * half)) < half
    a = jnp.where(in_lo, lo_ref[0], hi_ref[0])
    tm = a.shape[0]
    rows = [slice(r, r + tm // 4) for r in range(0, tm, tm // 4)]
    ys = [_dot(a[r], w_ref[...]) for r in rows]
    for r, y in zip(rows, ys):
        o_ref[r, :] = _layer_norm(ALPHA * x_ref[r, :] + y, g_ref[...], b_ref[...])


def _proj_ln(a_lo, a_hi, w, x, g, b):
    _, s_half, k = a_lo.shape
    n, d = x.shape
    tm = PROJ_TM
    half = s_half // tm
    return pl.pallas_call(
        functools.partial(_proj_ln_body, half),
        grid=(n // tm,),
        in_specs=[
            pl.BlockSpec((1, tm, k), lambda i: (i // (2 * half), jnp.minimum(i % (2 * half), half - 1), 0)),
            pl.BlockSpec((1, tm, k), lambda i: (i // (2 * half), jnp.maximum(i % (2 * half) - half, 0), 0)),
            pl.BlockSpec((k, d), lambda i: (0, 0)),
            pl.BlockSpec((tm, d), lambda i: (i, 0)),
            pl.BlockSpec((1, d), lambda i: (0, 0)),
            pl.BlockSpec((1, d), lambda i: (0, 0)),
        ],
        out_specs=pl.BlockSpec((tm, d), lambda i: (i, 0)),
        out_shape=jax.ShapeDtypeStruct((n, d), F32),
        compiler_params=_cparams(("parallel",)),
        name="proj_ln",
    )(a_lo, a_hi, w, x, g, b)


GELU_C0 = math.sqrt(2.0 / math.pi)
GELU_C1 = GELU_C0 * 0.044715


def _ffn_body(seq_tiles, x_ref, xp_ref, wa_ref, wb_ref, cwa_ref, cwb_ref, wd_ref, g_ref, b_ref, o_ref,
              perm_ref):
    i = pl.program_id(0)
    tm, d = x_ref.shape
    nj = tm // 8
    pitch = nj + 1
    n_lane = d // 128
    for s in range(8):
        for cb in range(n_lane):
            perm_ref[cb, pl.ds(s * pitch, nj), :] = x_ref[s * nj:(s + 1) * nj, cb * 128:(cb + 1) * 128]
    x = jnp.concatenate(
        [jnp.concatenate([perm_ref[cb, pl.ds(j, 8, stride=pitch), :] for cb in range(n_lane)], axis=1)
         for j in range(nj)], axis=0)
    starts_seq = (i % seq_tiles) == 0
    halo = jnp.where(starts_seq, 0.0, xp_ref[...]).astype(BF16)
    xcat = jnp.concatenate([halo, x.astype(BF16)], axis=0)
    n_chunks = wa_ref.shape[0]
    first_sublane = lax.broadcasted_iota(jnp.int32, (8, 1), 0) == 0

    def up(c):
        return _dot(xcat, wa_ref[c]), _dot(xcat, wb_ref[c])

    def conv(h, cw):
        h3 = h[HALO:].reshape(nj, 8, h.shape[-1])

        def wrap(slab, halo_row):
            return jnp.where(first_sublane, halo_row, pltpu.roll(slab, 1, axis=0))[None]

        prev1 = wrap(h3[nj - 1], h[HALO - 1:HALO])
        prev2 = wrap(h3[nj - 2], h[HALO - 2:HALO - 1])
        s1 = jnp.concatenate([prev1, h3[:-1]], axis=0)
        s2 = jnp.concatenate([prev2, prev1, h3[:-2]], axis=0)
        return (cw[0:1] * s2 + cw[1:2] * s1 + cw[2:3] * h3 + cw[3:4]).reshape(tm, h.shape[-1])

    y = None
    nxt = up(0)
    for c in range(n_chunks):
        cur, nxt = nxt, (up(c + 1) if c + 1 < n_chunks else None)
        ha = conv(cur[0], cwa_ref[c])
        act = ha + ha * jnp.tanh(ha * (GELU_C0 + GELU_C1 * (ha * ha)))
        hb = conv(cur[1], cwb_ref[c])
        gated = (act * hb).astype(BF16)
        part = _dot(gated, wd_ref[c])
        y = part if y is None else y + part
    out = _layer_norm(ALPHA * x + y, g_ref[...], b_ref[...])
    for j in range(nj):
        for cb in range(n_lane):
            perm_ref[cb, pl.ds(j, 8, stride=pitch), :] = out[j * 8:(j + 1) * 8, cb * 128:(cb + 1) * 128]
    for s in range(8):
        for cb in range(n_lane):
            o_ref[s * nj:(s + 1) * nj, cb * 128:(cb + 1) * 128] = perm_ref[cb, pl.ds(s * pitch, nj), :]


def _ffn(x, seq_len, wa, wb, cwa, cwb, wd, g, b):
    n, d = x.shape
    tm = FFN_TM
    nc = wa.shape[0]
    cf = wa.shape[2]
    hblk = tm // HALO
    return pl.pallas_call(
        functools.partial(_ffn_body, seq_len // tm),
        grid=(n // tm,),
        in_specs=[
            pl.BlockSpec((tm, d), lambda i: (i, 0)),
            pl.BlockSpec((HALO, d), lambda i: (jnp.maximum(i * hblk - 1, 0), 0)),
            pl.BlockSpec((nc, d, cf), lambda i: (0, 0, 0)),
            pl.BlockSpec((nc, d, cf), lambda i: (0, 0, 0)),
            pl.BlockSpec((nc, 4, cf), lambda i: (0, 0, 0)),
            pl.BlockSpec((nc, 4, cf), lambda i: (0, 0, 0)),
            pl.BlockSpec((nc, cf, d), lambda i: (0, 0, 0)),
            pl.BlockSpec((1, d), lambda i: (0, 0)),
            pl.BlockSpec((1, d), lambda i: (0, 0)),
        ],
        out_specs=pl.BlockSpec((tm, d), lambda i: (i, 0)),
        out_shape=jax.ShapeDtypeStruct((n, d), F32),
        scratch_shapes=[pltpu.VMEM((d // 128, tm + 8, 128), F32)],
        compiler_params=_cparams(("parallel",)),
        name="conv_ffn",
    )(x, x, wa, wb, cwa, cwb, wd, g, b)


def _ffn_weights(w_up, conv_w, conv_b, w_down):
    nc = D_FF // FF_CHUNK

    def up(w):
        return w.reshape(D_MODEL, nc, FF_CHUNK).transpose(1, 0, 2).astype(BF16)

    def taps(cw, cb):
        t = jnp.concatenate([cw, cb[None]], axis=0)
        return t.reshape(4, nc, FF_CHUNK).transpose(1, 0, 2)

    wa, wb = up(w_up[:, :D_FF]), up(w_up[:, D_FF:])
    cwa = taps(conv_w[:, :D_FF], conv_b[:D_FF])
    cwb = 0.5 * taps(conv_w[:, D_FF:], conv_b[D_FF:])
    wd = w_down.reshape(nc, FF_CHUNK, D_MODEL).astype(BF16)
    return wa, wb, cwa, cwb, wd


_NSA_T_ROWS = HQ + 2 * NSA_KV + N_BRANCH * N_HEADS
_NSA_N_COLS = 2 * NSA_KV + 2 * NSA_GROUPS * AUG


def _nsa_inproj_body(seq_tiles, x_ref, wt_ref, wn_ref, bg_ref, cost_ref, sint_ref, cosn_ref, sinn_ref,
                     qn_ref, qr_ref, vs_ref, vw_ref, gt_ref, ksa_ref, kwa_ref, kc_ref, vc_ref):
    i = pl.program_id(0)
    xb = x_ref[...].astype(BF16)
    t = _dot_nt(wt_ref[...], xb)
    n = _dot(xb, wn_ref[...])
    q = t[0:HQ] * QSCALE
    qn_ref[0] = q.astype(BF16)
    q3 = q.reshape(N_HEADS, HEAD_DIM, TT)
    half = HEAD_DIM // 2
    rot = jnp.concatenate([-q3[:, half:], q3[:, :half]], axis=1)
    qr = q3 * cost_ref[0][None] + rot * sint_ref[0][None]
    qr_ref[0] = qr.reshape(HQ, TT).astype(BF16)
    ones = jnp.ones((NSA_GROUPS, VROWS - HEAD_DIM, TT), BF16)
    for v_ref, lo in ((vs_ref, HQ), (vw_ref, HQ + NSA_KV)):
        v_ref[0, :, 0:HEAD_DIM, :] = t[lo:lo + NSA_KV].reshape(NSA_GROUPS, HEAD_DIM, TT).astype(BF16)
        v_ref[0, :, HEAD_DIM:VROWS, :] = ones
    gt_ref[0] = jax.nn.sigmoid(t[HQ + 2 * NSA_KV:] + bg_ref[...])

    kc_ref[...] = n[:, 0:NSA_KV]
    vc_ref[...] = n[:, NSA_KV:2 * NSA_KV]
    width = NSA_GROUPS * AUG
    lane = lax.broadcasted_iota(jnp.int32, (TT, width), 1) % AUG
    cosn, sinn = cosn_ref[...], sinn_ref[...]

    def rope_nat(k):
        rh = jnp.where(lane < half, -pltpu.roll(k, width - half, axis=1), pltpu.roll(k, half, axis=1))
        return k * cosn + rh * sinn

    ks = rope_nat(n[:, 2 * NSA_KV:2 * NSA_KV + width])
    kw = rope_nat(n[:, 2 * NSA_KV + width:])
    row = lax.broadcasted_iota(jnp.int32, (TT, width), 0)
    blk = ((i % seq_tiles) * TT + row) // SEL_LEN
    onehot = (lane - HEAD_DIM == blk).astype(F32)
    ksa_ref[...] = (ks + onehot).astype(BF16)
    kwa_ref[...] = kw.astype(BF16)


def _nsa_inproj(x, seq_len, wt, wn, bg, cost, sint, cosn, sinn):
    n = x.shape[0]
    nt = n // TT
    seq_tiles = seq_len // TT
    width = NSA_GROUPS * AUG
    full = lambda r, c: pl.BlockSpec((r, c), lambda i: (0, 0))
    tile3 = lambda r: pl.BlockSpec((1, r, TT), lambda i: (i, 0, 0))
    nat = lambda c: pl.BlockSpec((TT, c), lambda i: (i, 0))
    vtile = pl.BlockSpec((1, NSA_GROUPS, VROWS, TT), lambda i: (i, 0, 0, 0))
    return pl.pallas_call(
        functools.partial(_nsa_inproj_body, seq_tiles),
        grid=(nt,),
        in_specs=[
            nat(D_MODEL),
            full(_NSA_T_ROWS, D_MODEL),
            full(D_MODEL, _NSA_N_COLS),
            full(N_BRANCH * N_HEADS, 1),
            pl.BlockSpec((1, HEAD_DIM, TT), lambda i: (i % seq_tiles, 0, 0)),
            pl.BlockSpec((1, HEAD_DIM, TT), lambda i: (i % seq_tiles, 0, 0)),
            pl.BlockSpec((TT, width), lambda i: (i % seq_tiles, 0)),
            pl.BlockSpec((TT, width), lambda i: (i % seq_tiles, 0)),
        ],
        out_specs=[tile3(HQ), tile3(HQ), vtile, vtile, tile3(N_BRANCH * N_HEADS),
                   nat(width), nat(width), nat(NSA_KV), nat(NSA_KV)],
        out_shape=[
            jax.ShapeDtypeStruct((nt, HQ, TT), BF16),
            jax.ShapeDtypeStruct((nt, HQ, TT), BF16),
            jax.ShapeDtypeStruct((nt, NSA_GROUPS, VROWS, TT), BF16),
            jax.ShapeDtypeStruct((nt, NSA_GROUPS, VROWS, TT), BF16),
            jax.ShapeDtypeStruct((nt, N_BRANCH * N_HEADS, TT), F32),
            jax.ShapeDtypeStruct((n, width), BF16),
            jax.ShapeDtypeStruct((n, width), BF16),
            jax.ShapeDtypeStruct((n, NSA_KV), F32),
            jax.ShapeDtypeStruct((n, NSA_KV), F32),
        ],
        compiler_params=_cparams(("parallel",)),
        name="nsa_inproj",
    )(x, wt, wn, bg, cost, sint, cosn, sinn)


def _pad_heads(w, n_heads):
    w3 = w.reshape(w.shape[0], n_heads, HEAD_DIM)
    return jnp.concatenate([w3, jnp.zeros_like(w3)], axis=-1).reshape(w.shape[0], n_heads * AUG)


def _nsa_inproj_weights(w_in, b_gate):
    cuts = [HQ + i * NSA_KV for i in range(7)]
    wq, wkc, wvc, wks, wvs, wkw, wvw, wg = jnp.split(w_in, cuts, axis=1)
    wt = jnp.concatenate([wq, wvs, wvw, wg], axis=1).T.astype(BF16)
    wn = jnp.concatenate([wkc, wvc, _pad_heads(wks, NSA_GROUPS), _pad_heads(wkw, NSA_GROUPS)],
                         axis=1).astype(BF16)
    return wt, wn, b_gate.reshape(-1, 1)


def _rope_tables(seq_len):
    inv = ROPE_THETA ** (-jnp.arange(0, HEAD_DIM, 2, dtype=F32) / HEAD_DIM)
    ang = jnp.arange(seq_len, dtype=F32)[:, None] * inv[None, :]
    ang = jnp.concatenate([ang, ang], axis=-1)
    cos, sin = jnp.cos(ang), jnp.sin(ang)
    seq_tiles = seq_len // TT

    def transposed(t):
        return t.reshape(seq_tiles, TT, HEAD_DIM).transpose(0, 2, 1)

    def natural(t):
        return jnp.tile(jnp.concatenate([t, jnp.zeros_like(t)], axis=1), (1, NSA_GROUPS))

    return transposed(cos), transposed(sin), natural(cos), natural(sin)


def _compress_body(hk_ref, hv_ref, pe_ref, w1k_ref, b1k_ref, w2k_ref, w1vt_ref, b1v_ref, w2vt_ref,
                   kc_ref, vct_ref):
    nc = hk_ref.shape[1]
    pe_top, pe_bot = pe_ref[0:1], pe_ref[1:2]
    hk = hk_ref[0]
    top = _dot((hk + pe_top).astype(BF16), w1k_ref[0])
    bot = _dot((hk + pe_bot).astype(BF16), w1k_ref[1])
    hid = _gelu_tanh(top + pltpu.roll(bot, nc - 1, axis=0) + b1k_ref[...])
    kc_ref[0] = _dot(hid.astype(BF16), w2k_ref[...]).astype(BF16)

    hv = hv_ref[0]
    top_t = _dot_nt(w1vt_ref[0], (hv + pe_top).astype(BF16))
    bot_t = _dot_nt(w1vt_ref[1], (hv + pe_bot).astype(BF16))
    hid_t = _gelu_tanh(top_t + pltpu.roll(bot_t, nc - 1, axis=1) + b1v_ref[...])
    vct_ref[0] = _dot(w2vt_ref[...], hid_t.astype(BF16)).astype(BF16)


def _compress(hk, hv, pe2, w1k, b1k, w2k, w1vt, b1v, w2vt):
    bg, nc, hw = hk.shape
    cst = lambda shape: pl.BlockSpec(shape, lambda i: (0,) * len(shape))
    return pl.pallas_call(
        _compress_body,
        grid=(bg,),
        in_specs=[
            pl.BlockSpec((1, nc, hw), lambda i: (i, 0, 0)),
            pl.BlockSpec((1, nc, hw), lambda i: (i, 0, 0)),
            cst((2, hw)),
            cst((2, hw, PHI_HIDDEN)), cst((1, PHI_HIDDEN)), cst((PHI_HIDDEN, HEAD_DIM)),
            cst((2, PHI_HIDDEN, hw)), cst((PHI_HIDDEN, 1)), cst((HEAD_DIM, PHI_HIDDEN)),
        ],
        out_specs=[pl.BlockSpec((1, nc, HEAD_DIM), lambda i: (i, 0, 0)),
                   pl.BlockSpec((1, HEAD_DIM, nc), lambda i: (i, 0, 0))],
        out_shape=[jax.ShapeDtypeStruct((bg, nc, HEAD_DIM), BF16),
                   jax.ShapeDtypeStruct((bg, HEAD_DIM, nc), BF16)],
        compiler_params=_cparams(("parallel",)),
        name="nsa_compress",
    )(hk, hv, pe2, w1k, b1k, w2k, w1vt, b1v, w2vt)


def _slab_max(s):
    return jnp.max(s.reshape(-1, 8, s.shape[-1]), axis=0)


def _slab_sum(p):
    return jnp.sum(p.reshape(-1, 8, p.shape[-1]), axis=0)


def _paired_sweep(nq, qi, q_of, k_tile, v_tile, bias_row, s_ref, c_ref):
    tpos = lax.broadcasted_iota(jnp.int32, (1, TT), 1)
    kpos = lax.broadcasted_iota(jnp.int32, (TT, 1), 0)
    causal = kpos <= tpos
    half = nq // 2
    slots = []
    for s in range(nq + 1):
        if s < half:
            lo = s <= qi
            slots.append((lo, jnp.where(lo, s, s - qi - 1), jnp.where(lo, 0, 1)))
        else:
            slots.append((False, s - qi - 1, 1))

    slot_max, bias = [], []
    for s, (lo, kv, x) in enumerate(slots):
        sc = _dot(k_tile(kv), q_of(x))
        if s == nq:
            sc = jnp.where(causal, sc, MASK_BIAS)
        s_ref[s] = sc
        b = bias_row(lo, kv)
        bias.append(b)
        slot_max.append(_slab_max(sc) if b is None else _slab_max(sc) + b)
        yield 'A'
    diag = jnp.where(causal, s_ref[qi], MASK_BIAS)
    s_ref[qi] = diag
    m_lo = _slab_max(diag)
    m_hi = slot_max[nq]
    for s in range(nq):
        if s < half:
            m_lo = jnp.maximum(m_lo, jnp.where(s < qi, slot_max[s], MASK_BIAS))
            m_hi = jnp.maximum(m_hi, jnp.where(s > qi, slot_max[s], MASK_BIAS))
        else:
            m_hi = jnp.maximum(m_hi, slot_max[s])
    m_lo = jnp.max(m_lo, axis=0, keepdims=True)
    m_hi = jnp.max(m_hi, axis=0, keepdims=True)
    yield 'M'

    acc_lo = jnp.zeros((OROWS, TT), F32)
    acc_hi = jnp.zeros((OROWS, TT), F32)
    for s, (lo, kv, x) in enumerate(slots):
        m_row = m_hi if lo is False else jnp.where(lo, m_lo, m_hi)
        if bias[s] is not None:
            m_row = m_row - bias[s]
        p = jnp.exp2(s_ref[s] - m_row)
        pv = _dot(v_tile(kv), p.astype(BF16))[0:OROWS]
        if lo is False:
            acc_hi = acc_hi + pv
        else:
            c_ref[s] = pv
        yield 'B'
    for s in range(half):
        lo, c = slots[s][0], c_ref[s]
        acc_lo = acc_lo + jnp.where(lo, c, 0.0)
        acc_hi = acc_hi + jnp.where(lo, 0.0, c)
    o_lo = acc_lo[0:HEAD_DIM] * (1.0 / acc_lo[HEAD_DIM:HEAD_DIM + 1])
    o_hi = acc_hi[0:HEAD_DIM] * (1.0 / acc_hi[HEAD_DIM:HEAD_DIM + 1])
    return o_lo, o_hi


def _interleave(sweeps, width=SWEEP_WIDTH):
    outs = [None] * len(sweeps)
    groups = [list(range(i, min(i + width, len(sweeps)))) for i in range(0, len(sweeps), width)]

    def step(group, until):
        done = True
        for i in group:
            if state[i] == until:
                continue
            try:
                state[i] = next(sweeps[i])
            except StopIteration as fin:
                outs[i] = fin.value
                state[i] = 'END'
            done = done and state[i] == until
        return done

    state = [None] * len(sweeps)
    while not step(groups[0], 'M'):
        pass
    for g, group in enumerate(groups):
        nxt = groups[g + 1] if g + 1 < len(groups) else []
        cur_done, nxt_done = False, not nxt
        while not (cur_done and nxt_done):
            if not nxt_done:
                nxt_done = step(nxt, 'M')
            if not cur_done:
                cur_done = step(group, 'END')
    return outs


def _nsa_side(n_sb, qt, clamp, x, qn_ref, qr_ref, kc_ref, vct_ref, ovt_ref, kw_ref, vw_ref, s_refs):
    t0 = qt * TT
    tpos = t0 + lax.broadcasted_iota(jnp.int32, (1, TT), 1)
    kpos0 = lax.broadcasted_iota(jnp.int32, (TT, 1), 0)
    nc = kc_ref.shape[1]

    kc = kc_ref[0]
    cmp_end = lax.broadcasted_iota(jnp.int32, (nc, 1), 0) * CMP_STRIDE + (CMP_LEN - 1)
    cmask = cmp_end <= tpos
    c_slots = [s_refs[r].at[6 + x, 0:nc] for r in range(NSA_REP)]
    maxima, inv_sums = [], []
    for r in range(NSA_REP):
        s = jnp.where(cmask, _dot(kc, qn_ref[0, r * HEAD_DIM:(r + 1) * HEAD_DIM, :]), -jnp.inf)
        c_slots[r][...] = s
        m = jnp.max(_slab_max(s), axis=0, keepdims=True)
        maxima.append(jnp.where(m == -jnp.inf, 0.0, m))
        yield
    for r in range(NSA_REP):
        e = jnp.exp2(c_slots[r][...] - maxima[r])
        c_slots[r][...] = e
        d = jnp.sum(_slab_sum(e), axis=0, keepdims=True)
        inv_sums.append(1.0 / jnp.where(d > 0, d, 1.0))
        yield
    psum = jnp.zeros((nc, TT), F32)
    o_cmp = []
    for r in range(NSA_REP):
        p = c_slots[r][...] * inv_sums[r]
        psum = psum + p
        o_cmp.append(_dot(vct_ref[0], p.astype(BF16)))
        yield
    p_hi = psum.astype(BF16)
    p_lo = (psum - p_hi.astype(F32)).astype(BF16)
    imp = _dot(ovt_ref[...], p_hi) + _dot(ovt_ref[...], p_lo)

    nrow = ovt_ref.shape[0]
    blk = lax.broadcasted_iota(jnp.int32, (nrow, 1), 0)
    cur = tpos // SEL_LEN
    forced = (blk == 0) | (blk == cur) | (blk == cur - 1)
    valid = blk * SEL_LEN <= tpos
    taken = -3.0
    score = jnp.where(forced, taken, jnp.where(valid, imp, -1.0))
    score = jnp.where(blk < n_sb, score, -2.0)
    blk_f = blk.astype(F32)

    j1 = jnp.maximum(qt - 1, 0) if clamp else qt - 1
    j2 = jnp.maximum(qt - 2, 0) if clamp else qt - 2
    in_window = tpos - (j2 * TT + kpos0) < WINDOW
    masks = (kpos0 <= tpos - t0,
             jnp.broadcast_to(qt >= 1, (TT, TT)) if clamp else None,
             ((qt >= 2) & in_window) if clamp else in_window)
    k_tiles = [kw_ref[0, pl.ds(pl.multiple_of(j * TT, TT), TT), :][:, 0:HEAD_DIM] for j in (qt, j1, j2)]
    v_tiles = [vw_ref[0, j] for j in (qt, j1, j2)]
    w_max, o_win = [], []

    def window_scores(r):
        q = qr_ref[0, r * HEAD_DIM:(r + 1) * HEAD_DIM, :]
        maxima = []
        for t, (k, mask) in enumerate(zip(k_tiles, masks)):
            s = _dot(k, q)
            s = s if mask is None else jnp.where(mask, s, MASK_BIAS)
            s_refs[r][3 * x + t] = s
            maxima.append(_slab_max(s))
        w_max.append(jnp.max(functools.reduce(jnp.maximum, maxima), axis=0, keepdims=True))

    def window_output(r):
        acc = sum(_dot(v, jnp.exp2(s_refs[r][3 * x + t] - w_max[r]).astype(BF16))[0:OROWS]
                  for t, v in enumerate(v_tiles))
        o_win.append(acc[0:HEAD_DIM] * (1.0 / acc[HEAD_DIM:HEAD_DIM + 1]))

    window_steps = ([functools.partial(window_scores, r) for r in range(NSA_REP)]
                    + [functools.partial(window_output, r) for r in range(NSA_REP)])
    n_round = max(min(N_SEL, n_sb) - 3, 0)
    per_step = max(n_round // len(window_steps), 1)

    def over_blocks(pair_op, v):
        slabs = v.reshape(-1, 8, TT)
        v = functools.reduce(pair_op, [slabs[i] for i in range(slabs.shape[0])])
        for shift in (4, 2, 1):
            v = pair_op(v, pltpu.roll(v, shift, axis=0))
        return jnp.tile(v, (nrow // 8, 1))

    for i in range(n_round):
        top = over_blocks(jnp.maximum, score)
        first = over_blocks(jnp.minimum, jnp.where(score == top, blk_f, float(nrow)))
        score = jnp.where(blk_f == first, taken, score)
        if i % per_step == per_step - 1 and window_steps:
            window_steps.pop(0)()
        yield
    while window_steps:
        window_steps.pop(0)()
    sel_bias = jnp.where(score == taken, 0.0, MASK_BIAS).astype(BF16)
    return o_cmp, o_win, sel_bias


def _nsa_attn_body(nq, n_sb, qnl_ref, qnh_ref, qrl_ref, qrh_ref, kc_ref, vct_ref, ovt_ref, ks_ref, vs_ref,
                   kw_ref, vw_ref, gl_ref, gh_ref, olo_ref, ohi_ref, q_ref, *scratch):
    qi = pl.program_id(2)
    s_refs, c_refs = scratch[:NSA_REP], scratch[NSA_REP:]
    side_in = ((qi, qnl_ref, qrl_ref), (nq - 1 - qi, qnh_ref, qrh_ref))
    gens = [_nsa_side(n_sb, qt, x == 0 or nq < 6, x, qn_ref, qr_ref, kc_ref, vct_ref, ovt_ref, kw_ref, vw_ref,
                      s_refs) for x, (qt, qn_ref, qr_ref) in enumerate(side_in)]
    sides = [None, None]
    while None in sides:
        for x, gen in enumerate(gens):
            if sides[x] is None:
                try:
                    next(gen)
                except StopIteration as done:
                    sides[x] = done.value
    for x, (_, _, qr_ref) in enumerate(side_in):
        for r in range(NSA_REP):
            q_ref[x, r, 0:HEAD_DIM, :] = qr_ref[0, r * HEAD_DIM:(r + 1) * HEAD_DIM, :]
            q_ref[x, r, HEAD_DIM:AUG, :] = sides[x][2]

    k_tile = lambda kv: ks_ref[0, pl.ds(pl.multiple_of(kv * TT, TT), TT), :]
    v_tile = lambda kv: vs_ref[0, kv]
    o_sel = _interleave([_paired_sweep(nq, qi, lambda x, r=r: q_ref[x, r], k_tile, v_tile,
                                       lambda lo, kv: None, s_refs[r], c_refs[r]) for r in range(NSA_REP)])

    for x, (g_ref, o_ref) in enumerate(((gl_ref, olo_ref), (gh_ref, ohi_ref))):
        o_cmp, o_win, _ = sides[x]
        g = g_ref[0, 0]
        rows = []
        for r in range(NSA_REP):
            gc = g[r * N_BRANCH + 0:r * N_BRANCH + 1]
            gs = g[r * N_BRANCH + 1:r * N_BRANCH + 2]
            gw = g[r * N_BRANCH + 2:r * N_BRANCH + 3]
            rows.append(gc * o_cmp[r] + gs * o_sel[r][x] + gw * o_win[r])
        o_ref[0] = jnp.concatenate(rows, axis=0).T.astype(BF16)


def _nsa_attn(batch, seq_len, qn, qr, kc, vct, ovt, ksa, vs, kwa, vw, gt):
    nq = seq_len // TT
    nc = kc.shape[1]
    n_sb = seq_len // SEL_LEN
    g_rows = NSA_REP * N_BRANCH
    half = nq // 2
    ksa3 = ksa.reshape(batch, seq_len, NSA_GROUPS * AUG)
    kwa3 = kwa.reshape(batch, seq_len, NSA_GROUPS * AUG)
    vs5 = vs.reshape(batch, nq, NSA_GROUPS, VROWS, TT)
    vw5 = vw.reshape(batch, nq, NSA_GROUPS, VROWS, TT)
    gt4 = gt.reshape(batch * nq, NSA_GROUPS, g_rows, TT)
    lo_tile = lambda b, q: b * nq + q
    hi_tile = lambda b, q: b * nq + nq - 1 - q
    q_lo = pl.BlockSpec((1, NSA_REP * HEAD_DIM, TT), lambda b, g, q: (lo_tile(b, q), g, 0))
    q_hi = pl.BlockSpec((1, NSA_REP * HEAD_DIM, TT), lambda b, g, q: (hi_tile(b, q), g, 0))
    kspec = pl.BlockSpec((1, seq_len, AUG), lambda b, g, q: (b, 0, g))
    vspec = pl.BlockSpec((1, nq, None, VROWS, TT), lambda b, g, q: (b, 0, g, 0, 0))
    out = jax.ShapeDtypeStruct((batch, seq_len // 2, HQ), BF16)
    return pl.pallas_call(
        functools.partial(_nsa_attn_body, nq, n_sb),
        grid=(batch, NSA_GROUPS, half),
        in_specs=[
            q_lo, q_hi, q_lo, q_hi,
            pl.BlockSpec((1, nc, HEAD_DIM), lambda b, g, q: (b * NSA_GROUPS + g, 0, 0)),
            pl.BlockSpec((1, HEAD_DIM, nc), lambda b, g, q: (b * NSA_GROUPS + g, 0, 0)),
            pl.BlockSpec(ovt.shape, lambda b, g, q: (0, 0)),
            kspec, vspec, kspec, vspec,
            pl.BlockSpec((1, 1, g_rows, TT), lambda b, g, q: (lo_tile(b, q), g, 0, 0)),
            pl.BlockSpec((1, 1, g_rows, TT), lambda b, g, q: (hi_tile(b, q), g, 0, 0)),
        ],
        out_specs=[pl.BlockSpec((1, TT, NSA_REP * HEAD_DIM), lambda b, g, q: (b, q, g)),
                   pl.BlockSpec((1, TT, NSA_REP * HEAD_DIM), lambda b, g, q: (b, half - 1 - q, g))],
        out_shape=[out, out],
        scratch_shapes=([pltpu.VMEM((2, NSA_REP, AUG, TT), BF16)]
                        + [pltpu.VMEM((nq + 1, TT, TT), F32)] * NSA_REP
                        + [pltpu.VMEM((half, OROWS, TT), F32)] * NSA_REP),
        compiler_params=_cparams(("parallel", "parallel", "arbitrary")),
        name="nsa_attn",
    )(qn, qn, qr, qr, kc, vct, ovt, ksa3, vs5, kwa3, vw5, gt4, gt4)


def _overlap_t(seq_len, nc):
    n_cmp = (seq_len - CMP_LEN) // CMP_STRIDE + 1
    n_sb = seq_len // SEL_LEN
    cmp_start = jnp.arange(n_cmp) * CMP_STRIDE
    sel_start = jnp.arange(n_sb) * SEL_LEN
    ov = jnp.clip(jnp.minimum(cmp_start[:, None] + CMP_LEN, sel_start[None, :] + SEL_LEN)
                  - jnp.maximum(cmp_start[:, None], sel_start[None, :]), 0, None).astype(F32) / CMP_LEN
    out = jnp.zeros((SEL_LEN, nc), F32).at[:n_sb, :n_cmp].set(ov.T)
    return out.astype(BF16)


def _nsa_attention(x, batch, seq_len, w_in, b_gate, pe, k_w1, k_b1, k_w2, v_w1, v_b1, v_w2, tables):
    wt, wn, bg = _nsa_inproj_weights(w_in, b_gate)
    qn, qr, vs, vw, gt, ksa, kwa, kc, vc = _nsa_inproj(x, seq_len, wt, wn, bg, *tables)

    nc = seq_len // CMP_STRIDE
    hw = CMP_STRIDE * HEAD_DIM

    def half_blocks(t):
        t = t.reshape(batch, seq_len, NSA_GROUPS, HEAD_DIM).transpose(0, 2, 1, 3)
        return t.reshape(batch * NSA_GROUPS, nc, hw)

    kcmp, vcmp_t = _compress(
        half_blocks(kc), half_blocks(vc), pe.reshape(2, hw),
        k_w1.reshape(2, hw, PHI_HIDDEN).astype(BF16), k_b1.reshape(1, -1), k_w2.astype(BF16),
        v_w1.reshape(2, hw, PHI_HIDDEN).transpose(0, 2, 1).astype(BF16), v_b1.reshape(-1, 1),
        v_w2.T.astype(BF16))
    return _nsa_attn(batch, seq_len, qn, qr, kcmp, vcmp_t, _overlap_t(seq_len, nc), ksa, vs, kwa, vw, gt)


def _nsa_layer(x, batch, seq_len, w_in, b_gate, pe, k_w1, k_b1, k_w2, v_w1, v_b1, v_w2, w_o,
               ln_g, ln_b, tables):
    a_lo, a_hi = _nsa_attention(x, batch, seq_len, w_in, b_gate, pe, k_w1, k_b1, k_w2, v_w1, v_b1, v_w2,
                                tables)
    return _proj_ln(a_lo, a_hi, w_o.astype(BF16), x, ln_g.reshape(1, -1), ln_b.reshape(1, -1))


_FOX_T_ROWS = 2 * HQ + N_HEADS
_FOX_N_COLS = N_HEADS * AUG + 128
_N_PIECE = 3


def _fox_inproj_body(seq_tiles, x_ref, wt_ref, wn_ref, bft_ref, bfn_ref, place_ref, ones_ref, route_ref,
                     qa_ref, ka_ref, vt_ref, off_ref, run_ref):
    i = pl.program_id(0)
    xb = x_ref[...].astype(BF16)
    t = _dot_nt(wt_ref[...], xb)
    n = _dot(xb, wn_ref[...])
    vt_ref[0, :, 0:HEAD_DIM, :] = t[HQ:2 * HQ].reshape(N_HEADS, HEAD_DIM, TT).astype(BF16)
    vt_ref[0, :, HEAD_DIM:VROWS, :] = jnp.ones((N_HEADS, VROWS - HEAD_DIM, TT), BF16)
    lf_t = _log_sigmoid(t[2 * HQ:] + bft_ref[...]) * LOG2E
    kw = N_HEADS * AUG
    lf_n = _log_sigmoid(n[:, kw:kw + _N_PIECE * N_HEADS] + bfn_ref[...]) * LOG2E

    r_i = lax.broadcasted_iota(jnp.int32, (TT, TT), 0)
    c_i = lax.broadcasted_iota(jnp.int32, (TT, TT), 1)
    upper = ((r_i > 0) & (r_i <= c_i)).astype(BF16)
    lower = ((c_i > 0) & (c_i <= r_i)).astype(BF16)
    a_t = sum(_dot(p, upper) for p in _split3(lf_t))
    a_n = sum(_dot(lower, p) for p in _split3(lf_n))

    q3 = (t[0:HQ] * QSCALE).reshape(N_HEADS, HEAD_DIM, TT)
    qa_ref[0, :, 0:HEAD_DIM, :] = q3.astype(BF16)
    stacked = jnp.concatenate(list(_split3(a_t)) + [jnp.ones((N_HEADS, TT), BF16)], axis=0)
    for h in range(N_HEADS):
        qa_ref[0, h, HEAD_DIM:AUG, :] = _dot(route_ref[h], stacked).astype(BF16)

    b1, b2, b3 = _split3(-a_n)
    grp = lax.broadcasted_iota(jnp.int32, (TT, _N_PIECE * N_HEADS), 1) // N_HEADS
    bsel = jnp.where(grp == 0, b1, jnp.where(grp == 1, b2, b3))
    ka_ref[...] = (n[:, 0:kw] + _dot(bsel, place_ref[...]) + ones_ref[...]).astype(BF16)

    @pl.when(i % seq_tiles == 0)
    def _():
        run_ref[...] = jnp.zeros_like(run_ref)

    first = lf_t[:, 0:1]
    off_ref[0] = jnp.broadcast_to(run_ref[:, 0:1] + first, (N_HEADS, TT))
    run_ref[...] = run_ref[...] + (a_t[:, TT - 1:TT] + first)


def _fox_inproj(x, seq_len, wt, wn, bft, bfn, place, ones, route):
    n = x.shape[0]
    nt = n // TT
    kw = N_HEADS * AUG
    full = lambda r, c: pl.BlockSpec((r, c), lambda i: (0, 0))
    return pl.pallas_call(
        functools.partial(_fox_inproj_body, seq_len // TT),
        grid=(nt,),
        in_specs=[
            pl.BlockSpec((TT, D_MODEL), lambda i: (i, 0)),
            full(_FOX_T_ROWS, D_MODEL), full(D_MODEL, _FOX_N_COLS),
            full(N_HEADS, 1), full(1, _N_PIECE * N_HEADS),
            full(_N_PIECE * N_HEADS, kw), full(1, kw),
            pl.BlockSpec((N_HEADS, AUG - HEAD_DIM, AUG - HEAD_DIM), lambda i: (0, 0, 0)),
        ],
        out_specs=[
            pl.BlockSpec((1, N_HEADS, AUG, TT), lambda i: (i, 0, 0, 0)),
            pl.BlockSpec((TT, kw), lambda i: (i, 0)),
            pl.BlockSpec((1, N_HEADS, VROWS, TT), lambda i: (i, 0, 0, 0)),
            pl.BlockSpec((1, N_HEADS, TT), lambda i: (i, 0, 0)),
        ],
        out_shape=[
            jax.ShapeDtypeStruct((nt, N_HEADS, AUG, TT), BF16),
            jax.ShapeDtypeStruct((n, kw), BF16),
            jax.ShapeDtypeStruct((nt, N_HEADS, VROWS, TT), BF16),
            jax.ShapeDtypeStruct((nt, N_HEADS, TT), F32),
        ],
        scratch_shapes=[pltpu.VMEM((N_HEADS, 128), F32)],
        compiler_params=_cparams(("arbitrary",)),
        name="fox_inproj",
    )(x, wt, wn, bft, bfn, place, ones, route)


def _fox_inproj_weights(w_in, b_f):
    wq, wk, wv, wf = jnp.split(w_in, [HQ, 2 * HQ, 3 * HQ], axis=1)
    wt = jnp.concatenate([wq, wv, wf], axis=1).T.astype(BF16)
    pad = jnp.zeros((D_MODEL, _FOX_N_COLS - N_HEADS * AUG - _N_PIECE * N_HEADS), F32)
    wn = jnp.concatenate([_pad_heads(wk, N_HEADS)] + [wf] * _N_PIECE + [pad], axis=1).astype(BF16)
    bft = b_f.reshape(-1, 1)
    bfn = jnp.tile(b_f, _N_PIECE).reshape(1, -1)
    rows = jnp.arange(_N_PIECE * N_HEADS)
    cols = (rows % N_HEADS) * AUG + HEAD_DIM + _N_PIECE + rows // N_HEADS
    place = jnp.zeros((_N_PIECE * N_HEADS, N_HEADS * AUG), F32).at[rows, cols].set(1.0).astype(BF16)
    lane = jnp.arange(N_HEADS * AUG) % AUG
    ones = ((lane >= HEAD_DIM) & (lane < HEAD_DIM + _N_PIECE)).astype(F32).reshape(1, -1)
    hh = jnp.arange(N_HEADS)
    route = jnp.zeros((N_HEADS, AUG - HEAD_DIM, AUG - HEAD_DIM), F32)
    for k in range(_N_PIECE):
        route = route.at[hh, k, k * N_HEADS + hh].set(1.0)
        route = route.at[hh, _N_PIECE + k, _N_PIECE * N_HEADS].set(1.0)
    return wt, wn, bft, bfn, place, ones, route.astype(BF16)


def _fox_attn_body(nq, qlo_ref, qhi_ref, ka_ref, vt_ref, off_ref, olo_ref, ohi_ref, q_ref, *scratch):
    hg = pl.program_id(1)
    qi = pl.program_id(2)
    s_refs, c_refs = scratch[:N_SCORE_BUF], scratch[N_SCORE_BUF:]
    q_ref[0] = qlo_ref[0]
    q_ref[1] = qhi_ref[0]
    sweeps = []
    for r in range(HEAD_BLOCK):
        h = hg * HEAD_BLOCK + r

        def off_row(t, h=h):
            return off_ref[0, pl.ds(t, 1), pl.ds(h, 1), :].reshape(1, TT)

        off_lo, off_hi = off_row(qi), off_row(nq - 1 - qi)

        def bias_row(lo, kv, off_lo=off_lo, off_hi=off_hi, off_row=off_row):
            base = off_hi if lo is False else jnp.where(lo, off_lo, off_hi)
            return base - off_row(kv)

        def k_tile(kv, r=r):
            return ka_ref[0, pl.ds(pl.multiple_of(kv * TT, TT), TT), r * AUG:(r + 1) * AUG]

        def v_tile(kv, r=r):
            return vt_ref[0, kv, r]

        sweeps.append(_paired_sweep(nq, qi, lambda x, r=r: q_ref[x, r], k_tile, v_tile, bias_row,
                                    s_refs[r % N_SCORE_BUF], c_refs[r]))
    outs = _interleave(sweeps)
    for x, o_ref in enumerate((olo_ref, ohi_ref)):
        o_ref[0] = jnp.concatenate([o[x] for o in outs], axis=0).T.astype(BF16)


def _fox_attn(batch, seq_len, qa, ka, vt, off):
    nq = seq_len // TT
    half = nq // 2
    hb = HEAD_BLOCK
    ka3 = ka.reshape(batch, seq_len, N_HEADS * AUG)
    vt5 = vt.reshape(batch, nq, N_HEADS, VROWS, TT)
    off4 = off.reshape(batch, nq, N_HEADS, TT)
    out = jax.ShapeDtypeStruct((batch, seq_len // 2, HQ), BF16)
    return pl.pallas_call(
        functools.partial(_fox_attn_body, nq),
        grid=(batch, N_HEADS // hb, half),
        in_specs=[
            pl.BlockSpec((1, hb, AUG, TT), lambda b, h, q: (b * nq + q, h, 0, 0)),
            pl.BlockSpec((1, hb, AUG, TT), lambda b, h, q: (b * nq + nq - 1 - q, h, 0, 0)),
            pl.BlockSpec((1, seq_len, hb * AUG), lambda b, h, q: (b, 0, h)),
            pl.BlockSpec((1, nq, hb, VROWS, TT), lambda b, h, q: (b, 0, h, 0, 0)),
            pl.BlockSpec((1, nq, N_HEADS, TT), lambda b, h, q: (b, 0, 0, 0)),
        ],
        out_specs=[pl.BlockSpec((1, TT, hb * HEAD_DIM), lambda b, h, q: (b, q, h)),
                   pl.BlockSpec((1, TT, hb * HEAD_DIM), lambda b, h, q: (b, half - 1 - q, h))],
        out_shape=[out, out],
        scratch_shapes=([pltpu.VMEM((2, hb, AUG, TT), BF16)]
                        + [pltpu.VMEM((nq + 1, TT, TT), F32)] * N_SCORE_BUF
                        + [pltpu.VMEM((half, OROWS, TT), F32)] * hb),
        compiler_params=_cparams(("parallel", "parallel", "arbitrary")),
        name="fox_attn",
    )(qa, qa, ka3, vt5, off4)


def _fox_layer(x, batch, seq_len, w_in, b_f, w_o, ln_g, ln_b):
    qa, ka, vt, off = _fox_inproj(x, seq_len, *_fox_inproj_weights(w_in, b_f))
    a_lo, a_hi = _fox_attn(batch, seq_len, qa, ka, vt, off)
    return _proj_ln(a_lo, a_hi, w_o.astype(BF16), x, ln_g.reshape(1, -1), ln_b.reshape(1, -1))


def kernel(x, nsa_w_in, nsa_b_gate, nsa_pe, nsa_phik_w1, nsa_phik_b1, nsa_phik_w2, nsa_phiv_w1,
           nsa_phiv_b1, nsa_phiv_w2, nsa_w_o, fox_w_in, fox_b_f, fox_w_o, ffn_w_up, ffn_conv_w,
           ffn_conv_b, ffn_w_down, ln1_g, ln1_b, ln2_g, ln2_b):
    batch, seq_len, d = x.shape
    assert d == D_MODEL and seq_len % (2 * PROJ_TM) == 0 and seq_len % FFN_TM == 0 and seq_len % (2 * TT) == 0
    assert seq_len // SEL_LEN <= AUG - HEAD_DIM
    tables = _rope_tables(seq_len)
    h = x.reshape(batch * seq_len, d)
    for i in range(DEPTH):
        j = i // 2
        if i % 2 == 0:
            h = _nsa_layer(h, batch, seq_len, nsa_w_in[j], nsa_b_gate[j], nsa_pe[j], nsa_phik_w1[j],
                           nsa_phik_b1[j], nsa_phik_w2[j], nsa_phiv_w1[j], nsa_phiv_b1[j],
                           nsa_phiv_w2[j], nsa_w_o[j], ln1_g[i], ln1_b[i], tables)
        else:
            h = _fox_layer(h, batch, seq_len, fox_w_in[j], fox_b_f[j], fox_w_o[j], ln1_g[i], ln1_b[i])
        h = _ffn(h, seq_len, *_ffn_weights(ffn_w_up[i], ffn_conv_w[i], ffn_conv_b[i], ffn_w_down[i]),
                 ln2_g[i].reshape(1, -1), ln2_b[i].reshape(1, -1))
    return h.reshape(batch, seq_len, d)
```

```python
import functools
import math

import jax
import jax.numpy as jnp
from jax import lax
from jax.experimental import pallas as pl
from jax.experimental.pallas import tpu as pltpu

F32 = jnp.float32
BF16 = jnp.bfloat16

D_MODEL = 1024
DEPTH = 4
HEAD_DIM = 64
N_HEADS = 16
HQ = N_HEADS * HEAD_DIM
NSA_GROUPS = 4
NSA_REP = 4
NSA_KV = NSA_GROUPS * HEAD_DIM
CMP_LEN = 32
CMP_STRIDE = 16
SEL_LEN = 64
N_SEL = 16
WINDOW = 512
PHI_HIDDEN = 256
N_BRANCH = 3
D_FF = 2816
ROPE_THETA = 10000.0
ALPHA = (2 * DEPTH) ** 0.25
LN_EPS = 1e-5
SCALE = HEAD_DIM ** -0.5
LOG2E = math.log2(math.e)
QSCALE = SCALE * LOG2E
VROWS = 80
OROWS = 72

TT = 256
AUG = 128
MASK_BIAS = -1e30
FF_CHUNK = 256
FFN_TM = 256
PROJ_TM = 1024
HEAD_BLOCK = 8
N_SCORE_BUF = 3
SWEEP_WIDTH = 1
HALO = 16
VMEM_LIMIT = 56 * 1024 * 1024


def _cparams(sem):
    return pltpu.CompilerParams(dimension_semantics=sem, vmem_limit_bytes=VMEM_LIMIT)


def _layer_norm(z, g, b):
    mu = jnp.mean(z, axis=-1, keepdims=True)
    zc = z - mu
    var = jnp.mean(zc * zc, axis=-1, keepdims=True)
    return zc * lax.rsqrt(var + LN_EPS) * g + b


def _gelu_tanh(x):
    c = math.sqrt(2.0 / math.pi)
    return x * (0.5 * (1.0 + jnp.tanh(c * (x + 0.044715 * (x * x * x)))))


def _log_sigmoid(z):
    return -(jnp.maximum(-z, 0.0) + jnp.log1p(jnp.exp(-jnp.abs(z))))


def _dot(a, b):
    return jnp.dot(a, b, preferred_element_type=F32)


def _dot_nt(a, b):
    return lax.dot_general(a, b, (((1,), (1,)), ((), ())), preferred_element_type=F32)


def _split3(x):
    p1 = x.astype(BF16)
    r1 = x - p1.astype(F32)
    p2 = r1.astype(BF16)
    p3 = (r1 - p2.astype(F32)).astype(BF16)
    return p1, p2, p3


def _proj_ln_body(half, lo_ref, hi_ref, w_ref, x_ref, g_ref, b_ref, o_ref):
    in_lo = (pl.program_id(0) % (2 * half)) < half
    a = jnp.where(in_lo, lo_ref[0], hi_ref[0])
    tm = a.shape[0]
    rows = [slice(r, r + tm // 4) for r in range(0, tm, tm // 4)]
    ys = [_dot(a[r], w_ref[...]) for r in rows]
    for r, y in zip(rows, ys):
        o_ref[r, :] = _layer_norm(ALPHA * x_ref[r, :] + y, g_ref[...], b_ref[...])


def _proj_ln(a_lo, a_hi, w, x, g, b):
    _, s_half, k = a_lo.shape
    n, d = x.shape
    tm = PROJ_TM
    half = s_half // tm
    return pl.pallas_call(
        functools.partial(_proj_ln_body, half),
        grid=(n // tm,),
        in_specs=[
            pl.BlockSpec((1, tm, k), lambda i: (i // (2 * half), jnp.minimum(i % (2 * half), half - 1), 0)),
            pl.BlockSpec((1, tm, k), lambda i: (i // (2 * half), jnp.maximum(i % (2 * half) - half, 0), 0)),
            pl.BlockSpec((k, d), lambda i: (0, 0)),
            pl.BlockSpec((tm, d), lambda i: (i, 0)),
            pl.BlockSpec((1, d), lambda i: (0, 0)),
            pl.BlockSpec((1, d), lambda i: (0, 0)),
        ],
        out_specs=pl.BlockSpec((tm, d), lambda i: (i, 0)),
        out_shape=jax.ShapeDtypeStruct((n, d), F32),
        compiler_params=_cparams(("parallel",)),
        name="proj_ln",
    )(a_lo, a_hi, w, x, g, b)


GELU_C0 = math.sqrt(2.0 / math.pi)
GELU_C1 = GELU_C0 * 0.044715


def _ffn_body(seq_tiles, x_ref, xp_ref, wa_ref, wb_ref, cwa_ref, cwb_ref, wd_ref, g_ref, b_ref, o_ref,
              perm_ref):
    i = pl.program_id(0)
    tm, d = x_ref.shape
    nj = tm // 8
    pitch = nj + 1
    n_lane = d // 128
    for s in range(8):
        for cb in range(n_lane):
            perm_ref[cb, pl.ds(s * pitch, nj), :] = x_ref[s * nj:(s + 1) * nj, cb * 128:(cb + 1) * 128]
    x = jnp.concatenate(
        [jnp.concatenate([perm_ref[cb, pl.ds(j, 8, stride=pitch), :] for cb in range(n_lane)], axis=1)
         for j in range(nj)], axis=0)
    starts_seq = (i % seq_tiles) == 0
    halo = jnp.where(starts_seq, 0.0, xp_ref[...]).astype(BF16)
    xcat = jnp.concatenate([halo, x.astype(BF16)], axis=0)
    n_chunks = wa_ref.shape[0]
    first_sublane = lax.broadcasted_iota(jnp.int32, (8, 1), 0) == 0

    def up(c):
        return _dot(xcat, wa_ref[c]), _dot(xcat, wb_ref[c])

    def conv(h, cw):
        h3 = h[HALO:].reshape(nj, 8, h.shape[-1])

        def wrap(slab, halo_row):
            return jnp.where(first_sublane, halo_row, pltpu.roll(slab, 1, axis=0))[None]

        prev1 = wrap(h3[nj - 1], h[HALO - 1:HALO])
        prev2 = wrap(h3[nj - 2], h[HALO - 2:HALO - 1])
        s1 = jnp.concatenate([prev1, h3[:-1]], axis=0)
        s2 = jnp.concatenate([prev2, prev1, h3[:-2]], axis=0)
        return (cw[0:1] * s2 + cw[1:2] * s1 + cw[2:3] * h3 + cw[3:4]).reshape(tm, h.shape[-1])

    y = None
    nxt = up(0)
    for c in range(n_chunks):
        cur, nxt = nxt, (up(c + 1) if c + 1 < n_chunks else None)
        ha = conv(cur[0], cwa_ref[c])
        act = ha + ha * jnp.tanh(ha * (GELU_C0 + GELU_C1 * (ha * ha)))
        hb = conv(cur[1], cwb_ref[c])
        gated = (act * hb).astype(BF16)
        part = _dot(gated, wd_ref[c])
        y = part if y is None else y + part
    out = _layer_norm(ALPHA * x + y, g_ref[...], b_ref[...])
    for j in range(nj):
        for cb in range(n_lane):
            perm_ref[cb, pl.ds(j, 8, stride=pitch), :] = out[j * 8:(j + 1) * 8, cb * 128:(cb + 1) * 128]
    for s in range(8):
        for cb in range(n_lane):
            o_ref[s * nj:(s + 1) * nj, cb * 128:(cb + 1) * 128] = perm_ref[cb, pl.ds(s * pitch, nj), :]


def _ffn(x, seq_len, wa, wb, cwa, cwb, wd, g, b):
    n, d = x.shape
    tm = FFN_TM
    nc = wa.shape[0]
    cf = wa.shape[2]
    hblk = tm // HALO
    return pl.pallas_call(
        functools.partial(_ffn_body, seq_len // tm),
        grid=(n // tm,),
        in_specs=[
            pl.BlockSpec((tm, d), lambda i: (i, 0)),
            pl.BlockSpec((HALO, d), lambda i: (jnp.maximum(i * hblk - 1, 0), 0)),
            pl.BlockSpec((nc, d, cf), lambda i: (0, 0, 0)),
            pl.BlockSpec((nc, d, cf), lambda i: (0, 0, 0)),
            pl.BlockSpec((nc, 4, cf), lambda i: (0, 0, 0)),
            pl.BlockSpec((nc, 4, cf), lambda i: (0, 0, 0)),
            pl.BlockSpec((nc, cf, d), lambda i: (0, 0, 0)),
            pl.BlockSpec((1, d), lambda i: (0, 0)),
            pl.BlockSpec((1, d), lambda i: (0, 0)),
        ],
        out_specs=pl.BlockSpec((tm, d), lambda i: (i, 0)),
        out_shape=jax.ShapeDtypeStruct((n, d), F32),
        scratch_shapes=[pltpu.VMEM((d // 128, tm + 8, 128), F32)],
        compiler_params=_cparams(("parallel",)),
        name="conv_ffn",
    )(x, x, wa, wb, cwa, cwb, wd, g, b)


def _ffn_weights(w_up, conv_w, conv_b, w_down):
    nc = D_FF // FF_CHUNK

    def up(w):
        return w.reshape(D_MODEL, nc, FF_CHUNK).transpose(1, 0, 2).astype(BF16)

    def taps(cw, cb):
        t = jnp.concatenate([cw, cb[None]], axis=0)
        return t.reshape(4, nc, FF_CHUNK).transpose(1, 0, 2)

    wa, wb = up(w_up[:, :D_FF]), up(w_up[:, D_FF:])
    cwa = taps(conv_w[:, :D_FF], conv_b[:D_FF])
    cwb = 0.5 * taps(conv_w[:, D_FF:], conv_b[D_FF:])
    wd = w_down.reshape(nc, FF_CHUNK, D_MODEL).astype(BF16)
    return wa, wb, cwa, cwb, wd


_NSA_T_ROWS = HQ + 2 * NSA_KV + N_BRANCH * N_HEADS
_NSA_N_COLS = 2 * NSA_KV + 2 * NSA_GROUPS * AUG


def _nsa_inproj_body(seq_tiles, x_ref, wt_ref, wn_ref, bg_ref, cost_ref, sint_ref, cosn_ref, sinn_ref,
                     qn_ref, qr_ref, vs_ref, vw_ref, gt_ref, ksa_ref, kwa_ref, kc_ref, vc_ref):
    i = pl.program_id(0)
    xb = x_ref[...].astype(BF16)
    t = _dot_nt(wt_ref[...], xb)
    n = _dot(xb, wn_ref[...])
    q = t[0:HQ] * QSCALE
    qn_ref[0] = q.astype(BF16)
    q3 = q.reshape(N_HEADS, HEAD_DIM, TT)
    half = HEAD_DIM // 2
    rot = jnp.concatenate([-q3[:, half:], q3[:, :half]], axis=1)
    qr = q3 * cost_ref[0][None] + rot * sint_ref[0][None]
    qr_ref[0] = qr.reshape(HQ, TT).astype(BF16)
    ones = jnp.ones((NSA_GROUPS, VROWS - HEAD_DIM, TT), BF16)
    for v_ref, lo in ((vs_ref, HQ), (vw_ref, HQ + NSA_KV)):
        v_ref[0, :, 0:HEAD_DIM, :] = t[lo:lo + NSA_KV].reshape(NSA_GROUPS, HEAD_DIM, TT).astype(BF16)
        v_ref[0, :, HEAD_DIM:VROWS, :] = ones
    gt_ref[0] = jax.nn.sigmoid(t[HQ + 2 * NSA_KV:] + bg_ref[...])

    kc_ref[...] = n[:, 0:NSA_KV]
    vc_ref[...] = n[:, NSA_KV:2 * NSA_KV]
    width = NSA_GROUPS * AUG
    lane = lax.broadcasted_iota(jnp.int32, (TT, width), 1) % AUG
    cosn, sinn = cosn_ref[...], sinn_ref[...]

    def rope_nat(k):
        rh = jnp.where(lane < half, -pltpu.roll(k, width - half, axis=1), pltpu.roll(k, half, axis=1))
        return k * cosn + rh * sinn

    ks = rope_nat(n[:, 2 * NSA_KV:2 * NSA_KV + width])
    kw = rope_nat(n[:, 2 * NSA_KV + width:])
    row = lax.broadcasted_iota(jnp.int32, (TT, width), 0)
    blk = ((i % seq_tiles) * TT + row) // SEL_LEN
    onehot = (lane - HEAD_DIM == blk).astype(F32)
    ksa_ref[...] = (ks + onehot).astype(BF16)
    kwa_ref[...] = kw.astype(BF16)


def _nsa_inproj(x, seq_len, wt, wn, bg, cost, sint, cosn, sinn):
    n = x.shape[0]
    nt = n // TT
    seq_tiles = seq_len // TT
    width = NSA_GROUPS * AUG
    full = lambda r, c: pl.BlockSpec((r, c), lambda i: (0, 0))
    tile3 = lambda r: pl.BlockSpec((1, r, TT), lambda i: (i, 0, 0))
    nat = lambda c: pl.BlockSpec((TT, c), lambda i: (i, 0))
    vtile = pl.BlockSpec((1, NSA_GROUPS, VROWS, TT), lambda i: (i, 0, 0, 0))
    return pl.pallas_call(
        functools.partial(_nsa_inproj_body, seq_tiles),
        grid=(nt,),
        in_specs=[
            nat(D_MODEL),
            full(_NSA_T_ROWS, D_MODEL),
            full(D_MODEL, _NSA_N_COLS),
            full(N_BRANCH * N_HEADS, 1),
            pl.BlockSpec((1, HEAD_DIM, TT), lambda i: (i % seq_tiles, 0, 0)),
            pl.BlockSpec((1, HEAD_DIM, TT), lambda i: (i % seq_tiles, 0, 0)),
            pl.BlockSpec((TT, width), lambda i: (i % seq_tiles, 0)),
            pl.BlockSpec((TT, width), lambda i: (i % seq_tiles, 0)),
        ],
        out_specs=[tile3(HQ), tile3(HQ), vtile, vtile, tile3(N_BRANCH * N_HEADS),
                   nat(width), nat(width), nat(NSA_KV), nat(NSA_KV)],
        out_shape=[
            jax.ShapeDtypeStruct((nt, HQ, TT), BF16),
            jax.ShapeDtypeStruct((nt, HQ, TT), BF16),
            jax.ShapeDtypeStruct((nt, NSA_GROUPS, VROWS, TT), BF16),
            jax.ShapeDtypeStruct((nt, NSA_GROUPS, VROWS, TT), BF16),
            jax.ShapeDtypeStruct((nt, N_BRANCH * N_HEADS, TT), F32),
            jax.ShapeDtypeStruct((n, width), BF16),
            jax.ShapeDtypeStruct((n, width), BF16),
            jax.ShapeDtypeStruct((n, NSA_KV), F32),
            jax.ShapeDtypeStruct((n, NSA_KV), F32),
        ],
        compiler_params=_cparams(("parallel",)),
        name="nsa_inproj",
    )(x, wt, wn, bg, cost, sint, cosn, sinn)


def _pad_heads(w, n_heads):
    w3 = w.reshape(w.shape[0], n_heads, HEAD_DIM)
    return jnp.concatenate([w3, jnp.zeros_like(w3)], axis=-1).reshape(w.shape[0], n_heads * AUG)


def _nsa_inproj_weights(w_in, b_gate):
    cuts = [HQ + i * NSA_KV for i in range(7)]
    wq, wkc, wvc, wks, wvs, wkw, wvw, wg = jnp.split(w_in, cuts, axis=1)
    wt = jnp.concatenate([wq, wvs, wvw, wg], axis=1).T.astype(BF16)
    wn = jnp.concatenate([wkc, wvc, _pad_heads(wks, NSA_GROUPS), _pad_heads(wkw, NSA_GROUPS)],
                         axis=1).astype(BF16)
    return wt, wn, b_gate.reshape(-1, 1)


def _rope_tables(seq_len):
    inv = ROPE_THETA ** (-jnp.arange(0, HEAD_DIM, 2, dtype=F32) / HEAD_DIM)
    ang = jnp.arange(seq_len, dtype=F32)[:, None] * inv[None, :]
    ang = jnp.concatenate([ang, ang], axis=-1)
    cos, sin = jnp.cos(ang), jnp.sin(ang)
    seq_tiles = seq_len // TT

    def transposed(t):
        return t.reshape(seq_tiles, TT, HEAD_DIM).transpose(0, 2, 1)

    def natural(t):
        return jnp.tile(jnp.concatenate([t, jnp.zeros_like(t)], axis=1), (1, NSA_GROUPS))

    return transposed(cos), transposed(sin), natural(cos), natural(sin)


def _compress_body(hk_ref, hv_ref, pe_ref, w1k_ref, b1k_ref, w2k_ref, w1vt_ref, b1v_ref, w2vt_ref,
                   kc_ref, vct_ref):
    nc = hk_ref.shape[1]
    pe_top, pe_bot = pe_ref[0:1], pe_ref[1:2]
    hk = hk_ref[0]
    top = _dot((hk + pe_top).astype(BF16), w1k_ref[0])
    bot = _dot((hk + pe_bot).astype(BF16), w1k_ref[1])
    hid = _gelu_tanh(top + pltpu.roll(bot, nc - 1, axis=0) + b1k_ref[...])
    kc_ref[0] = _dot(hid.astype(BF16), w2k_ref[...]).astype(BF16)

    hv = hv_ref[0]
    top_t = _dot_nt(w1vt_ref[0], (hv + pe_top).astype(BF16))
    bot_t = _dot_nt(w1vt_ref[1], (hv + pe_bot).astype(BF16))
    hid_t = _gelu_tanh(top_t + pltpu.roll(bot_t, nc - 1, axis=1) + b1v_ref[...])
    vct_ref[0] = _dot(w2vt_ref[...], hid_t.astype(BF16)).astype(BF16)


def _compress(hk, hv, pe2, w1k, b1k, w2k, w1vt, b1v, w2vt):
    bg, nc, hw = hk.shape
    cst = lambda shape: pl.BlockSpec(shape, lambda i: (0,) * len(shape))
    return pl.pallas_call(
        _compress_body,
        grid=(bg,),
        in_specs=[
            pl.BlockSpec((1, nc, hw), lambda i: (i, 0, 0)),
            pl.BlockSpec((1, nc, hw), lambda i: (i, 0, 0)),
            cst((2, hw)),
            cst((2, hw, PHI_HIDDEN)), cst((1, PHI_HIDDEN)), cst((PHI_HIDDEN, HEAD_DIM)),
            cst((2, PHI_HIDDEN, hw)), cst((PHI_HIDDEN, 1)), cst((HEAD_DIM, PHI_HIDDEN)),
        ],
        out_specs=[pl.BlockSpec((1, nc, HEAD_DIM), lambda i: (i, 0, 0)),
                   pl.BlockSpec((1, HEAD_DIM, nc), lambda i: (i, 0, 0))],
        out_shape=[jax.ShapeDtypeStruct((bg, nc, HEAD_DIM), BF16),
                   jax.ShapeDtypeStruct((bg, HEAD_DIM, nc), BF16)],
        compiler_params=_cparams(("parallel",)),
        name="nsa_compress",
    )(hk, hv, pe2, w1k, b1k, w2k, w1vt, b1v, w2vt)


def _slab_max(s):
    return jnp.max(s.reshape(-1, 8, s.shape[-1]), axis=0)


def _slab_sum(p):
    return jnp.sum(p.reshape(-1, 8, p.shape[-1]), axis=0)


def _paired_sweep(nq, qi, q_of, k_tile, v_tile, bias_row, s_ref, c_ref):
    tpos = lax.broadcasted_iota(jnp.int32, (1, TT), 1)
    kpos = lax.broadcasted_iota(jnp.int32, (TT, 1), 0)
    causal = kpos <= tpos
    half = nq // 2
    slots = []
    for s in range(nq + 1):
        if s < half:
            lo = s <= qi
            slots.append((lo, jnp.where(lo, s, s - qi - 1), jnp.where(lo, 0, 1)))
        else:
            slots.append((False, s - qi - 1, 1))

    slot_max, bias = [], []
    for s, (lo, kv, x) in enumerate(slots):
        sc = _dot(k_tile(kv), q_of(x))
        if s == nq:
            sc = jnp.where(causal, sc, MASK_BIAS)
        s_ref[s] = sc
        b = bias_row(lo, kv)
        bias.append(b)
        slot_max.append(_slab_max(sc) if b is None else _slab_max(sc) + b)
        yield 'A'
    diag = jnp.where(causal, s_ref[qi], MASK_BIAS)
    s_ref[qi] = diag
    m_lo = _slab_max(diag)
    m_hi = slot_max[nq]
    for s in range(nq):
        if s < half:
            m_lo = jnp.maximum(m_lo, jnp.where(s < qi, slot_max[s], MASK_BIAS))
            m_hi = jnp.maximum(m_hi, jnp.where(s > qi, slot_max[s], MASK_BIAS))
        else:
            m_hi = jnp.maximum(m_hi, slot_max[s])
    m_lo = jnp.max(m_lo, axis=0, keepdims=True)
    m_hi = jnp.max(m_hi, axis=0, keepdims=True)
    yield 'M'

    acc_lo = jnp.zeros((OROWS, TT), F32)
    acc_hi = jnp.zeros((OROWS, TT), F32)
    for s, (lo, kv, x) in enumerate(slots):
        m_row = m_hi if lo is False else jnp.where(lo, m_lo, m_hi)
        if bias[s] is not None:
            m_row = m_row - bias[s]
        p = jnp.exp2(s_ref[s] - m_row)
        pv = _dot(v_tile(kv), p.astype(BF16))[0:OROWS]
        if lo is False:
            acc_hi = acc_hi + pv
        else:
            c_ref[s] = pv
        yield 'B'
    for s in range(half):
        lo, c = slots[s][0], c_ref[s]
        acc_lo = acc_lo + jnp.where(lo, c, 0.0)
        acc_hi = acc_hi + jnp.where(lo, 0.0, c)
    o_lo = acc_lo[0:HEAD_DIM] * (1.0 / acc_lo[HEAD_DIM:HEAD_DIM + 1])
    o_hi = acc_hi[0:HEAD_DIM] * (1.0 / acc_hi[HEAD_DIM:HEAD_DIM + 1])
    return o_lo, o_hi


def _interleave(sweeps, width=SWEEP_WIDTH):
    outs = [None] * len(sweeps)
    groups = [list(range(i, min(i + width, len(sweeps)))) for i in range(0, len(sweeps), width)]

    def step(group, until):
        done = True
        for i in group:
            if state[i] == until:
                continue
            try:
                state[i] = next(sweeps[i])
            except StopIteration as fin:
                outs[i] = fin.value
                state[i] = 'END'
            done = done and state[i] == until
        return done

    state = [None] * len(sweeps)
    while not step(groups[0], 'M'):
        pass
    for g, group in enumerate(groups):
        nxt = groups[g + 1] if g + 1 < len(groups) else []
        cur_done, nxt_done = False, not nxt
        while not (cur_done and nxt_done):
            if not nxt_done:
                nxt_done = step(nxt, 'M')
            if not cur_done:
                cur_done = step(group, 'END')
    return outs


def _nsa_side(n_sb, qt, clamp, x, qn_ref, qr_ref, kc_ref, vct_ref, ovt_ref, kw_ref, vw_ref, s_refs):
    t0 = qt * TT
    tpos = t0 + lax.broadcasted_iota(jnp.int32, (1, TT), 1)
    kpos0 = lax.broadcasted_iota(jnp.int32, (TT, 1), 0)
    nc = kc_ref.shape[1]

    kc = kc_ref[0]
    cmp_end = lax.broadcasted_iota(jnp.int32, (nc, 1), 0) * CMP_STRIDE + (CMP_LEN - 1)
    cmask = cmp_end <= tpos
    c_slots = [s_refs[r].at[6 + x, 0:nc] for r in range(NSA_REP)]
    maxima, inv_sums = [], []
    for r in range(NSA_REP):
        s = jnp.where(cmask, _dot(kc, qn_ref[0, r * HEAD_DIM:(r + 1) * HEAD_DIM, :]), -jnp.inf)
        c_slots[r][...] = s
        m = jnp.max(_slab_max(s), axis=0, keepdims=True)
        maxima.append(jnp.where(m == -jnp.inf, 0.0, m))
        yield
    for r in range(NSA_REP):
        e = jnp.exp2(c_slots[r][...] - maxima[r])
        c_slots[r][...] = e
        d = jnp.sum(_slab_sum(e), axis=0, keepdims=True)
        inv_sums.append(1.0 / jnp.where(d > 0, d, 1.0))
        yield
    psum = jnp.zeros((nc, TT), F32)
    o_cmp = []
    for r in range(NSA_REP):
        p = c_slots[r][...] * inv_sums[r]
        psum = psum + p
        o_cmp.append(_dot(vct_ref[0], p.astype(BF16)))
        yield
    p_hi = psum.astype(BF16)
    p_lo = (psum - p_hi.astype(F32)).astype(BF16)
    imp = _dot(ovt_ref[...], p_hi) + _dot(ovt_ref[...], p_lo)

    nrow = ovt_ref.shape[0]
    blk = lax.broadcasted_iota(jnp.int32, (nrow, 1), 0)
    cur = tpos // SEL_LEN
    forced = (blk == 0) | (blk == cur) | (blk == cur - 1)
    valid = blk * SEL_LEN <= tpos
    taken = -3.0
    score = jnp.where(forced, taken, jnp.where(valid, imp, -1.0))
    score = jnp.where(blk < n_sb, score, -2.0)
    blk_f = blk.astype(F32)

    j1 = jnp.maximum(qt - 1, 0) if clamp else qt - 1
    j2 = jnp.maximum(qt - 2, 0) if clamp else qt - 2
    in_window = tpos - (j2 * TT + kpos0) < WINDOW
    masks = (kpos0 <= tpos - t0,
             jnp.broadcast_to(qt >= 1, (TT, TT)) if clamp else None,
             ((qt >= 2) & in_window) if clamp else in_window)
    k_tiles = [kw_ref[0, pl.ds(pl.multiple_of(j * TT, TT), TT), :][:, 0:HEAD_DIM] for j in (qt, j1, j2)]
    v_tiles = [vw_ref[0, j] for j in (qt, j1, j2)]
    w_max, o_win = [], []

    def window_scores(r):
        q = qr_ref[0, r * HEAD_DIM:(r + 1) * HEAD_DIM, :]
        maxima = []
        for t, (k, mask) in enumerate(zip(k_tiles, masks)):
            s = _dot(k, q)
            s = s if mask is None else jnp.where(mask, s, MASK_BIAS)
            s_refs[r][3 * x + t] = s
            maxima.append(_slab_max(s))
        w_max.append(jnp.max(functools.reduce(jnp.maximum, maxima), axis=0, keepdims=True))

    def window_output(r):
        acc = sum(_dot(v, jnp.exp2(s_refs[r][3 * x + t] - w_max[r]).astype(BF16))[0:OROWS]
                  for t, v in enumerate(v_tiles))
        o_win.append(acc[0:HEAD_DIM] * (1.0 / acc[HEAD_DIM:HEAD_DIM + 1]))

    window_steps = ([functools.partial(window_scores, r) for r in range(NSA_REP)]
                    + [functools.partial(window_output, r) for r in range(NSA_REP)])
    n_round = max(min(N_SEL, n_sb) - 3, 0)
    per_step = max(n_round // len(window_steps), 1)

    def over_blocks(pair_op, v):
        slabs = v.reshape(-1, 8, TT)
        v = functools.reduce(pair_op, [slabs[i] for i in range(slabs.shape[0])])
        for shift in (4, 2, 1):
            v = pair_op(v, pltpu.roll(v, shift, axis=0))
        return jnp.tile(v, (nrow // 8, 1))

    for i in range(n_round):
        top = over_blocks(jnp.maximum, score)
        first = over_blocks(jnp.minimum, jnp.where(score == top, blk_f, float(nrow)))
        score = jnp.where(blk_f == first, taken, score)
        if i % per_step == per_step - 1 and window_steps:
            window_steps.pop(0)()
        yield
    while window_steps:
        window_steps.pop(0)()
    sel_bias = jnp.where(score == taken, 0.0, MASK_BIAS).astype(BF16)
    return o_cmp, o_win, sel_bias


def _nsa_attn_body(nq, n_sb, qnl_ref, qnh_ref, qrl_ref, qrh_ref, kc_ref, vct_ref, ovt_ref, ks_ref, vs_ref,
                   kw_ref, vw_ref, gl_ref, gh_ref, olo_ref, ohi_ref, q_ref, *scratch):
    qi = pl.program_id(2)
    s_refs, c_refs = scratch[:NSA_REP], scratch[NSA_REP:]
    side_in = ((qi, qnl_ref, qrl_ref), (nq - 1 - qi, qnh_ref, qrh_ref))
    gens = [_nsa_side(n_sb, qt, x == 0 or nq < 6, x, qn_ref, qr_ref, kc_ref, vct_ref, ovt_ref, kw_ref, vw_ref,
                      s_refs) for x, (qt, qn_ref, qr_ref) in enumerate(side_in)]
    sides = [None, None]
    while None in sides:
        for x, gen in enumerate(gens):
            if sides[x] is None:
                try:
                    next(gen)
                except StopIteration as done:
                    sides[x] = done.value
    for x, (_, _, qr_ref) in enumerate(side_in):
        for r in range(NSA_REP):
            q_ref[x, r, 0:HEAD_DIM, :] = qr_ref[0, r * HEAD_DIM:(r + 1) * HEAD_DIM, :]
            q_ref[x, r, HEAD_DIM:AUG, :] = sides[x][2]

    k_tile = lambda kv: ks_ref[0, pl.ds(pl.multiple_of(kv * TT, TT), TT), :]
    v_tile = lambda kv: vs_ref[0, kv]
    o_sel = _interleave([_paired_sweep(nq, qi, lambda x, r=r: q_ref[x, r], k_tile, v_tile,
                                       lambda lo, kv: None, s_refs[r], c_refs[r]) for r in range(NSA_REP)])

    for x, (g_ref, o_ref) in enumerate(((gl_ref, olo_ref), (gh_ref, ohi_ref))):
        o_cmp, o_win, _ = sides[x]
        g = g_ref[0, 0]
        rows = []
        for r in range(NSA_REP):
            gc = g[r * N_BRANCH + 0:r * N_BRANCH + 1]
            gs = g[r * N_BRANCH + 1:r * N_BRANCH + 2]
            gw = g[r * N_BRANCH + 2:r * N_BRANCH + 3]
            rows.append(gc * o_cmp[r] + gs * o_sel[r][x] + gw * o_win[r])
        o_ref[0] = jnp.concatenate(rows, axis=0).T.astype(BF16)


def _nsa_attn(batch, seq_len, qn, qr, kc, vct, ovt, ksa, vs, kwa, vw, gt):
    nq = seq_len // TT
    nc = kc.shape[1]
    n_sb = seq_len // SEL_LEN
    g_rows = NSA_REP * N_BRANCH
    half = nq // 2
    ksa3 = ksa.reshape(batch, seq_len, NSA_GROUPS * AUG)
    kwa3 = kwa.reshape(batch, seq_len, NSA_GROUPS * AUG)
    vs5 = vs.reshape(batch, nq, NSA_GROUPS, VROWS, TT)
    vw5 = vw.reshape(batch, nq, NSA_GROUPS, VROWS, TT)
    gt4 = gt.reshape(batch * nq, NSA_GROUPS, g_rows, TT)
    lo_tile = lambda b, q: b * nq + q
    hi_tile = lambda b, q: b * nq + nq - 1 - q
    q_lo = pl.BlockSpec((1, NSA_REP * HEAD_DIM, TT), lambda b, g, q: (lo_tile(b, q), g, 0))
    q_hi = pl.BlockSpec((1, NSA_REP * HEAD_DIM, TT), lambda b, g, q: (hi_tile(b, q), g, 0))
    kspec = pl.BlockSpec((1, seq_len, AUG), lambda b, g, q: (b, 0, g))
    vspec = pl.BlockSpec((1, nq, None, VROWS, TT), lambda b, g, q: (b, 0, g, 0, 0))
    out = jax.ShapeDtypeStruct((batch, seq_len // 2, HQ), BF16)
    return pl.pallas_call(
        functools.partial(_nsa_attn_body, nq, n_sb),
        grid=(batch, NSA_GROUPS, half),
        in_specs=[
            q_lo, q_hi, q_lo, q_hi,
            pl.BlockSpec((1, nc, HEAD_DIM), lambda b, g, q: (b * NSA_GROUPS + g, 0, 0)),
            pl.BlockSpec((1, HEAD_DIM, nc), lambda b, g, q: (b * NSA_GROUPS + g, 0, 0)),
            pl.BlockSpec(ovt.shape, lambda b, g, q: (0, 0)),
            kspec, vspec, kspec, vspec,
            pl.BlockSpec((1, 1, g_rows, TT), lambda b, g, q: (lo_tile(b, q), g, 0, 0)),
            pl.BlockSpec((1, 1, g_rows, TT), lambda b, g, q: (hi_tile(b, q), g, 0, 0)),
        ],
        out_specs=[pl.BlockSpec((1, TT, NSA_REP * HEAD_DIM), lambda b, g, q: (b, q, g)),
                   pl.BlockSpec((1, TT, NSA_REP * HEAD_DIM), lambda b, g, q: (b, half - 1 - q, g))],
        out_shape=[out, out],
        scratch_shapes=([pltpu.VMEM((2, NSA_REP, AUG, TT), BF16)]
                        + [pltpu.VMEM((nq + 1, TT, TT), F32)] * NSA_REP
                        + [pltpu.VMEM((half, OROWS, TT), F32)] * NSA_REP),
        compiler_params=_cparams(("parallel", "parallel", "arbitrary")),
        name="nsa_attn",
    )(qn, qn, qr, qr, kc, vct, ovt, ksa3, vs5, kwa3, vw5, gt4, gt4)


def _overlap_t(seq_len, nc):
    n_cmp = (seq_len - CMP_LEN) // CMP_STRIDE + 1
    n_sb = seq_len // SEL_LEN
    cmp_start = jnp.arange(n_cmp) * CMP_STRIDE
    sel_start = jnp.arange(n_sb) * SEL_LEN
    ov = jnp.clip(jnp.minimum(cmp_start[:, None] + CMP_LEN, sel_start[None, :] + SEL_LEN)
                  - jnp.maximum(cmp_start[:, None], sel_start[None, :]), 0, None).astype(F32) / CMP_LEN
    out = jnp.zeros((SEL_LEN, nc), F32).at[:n_sb, :n_cmp].set(ov.T)
    return out.astype(BF16)


def _nsa_attention(x, batch, seq_len, w_in, b_gate, pe, k_w1, k_b1, k_w2, v_w1, v_b1, v_w2, tables):
    wt, wn, bg = _nsa_inproj_weights(w_in, b_gate)
    qn, qr, vs, vw, gt, ksa, kwa, kc, vc = _nsa_inproj(x, seq_len, wt, wn, bg, *tables)

    nc = seq_len // CMP_STRIDE
    hw = CMP_STRIDE * HEAD_DIM

    def half_blocks(t):
        t = t.reshape(batch, seq_len, NSA_GROUPS, HEAD_DIM).transpose(0, 2, 1, 3)
        return t.reshape(batch * NSA_GROUPS, nc, hw)

    kcmp, vcmp_t = _compress(
        half_blocks(kc), half_blocks(vc), pe.reshape(2, hw),
        k_w1.reshape(2, hw, PHI_HIDDEN).astype(BF16), k_b1.reshape(1, -1), k_w2.astype(BF16),
        v_w1.reshape(2, hw, PHI_HIDDEN).transpose(0, 2, 1).astype(BF16), v_b1.reshape(-1, 1),
        v_w2.T.astype(BF16))
    return _nsa_attn(batch, seq_len, qn, qr, kcmp, vcmp_t, _overlap_t(seq_len, nc), ksa, vs, kwa, vw, gt)


def _nsa_layer(x, batch, seq_len, w_in, b_gate, pe, k_w1, k_b1, k_w2, v_w1, v_b1, v_w2, w_o,
               ln_g, ln_b, tables):
    a_lo, a_hi = _nsa_attention(x, batch, seq_len, w_in, b_gate, pe, k_w1, k_b1, k_w2, v_w1, v_b1, v_w2,
                                tables)
    return _proj_ln(a_lo, a_hi, w_o.astype(BF16), x, ln_g.reshape(1, -1), ln_b.reshape(1, -1))


_FOX_T_ROWS = 2 * HQ + N_HEADS
_FOX_N_COLS = N_HEADS * AUG + 128
_N_PIECE = 3


def _fox_inproj_body(seq_tiles, x_ref, wt_ref, wn_ref, bft_ref, bfn_ref, place_ref, ones_ref, route_ref,
                     qa_ref, ka_ref, vt_ref, off_ref, run_ref):
    i = pl.program_id(0)
    xb = x_ref[...].astype(BF16)
    t = _dot_nt(wt_ref[...], xb)
    n = _dot(xb, wn_ref[...])
    vt_ref[0, :, 0:HEAD_DIM, :] = t[HQ:2 * HQ].reshape(N_HEADS, HEAD_DIM, TT).astype(BF16)
    vt_ref[0, :, HEAD_DIM:VROWS, :] = jnp.ones((N_HEADS, VROWS - HEAD_DIM, TT), BF16)
    lf_t = _log_sigmoid(t[2 * HQ:] + bft_ref[...]) * LOG2E
    kw = N_HEADS * AUG
    lf_n = _log_sigmoid(n[:, kw:kw + _N_PIECE * N_HEADS] + bfn_ref[...]) * LOG2E

    r_i = lax.broadcasted_iota(jnp.int32, (TT, TT), 0)
    c_i = lax.broadcasted_iota(jnp.int32, (TT, TT), 1)
    upper = ((r_i > 0) & (r_i <= c_i)).astype(BF16)
    lower = ((c_i > 0) & (c_i <= r_i)).astype(BF16)
    a_t = sum(_dot(p, upper) for p in _split3(lf_t))
    a_n = sum(_dot(lower, p) for p in _split3(lf_n))

    q3 = (t[0:HQ] * QSCALE).reshape(N_HEADS, HEAD_DIM, TT)
    qa_ref[0, :, 0:HEAD_DIM, :] = q3.astype(BF16)
    stacked = jnp.concatenate(list(_split3(a_t)) + [jnp.ones((N_HEADS, TT), BF16)], axis=0)
    for h in range(N_HEADS):
        qa_ref[0, h, HEAD_DIM:AUG, :] = _dot(route_ref[h], stacked).astype(BF16)

    b1, b2, b3 = _split3(-a_n)
    grp = lax.broadcasted_iota(jnp.int32, (TT, _N_PIECE * N_HEADS), 1) // N_HEADS
    bsel = jnp.where(grp == 0, b1, jnp.where(grp == 1, b2, b3))
    ka_ref[...] = (n[:, 0:kw] + _dot(bsel, place_ref[...]) + ones_ref[...]).astype(BF16)

    @pl.when(i % seq_tiles == 0)
    def _():
        run_ref[...] = jnp.zeros_like(run_ref)

    first = lf_t[:, 0:1]
    off_ref[0] = jnp.broadcast_to(run_ref[:, 0:1] + first, (N_HEADS, TT))
    run_ref[...] = run_ref[...] + (a_t[:, TT - 1:TT] + first)


def _fox_inproj(x, seq_len, wt, wn, bft, bfn, place, ones, route):
    n = x.shape[0]
    nt = n // TT
    kw = N_HEADS * AUG
    full = lambda r, c: pl.BlockSpec((r, c), lambda i: (0, 0))
    return pl.pallas_call(
        functools.partial(_fox_inproj_body, seq_len // TT),
        grid=(nt,),
        in_specs=[
            pl.BlockSpec((TT, D_MODEL), lambda i: (i, 0)),
            full(_FOX_T_ROWS, D_MODEL), full(D_MODEL, _FOX_N_COLS),
            full(N_HEADS, 1), full(1, _N_PIECE * N_HEADS),
            full(_N_PIECE * N_HEADS, kw), full(1, kw),
            pl.BlockSpec((N_HEADS, AUG - HEAD_DIM, AUG - HEAD_DIM), lambda i: (0, 0, 0)),
        ],
        out_specs=[
            pl.BlockSpec((1, N_HEADS, AUG, TT), lambda i: (i, 0, 0, 0)),
            pl.BlockSpec((TT, kw), lambda i: (i, 0)),
            pl.BlockSpec((1, N_HEADS, VROWS, TT), lambda i: (i, 0, 0, 0)),
            pl.BlockSpec((1, N_HEADS, TT), lambda i: (i, 0, 0)),
        ],
        out_shape=[
            jax.ShapeDtypeStruct((nt, N_HEADS, AUG, TT), BF16),
            jax.ShapeDtypeStruct((n, kw), BF16),
            jax.ShapeDtypeStruct((nt, N_HEADS, VROWS, TT), BF16),
            jax.ShapeDtypeStruct((nt, N_HEADS, TT), F32),
        ],
        scratch_shapes=[pltpu.VMEM((N_HEADS, 128), F32)],
        compiler_params=_cparams(("arbitrary",)),
        name="fox_inproj",
    )(x, wt, wn, bft, bfn, place, ones, route)


def _fox_inproj_weights(w_in, b_f):
    wq, wk, wv, wf = jnp.split(w_in, [HQ, 2 * HQ, 3 * HQ], axis=1)
    wt = jnp.concatenate([wq, wv, wf], axis=1).T.astype(BF16)
    pad = jnp.zeros((D_MODEL, _FOX_N_COLS - N_HEADS * AUG - _N_PIECE * N_HEADS), F32)
    wn = jnp.concatenate([_pad_heads(wk, N_HEADS)] + [wf] * _N_PIECE + [pad], axis=1).astype(BF16)
    bft = b_f.reshape(-1, 1)
    bfn = jnp.tile(b_f, _N_PIECE).reshape(1, -1)
    rows = jnp.arange(_N_PIECE * N_HEADS)
    cols = (rows % N_HEADS) * AUG + HEAD_DIM + _N_PIECE + rows // N_HEADS
    place = jnp.zeros((_N_PIECE * N_HEADS, N_HEADS * AUG), F32).at[rows, cols].set(1.0).astype(BF16)
    lane = jnp.arange(N_HEADS * AUG) % AUG
    ones = ((lane >= HEAD_DIM) & (lane < HEAD_DIM + _N_PIECE)).astype(F32).reshape(1, -1)
    hh = jnp.arange(N_HEADS)
    route = jnp.zeros((N_HEADS, AUG - HEAD_DIM, AUG - HEAD_DIM), F32)
    for k in range(_N_PIECE):
        route = route.at[hh, k, k * N_HEADS + hh].set(1.0)
        route = route.at[hh, _N_PIECE + k, _N_PIECE * N_HEADS].set(1.0)
    return wt, wn, bft, bfn, place, ones, route.astype(BF16)


def _fox_attn_body(nq, qlo_ref, qhi_ref, ka_ref, vt_ref, off_ref, olo_ref, ohi_ref, q_ref, *scratch):
    hg = pl.program_id(1)
    qi = pl.program_id(2)
    s_refs, c_refs = scratch[:N_SCORE_BUF], scratch[N_SCORE_BUF:]
    q_ref[0] = qlo_ref[0]
    q_ref[1] = qhi_ref[0]
    sweeps = []
    for r in range(HEAD_BLOCK):
        h = hg * HEAD_BLOCK + r

        def off_row(t, h=h):
            return off_ref[0, pl.ds(t, 1), pl.ds(h, 1), :].reshape(1, TT)

        off_lo, off_hi = off_row(qi), off_row(nq - 1 - qi)

        def bias_row(lo, kv, off_lo=off_lo, off_hi=off_hi, off_row=off_row):
            base = off_hi if lo is False else jnp.where(lo, off_lo, off_hi)
            return base - off_row(kv)

        def k_tile(kv, r=r):
            return ka_ref[0, pl.ds(pl.multiple_of(kv * TT, TT), TT), r * AUG:(r + 1) * AUG]

        def v_tile(kv, r=r):
            return vt_ref[0, kv, r]

        sweeps.append(_paired_sweep(nq, qi, lambda x, r=r: q_ref[x, r], k_tile, v_tile, bias_row,
                                    s_refs[r % N_SCORE_BUF], c_refs[r]))
    outs = _interleave(sweeps)
    for x, o_ref in enumerate((olo_ref, ohi_ref)):
        o_ref[0] = jnp.concatenate([o[x] for o in outs], axis=0).T.astype(BF16)


def _fox_attn(batch, seq_len, qa, ka, vt, off):
    nq = seq_len // TT
    half = nq // 2
    hb = HEAD_BLOCK
    ka3 = ka.reshape(batch, seq_len, N_HEADS * AUG)
    vt5 = vt.reshape(batch, nq, N_HEADS, VROWS, TT)
    off4 = off.reshape(batch, nq, N_HEADS, TT)
    out = jax.ShapeDtypeStruct((batch, seq_len // 2, HQ), BF16)
    return pl.pallas_call(
        functools.partial(_fox_attn_body, nq),
        grid=(batch, N_HEADS // hb, half),
        in_specs=[
            pl.BlockSpec((1, hb, AUG, TT), lambda b, h, q: (b * nq + q, h, 0, 0)),
            pl.BlockSpec((1, hb, AUG, TT), lambda b, h, q: (b * nq + nq - 1 - q, h, 0, 0)),
            pl.BlockSpec((1, seq_len, hb * AUG), lambda b, h, q: (b, 0, h)),
            pl.BlockSpec((1, nq, hb, VROWS, TT), lambda b, h, q: (b, 0, h, 0, 0)),
            pl.BlockSpec((1, nq, N_HEADS, TT), lambda b, h, q: (b, 0, 0, 0)),
        ],
        out_specs=[pl.BlockSpec((1, TT, hb * HEAD_DIM), lambda b, h, q: (b, q, h)),
                   pl.BlockSpec((1, TT, hb * HEAD_DIM), lambda b, h, q: (b, half - 1 - q, h))],
        out_shape=[out, out],
        scratch_shapes=([pltpu.VMEM((2, hb, AUG, TT), BF16)]
                        + [pltpu.VMEM((nq + 1, TT, TT), F32)] * N_SCORE_BUF
                        + [pltpu.VMEM((half, OROWS, TT), F32)] * hb),
        compiler_params=_cparams(("parallel", "parallel", "arbitrary")),
        name="fox_attn",
    )(qa, qa, ka3, vt5, off4)


def _fox_layer(x, batch, seq_len, w_in, b_f, w_o, ln_g, ln_b):
    qa, ka, vt, off = _fox_inproj(x, seq_len, *_fox_inproj_weights(w_in, b_f))
    a_lo, a_hi = _fox_attn(batch, seq_len, qa, ka, vt, off)
    return _proj_ln(a_lo, a_hi, w_o.astype(BF16), x, ln_g.reshape(1, -1), ln_b.reshape(1, -1))


def kernel(x, nsa_w_in, nsa_b_gate, nsa_pe, nsa_phik_w1, nsa_phik_b1, nsa_phik_w2, nsa_phiv_w1,
           nsa_phiv_b1, nsa_phiv_w2, nsa_w_o, fox_w_in, fox_b_f, fox_w_o, ffn_w_up, ffn_conv_w,
           ffn_conv_b, ffn_w_down, ln1_g, ln1_b, ln2_g, ln2_b):
    batch, seq_len, d = x.shape
    assert d == D_MODEL and seq_len % (2 * PROJ_TM) == 0 and seq_len % FFN_TM == 0 and seq_len % (2 * TT) == 0
    assert seq_len // SEL_LEN <= AUG - HEAD_DIM
    tables = _rope_tables(seq_len)
    h = x.reshape(batch * seq_len, d)
    for i in range(DEPTH):
        j = i // 2
        if i % 2 == 0:
            h = _nsa_layer(h, batch, seq_len, nsa_w_in[j], nsa_b_gate[j], nsa_pe[j], nsa_phik_w1[j],
                           nsa_phik_b1[j], nsa_phik_w2[j], nsa_phiv_w1[j], nsa_phiv_b1[j],
                           nsa_phiv_w2[j], nsa_w_o[j], ln1_g[i], ln1_b[i], tables)
        else:
            h = _fox_layer(h, batch, seq_len, fox_w_in[j], fox_b_f[j], fox_w_o[j], ln1_g[i], ln1_b[i])
        h = _ffn(h, seq_len, *_ffn_weights(ffn_w_up[i], ffn_conv_w[i], ffn_conv_b[i], ffn_w_down[i]),
                 ln2_g[i].reshape(1, -1), ln2_b[i].reshape(1, -1))
    return h.reshape(batch, seq_len, d)
```
